```python
import math
import jax
import jax.numpy as jnp
from jax import lax
import numpy as np

D_MODEL = 1024
BATCH = 4
SEQ = 4096
DEPTH = 4
DEC_BATCH = 128
DEC_SEQ = 8
PAST_LEN = 2048
PAGE_SIZE = 128

N_MIXERS = 3
N_A = (DEPTH + 2) // 3
N_B = (DEPTH + 1) // 3
N_C = DEPTH // 3
D_A = 2 * D_MODEL
N_GROUPS_A = 8
CHUNK_A = 128
HEAD_DIM_B = 128
N_HEADS_B = D_MODEL // HEAD_DIM_B
BLOCK_B = 256
TOPK_B = 3
Q_CHUNK = 32
N_BUCKETS = 32
MAX_DIST = 1024
D_C = D_MODEL
CONV_W = 3
D_FF = 4 * D_MODEL
EPS = 1e-6
MASK_VALUE = -1e30

kernel_name = 'hybrid_gmlp_moba_shortconv_decode_step'


def rms_norm(x, g):
    xf = x.astype(jnp.float32)
    y = xf * lax.rsqrt(jnp.mean(xf * xf, axis=-1, keepdims=True) + EPS)
    return (y * g.astype(jnp.float32)).astype(x.dtype)


def ada_modulation(c, w_mod, b_mod):
    m = jnp.einsum('bd,de->be', jax.nn.silu(c), w_mod) + b_mod
    return jnp.split(m[:, None, :], 6, axis=-1)


def modulate(x, g, shift, scale):
    return rms_norm(x, g) * (1 + scale) + shift


def sqrelu_mlp(h, w_up, w_down):
    a = jax.nn.relu(jnp.einsum('btd,df->btf', h, w_up))
    return jnp.einsum('btf,fd->btd', a * a, w_down)


def spatial_gate(u, v, w_s, b_s):
    L = v.shape[2]
    vg = v.reshape(v.shape[:3] + (N_GROUPS_A, D_A // N_GROUPS_A))
    w = jnp.tril(w_s[:, :L, :L])
    s = jnp.einsum('gij,bcjgd->bcigd', w, vg) + jnp.transpose(b_s[:, :L])[None, None, :, :, None]
    return u * s.reshape(v.shape)


def gmlp_mixer(h, w_in, v_gain, w_s, b_s, w_out, chunk_len):
    B, T, _ = h.shape
    z = jax.nn.gelu(jnp.einsum('btd,de->bte', h, w_in))
    u, v = jnp.split(z, 2, axis=-1)
    v = rms_norm(v, v_gain)
    n_ch = T // chunk_len
    g = spatial_gate(u.reshape(B, n_ch, chunk_len, D_A), v.reshape(B, n_ch, chunk_len, D_A), w_s, b_s)
    y = jnp.einsum('bte,ed->btd', g.reshape(B, T, D_A), w_out)
    return y, v


def short_conv_mixer(h, conv_state, w_in, conv_w, w_out):
    T = h.shape[1]
    bcx = jnp.einsum('btd,de->bte', h, w_in)
    gate_out, gate_in, xv = jnp.split(bcx, 3, axis=-1)
    xin = gate_in * xv
    xs = jnp.concatenate([conv_state.astype(xin.dtype), xin], axis=1)
    y = conv_w[0] * xs[:, 0:T]
    for kk in range(1, CONV_W):
        y = y + conv_w[kk] * xs[:, kk:kk + T]
    out = jnp.einsum('bte,ed->btd', gate_out * y, w_out)
    return out, xs[:, T:]


def rel_bucket(dist):
    max_exact = N_BUCKETS // 2
    d = jnp.maximum(dist, 0)
    log_ratio = jnp.log(jnp.maximum(d, 1).astype(jnp.float32) / max_exact) / math.log(MAX_DIST / max_exact)
    large = jnp.minimum(max_exact + (log_ratio * (N_BUCKETS - max_exact)).astype(jnp.int32), N_BUCKETS - 1)
    return jnp.where(d < max_exact, d, large)


def moba_attention(q, q_pos, k_mean, fetch_kv, rel_bias):
    Bq, Tq, H, hd = q.shape
    n_gate = k_mean.shape[1]
    own = q_pos // BLOCK_B
    gate = jnp.einsum('bthd,bnhd->bthn', q.astype(jnp.float32), k_mean.astype(jnp.float32))
    eligible = jnp.arange(n_gate)[None, :] < own[:, None]
    gate = jnp.where(eligible[None, :, None, :], gate, MASK_VALUE)
    kk = min(TOPK_B, n_gate)
    _, top_idx = lax.top_k(gate, kk)
    own_b = jnp.broadcast_to(own[None, :, None, None], (Bq, Tq, H, 1))
    sel = jnp.concatenate([top_idx.astype(jnp.int32), own_b.astype(jnp.int32)], axis=-1)
    valid = jnp.concatenate([jnp.arange(kk)[None, :] < own[:, None], jnp.ones((Tq, 1), bool)], axis=-1)
    n_sel = kk + 1
    R = Bq * Tq
    n_chunks = -(-R // Q_CHUNK)
    pad = n_chunks * Q_CHUNK - R

    def rows(a):
        a = a.reshape((R,) + a.shape[2:])
        a = jnp.pad(a, [(0, pad)] + [(0, 0)] * (a.ndim - 1))
        return a.reshape((n_chunks, Q_CHUNK) + a.shape[1:])

    q_rows = rows(q)
    b_rows = rows(jnp.broadcast_to(jnp.arange(Bq, dtype=jnp.int32)[:, None], (Bq, Tq)))
    t_rows = rows(jnp.broadcast_to(q_pos[None, :], (Bq, Tq)))
    sel_rows = rows(sel)
    valid_rows = rows(jnp.broadcast_to(valid[None], (Bq, Tq, n_sel)))
    h_idx = jnp.arange(H, dtype=jnp.int32)[None, :, None, None]
    offs = jnp.arange(BLOCK_B, dtype=jnp.int32)
    scale = 1.0 / math.sqrt(hd)

    def step(args):
        qc, bc, tc, selc, validc = args
        kpos = selc[..., None] * BLOCK_B + offs
        kg, vg = fetch_kv(bc[:, None, None, None], kpos, h_idx)
        logits = jnp.einsum('rhd,rhnkd->rhnk', qc.astype(jnp.float32), kg.astype(jnp.float32)) * scale
        dist = tc[:, None, None, None] - kpos
        bias = rel_bias[rel_bucket(dist), h_idx].astype(jnp.float32)
        mask = validc[:, None, :, None] & (dist >= 0)
        logits = jnp.where(mask, logits + bias, MASK_VALUE)
        p = jax.nn.softmax(logits.reshape(Q_CHUNK, H, n_sel * BLOCK_B), axis=-1).reshape(logits.shape)
        return jnp.einsum('rhnk,rhnkd->rhd', p, vg.astype(jnp.float32)).astype(q.dtype)

    o = lax.map(step, (q_rows, b_rows, t_rows, sel_rows, valid_rows))
    return o.reshape(n_chunks * Q_CHUNK, H, hd)[:R].reshape(Bq, Tq, H, hd)


def moba_qkv(h, w_qkv, q_gain, k_gain):
    B, T, _ = h.shape
    qkv = jnp.einsum('btd,de->bte', h, w_qkv).reshape(B, T, 3, N_HEADS_B, HEAD_DIM_B)
    q = rms_norm(qkv[:, :, 0], q_gain)
    k = rms_norm(qkv[:, :, 1], k_gain)
    return q, k, qkv[:, :, 2]


def moba_prompt(h, w_qkv, q_gain, k_gain, w_out, rel_bias):
    B, S, _ = h.shape
    q, k, v = moba_qkv(h, w_qkv, q_gain, k_gain)
    n_full = (S - 1) // BLOCK_B
    if n_full > 0:
        k_mean = jnp.mean(k[:, :n_full * BLOCK_B].astype(jnp.float32).reshape(
            B, n_full, BLOCK_B, N_HEADS_B, HEAD_DIM_B), axis=2)
    else:
        k_mean = jnp.zeros((B, 1, N_HEADS_B, HEAD_DIM_B), jnp.float32)

    def fetch(b, p, hh):
        p = jnp.clip(p, 0, S - 1)
        return k[b, p, hh], v[b, p, hh]

    o = moba_attention(q, jnp.arange(S, dtype=jnp.int32), k_mean, fetch, rel_bias)
    y = jnp.einsum('bte,ed->btd', o.reshape(B, S, N_HEADS_B * HEAD_DIM_B), w_out)
    return y, k, v


def moba_sample(h, cache_k, cache_v, page_table, w_qkv, q_gain, k_gain, w_out, rel_bias):
    DB, T, _ = h.shape
    n_pages = page_table.shape[1]
    q, k, v = moba_qkv(h, w_qkv, q_gain, k_gain)
    q_pos = PAST_LEN + jnp.arange(T, dtype=jnp.int32)
    n_gate = max((PAST_LEN + T - 1) // BLOCK_B, 1)
    blk_ids = jnp.arange(n_gate, dtype=jnp.int32)
    page_sum = jnp.sum(cache_k[page_table].astype(jnp.float32), axis=2)
    page_blk = (jnp.arange(n_pages, dtype=jnp.int32) * PAGE_SIZE) // BLOCK_B
    onehot_p = (page_blk[:, None] == blk_ids[None, :]).astype(jnp.float32)
    onehot_t = ((q_pos // BLOCK_B)[:, None] == blk_ids[None, :]).astype(jnp.float32)
    k_mean = (jnp.einsum('bphd,pn->bnhd', page_sum, onehot_p)
              + jnp.einsum('bthd,tn->bnhd', k.astype(jnp.float32), onehot_t)) / BLOCK_B

    def fetch(b, p, hh):
        is_past = (p < PAST_LEN)[..., None]
        pp = jnp.clip(p, 0, PAST_LEN - 1)
        phys = page_table[b, pp // PAGE_SIZE]
        row = pp % PAGE_SIZE
        pn = jnp.clip(p - PAST_LEN, 0, T - 1)
        kk = jnp.where(is_past, cache_k[phys, row, hh], k[b, pn, hh])
        vv = jnp.where(is_past, cache_v[phys, row, hh], v[b, pn, hh])
        return kk, vv

    o = moba_attention(q, q_pos, k_mean, fetch, rel_bias)
    y = jnp.einsum('bte,ed->btd', o.reshape(DB, T, N_HEADS_B * HEAD_DIM_B), w_out)
    return y, k, v


def setup_inputs(seed: int = 0) -> dict:
    key = jax.random.key(seed)
    ks = jax.random.split(key, 32)
    f32 = jnp.float32
    H, hd = N_HEADS_B, HEAD_DIM_B
    n_pages = PAST_LEN // PAGE_SIZE
    n_pool = (DEC_BATCH * n_pages * 5) // 4

    def w(k, shape, fan_in, s=1.0):
        return jax.random.normal(k, shape, f32) * (s * fan_in ** -0.5)

    def gain(k, shape):
        return 1.0 + 0.05 * jax.random.normal(k, shape, f32)

    page_table = jax.random.permutation(ks[0], n_pool)[:DEC_BATCH * n_pages].reshape(
        DEC_BATCH, n_pages).astype(jnp.int32)
    return {
        'x_prompt': jax.random.normal(ks[1], (BATCH, SEQ, D_MODEL), f32),
        'x_sample': jax.random.normal(ks[2], (DEC_BATCH, DEC_SEQ, D_MODEL), f32),
        'cache_k': jax.random.normal(ks[3], (N_B, n_pool, PAGE_SIZE, H, hd), f32),
        'cache_v': jax.random.normal(ks[4], (N_B, n_pool, PAGE_SIZE, H, hd), f32),
        'state_conv': jax.random.normal(ks[5], (N_C, DEC_BATCH, CONV_W - 1, D_C), f32),
        'page_table': page_table,
        'c_prompt': jax.random.normal(ks[6], (BATCH, D_MODEL), f32),
        'c_sample': jax.random.normal(ks[7], (DEC_BATCH, D_MODEL), f32),
        'rel_bias': 0.1 * jax.random.normal(ks[8], (N_BUCKETS, H), f32),
        'norm_mix': gain(ks[9], (DEPTH, D_MODEL)),
        'norm_mlp': gain(ks[10], (DEPTH, D_MODEL)),
        'w_mod': w(ks[11], (DEPTH, D_MODEL, 6 * D_MODEL), D_MODEL, 0.5),
        'b_mod': 0.02 * jax.random.normal(ks[12], (DEPTH, 6 * D_MODEL), f32),
        'w_up': w(ks[13], (DEPTH, D_MODEL, D_FF), D_MODEL),
        'w_down': w(ks[14], (DEPTH, D_FF, D_MODEL), D_FF),
        'a_w_in': w(ks[15], (N_A, D_MODEL, 2 * D_A), D_MODEL),
        'a_v_gain': gain(ks[16], (N_A, D_A)),
        'a_w_s': w(ks[17], (N_A, N_GROUPS_A, CHUNK_A, CHUNK_A), CHUNK_A),
        'a_b_s': gain(ks[18], (N_A, N_GROUPS_A, CHUNK_A)),
        'a_w_out': w(ks[19], (N_A, D_A, D_MODEL), D_A),
        'b_w_qkv': w(ks[20], (N_B, D_MODEL, 3 * H * hd), D_MODEL),
        'b_q_gain': gain(ks[21], (N_B, hd)),
        'b_k_gain': gain(ks[22], (N_B, hd)),
        'b_w_out': w(ks[23], (N_B, H * hd, D_MODEL), H * hd),
        'c_w_in': w(ks[24], (N_C, D_MODEL, 3 * D_C), D_MODEL),
        'c_conv': w(ks[25], (N_C, CONV_W, D_C), CONV_W),
        'c_w_out': w(ks[26], (N_C, D_C, D_MODEL), D_C),
    }


def reference(x_prompt, x_sample, cache_k, cache_v, state_conv, page_table, c_prompt, c_sample,
              rel_bias, norm_mix, norm_mlp, w_mod, b_mod, w_up, w_down,
              a_w_in, a_v_gain, a_w_s, a_b_s, a_w_out,
              b_w_qkv, b_q_gain, b_k_gain, b_w_out,
              c_w_in, c_conv, c_w_out):
    xp, xs = x_prompt, x_sample
    k_p, v_p, k_s, v_s, conv_p, conv_s, chunkv_s = [], [], [], [], [], [], []
    for i in range(DEPTH):
        kind, j = i % N_MIXERS, i // N_MIXERS
        mp = ada_modulation(c_prompt, w_mod[i], b_mod[i])
        ms = ada_modulation(c_sample, w_mod[i], b_mod[i])
        hp = modulate(xp, norm_mix[i], mp[0], mp[1])
        hs = modulate(xs, norm_mix[i], ms[0], ms[1])
        if kind == 0:
            yp, _ = gmlp_mixer(hp, a_w_in[j], a_v_gain[j], a_w_s[j], a_b_s[j], a_w_out[j], CHUNK_A)
            ys, v_new = gmlp_mixer(hs, a_w_in[j], a_v_gain[j], a_w_s[j], a_b_s[j], a_w_out[j], hs.shape[1])
            chunkv_s.append(v_new)
        elif kind == 1:
            yp, kp_new, vp_new = moba_prompt(hp, b_w_qkv[j], b_q_gain[j], b_k_gain[j], b_w_out[j], rel_bias)
            ys, ks_new, vs_new = moba_sample(hs, cache_k[j], cache_v[j], page_table, b_w_qkv[j],
                                             b_q_gain[j], b_k_gain[j], b_w_out[j], rel_bias)
            k_p.append(kp_new)
            v_p.append(vp_new)
            k_s.append(ks_new)
            v_s.append(vs_new)
        else:
            zero_state = jnp.zeros((hp.shape[0], CONV_W - 1, D_C), hp.dtype)
            yp, cp_new = short_conv_mixer(hp, zero_state, c_w_in[j], c_conv[j], c_w_out[j])
            ys, cs_new = short_conv_mixer(hs, state_conv[j], c_w_in[j], c_conv[j], c_w_out[j])
            conv_p.append(cp_new)
            conv_s.append(cs_new)
        xp = xp + mp[2] * yp
        xs = xs + ms[2] * ys
        hp = modulate(xp, norm_mlp[i], mp[3], mp[4])
        hs = modulate(xs, norm_mlp[i], ms[3], ms[4])
        xp = xp + mp[5] * sqrelu_mlp(hp, w_up[i], w_down[i])
        xs = xs + ms[5] * sqrelu_mlp(hs, w_up[i], w_down[i])
    return (xp, xs, jnp.stack(k_p), jnp.stack(v_p), jnp.stack(k_s), jnp.stack(v_s),
            jnp.stack(conv_p), jnp.stack(conv_s), jnp.stack(chunkv_s))
```

```python
import functools
import math

import jax
import jax.numpy as jnp
import numpy as np
from jax import lax
from jax.experimental import pallas as pl
from jax.experimental.pallas import tpu as pltpu

D_MODEL = 1024
DEPTH = 4
N_MIXERS = 3
D_A = 2 * D_MODEL
N_GROUPS_A = 8
GROUP_A = D_A // N_GROUPS_A
CHUNK_A = 128
HEAD_DIM_B = 128
N_HEADS_B = D_MODEL // HEAD_DIM_B
BLOCK_B = 256
TOPK_B = 3
N_BUCKETS = 32
MAX_DIST = 1024
D_C = D_MODEL
CONV_W = 3
D_FF = 4 * D_MODEL
EPS = 1e-6
MASK_VALUE = -1e30
PAGE_SIZE = 128

SUBLANES = 8
LANES = 128
VMEM_LIMIT_BYTES = 56 * 1024 * 1024

F32 = jnp.float32
BF16 = jnp.bfloat16
HIGHEST = lax.Precision.HIGHEST


def _cparams(*sem):
    return pltpu.CompilerParams(dimension_semantics=sem, vmem_limit_bytes=VMEM_LIMIT_BYTES)


def _rows(ref):
    v = ref[...]
    return v.reshape(v.shape[-2], v.shape[-1])


def _modulate(x, g, shift, scale):
    y = x * lax.rsqrt(jnp.mean(x * x, axis=-1, keepdims=True) + EPS)
    return (y * g) * (1.0 + scale) + shift


def _gelu_tanh(x):
    c = math.sqrt(2.0 / math.pi)
    return x * (0.5 * (1.0 + jnp.tanh(c * (x + 0.044715 * (x * x * x)))))


def _dot(a, b):
    return jnp.dot(a, b, preferred_element_type=F32)


def _dot_nt(a, b, **kw):
    return lax.dot_general(a, b, (((1,), (1,)), ((), ())), preferred_element_type=F32, **kw)


def _dot_tn(a, b):
    return lax.dot_general(a, b, (((0,), (0,)), ((), ())), preferred_element_type=F32)


def _mod_spec(per_row, tm, tiles_per_batch, ngrid):
    if per_row:
        if ngrid == 1:
            return pl.BlockSpec((tm, D_MODEL), lambda i: (i, 0))
        return pl.BlockSpec((tm, D_MODEL), lambda i, f: (i, 0))
    if ngrid == 1:
        return pl.BlockSpec((1, 1, D_MODEL), lambda i: (i // tiles_per_batch, 0, 0))
    return pl.BlockSpec((1, 1, D_MODEL), lambda i, f: (i // tiles_per_batch, 0, 0))


def _const_spec(shape):
    nd = len(shape)
    return pl.BlockSpec(shape, lambda i: (0,) * nd, pipeline_mode=pl.Buffered(1))


def _mod_kernel(c_ref, w_ref, b_ref, o_ref):
    c = c_ref[...]
    sc = (c * jax.nn.sigmoid(c)).astype(BF16)
    o_ref[0] = _dot(sc, w_ref[0].astype(BF16)) + b_ref[0]


def _ada_all(c_all, w_mod, b_mod):
    nrow = c_all.shape[0]
    tn = 1536
    nn = (6 * D_MODEL) // tn
    return pl.pallas_call(
        _mod_kernel,
        grid=(DEPTH, nn),
        in_specs=[
            pl.BlockSpec((nrow, D_MODEL), lambda l, n: (0, 0)),
            pl.BlockSpec((1, D_MODEL, tn), lambda l, n: (l, 0, n)),
            pl.BlockSpec((1, 1, tn), lambda l, n: (l, 0, n)),
        ],
        out_specs=pl.BlockSpec((1, nrow, tn), lambda l, n: (l, 0, n)),
        out_shape=jax.ShapeDtypeStruct((DEPTH, nrow, 6 * D_MODEL), F32),
        compiler_params=_cparams("arbitrary", "arbitrary"),
        name="ada_mod",
    )(c_all, w_mod, b_mod.reshape(DEPTH, 1, 6 * D_MODEL))


def _mlp_kernel(x_ref, sh_ref, sc_ref, gt_ref, g_ref, wu_ref, wd_ref, o_ref, h_scr, acc_scr):
    f = pl.program_id(1)

    @pl.when(f == 0)
    def _():
        h = _modulate(x_ref[...], g_ref[...], _rows(sh_ref), _rows(sc_ref))
        h_scr[...] = h.astype(BF16)
        acc_scr[...] = jnp.zeros_like(acc_scr)

    a = jnp.maximum(_dot(h_scr[...], wu_ref[...]), 0.0)
    acc_scr[...] += _dot((a * a).astype(BF16), wd_ref[...])

    @pl.when(f == pl.num_programs(1) - 1)
    def _():
        o_ref[...] = x_ref[...] + _rows(gt_ref) * acc_scr[...]


def _mlp_layer(x, shift, scale, gate, g, w_up, w_down, per_row, rows_per_batch):
    T = x.shape[0]
    tm = 1024
    tf = 512
    tpb = rows_per_batch // tm if not per_row else 1
    ms = _mod_spec(per_row, tm, tpb, 2)
    return pl.pallas_call(
        _mlp_kernel,
        grid=(T // tm, D_FF // tf),
        in_specs=[
            pl.BlockSpec((tm, D_MODEL), lambda i, f: (i, 0)),
            ms, ms, ms,
            pl.BlockSpec((1, D_MODEL), lambda i, f: (0, 0)),
            pl.BlockSpec((D_MODEL, tf), lambda i, f: (0, f)),
            pl.BlockSpec((tf, D_MODEL), lambda i, f: (f, 0)),
        ],
        out_specs=pl.BlockSpec((tm, D_MODEL), lambda i, f: (i, 0)),
        out_shape=jax.ShapeDtypeStruct((T, D_MODEL), F32),
        scratch_shapes=[pltpu.VMEM((tm, D_MODEL), BF16), pltpu.VMEM((tm, D_MODEL), F32)],
        compiler_params=_cparams("parallel", "arbitrary"),
        name="mlp",
    )(x, shift, scale, gate, g, w_up, w_down)


def _gmlp_kernel(x_ref, sh_ref, sc_ref, gt_ref, g_ref, win_ref, vg_ref, wmix_ref, bmix_ref, wout_ref,
                 *out_refs, tm, emit_v):
    o_ref = out_refs[0]
    gs_scr = out_refs[-1]
    x = x_ref[...]
    h = _modulate(x, g_ref[...], _rows(sh_ref), _rows(sc_ref)).astype(BF16)
    z = _gelu_tanh(_dot(h, win_ref[...]))
    u = z[:, :D_A]
    v = z[:, D_A:]
    v = v * lax.rsqrt(jnp.mean(v * v, axis=-1, keepdims=True) + EPS) * vg_ref[...]
    if emit_v:
        out_refs[1][...] = v
    vb = v.astype(BF16)
    for c in range(tm // CHUNK_A):
        r0 = c * CHUNK_A
        for gi in range(N_GROUPS_A):
            c0 = gi * GROUP_A
            s = _dot(wmix_ref[gi], vb[r0:r0 + CHUNK_A, c0:c0 + GROUP_A])
            s = s + bmix_ref[:, c0:c0 + GROUP_A]
            gs_scr[r0:r0 + CHUNK_A, c0:c0 + GROUP_A] = (u[r0:r0 + CHUNK_A, c0:c0 + GROUP_A] * s).astype(BF16)
    y = _dot(gs_scr[...], wout_ref[...])
    o_ref[...] = x + _rows(gt_ref) * y


def _gmlp_layer(x, shift, scale, gate, g, w_in, v_gain, w_mix, b_mix, w_out, per_row, rows_per_batch,
                emit_v):
    T = x.shape[0]
    tm = 512
    tpb = rows_per_batch // tm if not per_row else 1
    ms = _mod_spec(per_row, tm, tpb, 1)
    out_shape = [jax.ShapeDtypeStruct((T, D_MODEL), F32)]
    out_specs = [pl.BlockSpec((tm, D_MODEL), lambda i: (i, 0))]
    if emit_v:
        out_shape.append(jax.ShapeDtypeStruct((T, D_A), F32))
        out_specs.append(pl.BlockSpec((tm, D_A), lambda i: (i, 0)))
    return pl.pallas_call(
        functools.partial(_gmlp_kernel, tm=tm, emit_v=emit_v),
        grid=(T // tm,),
        in_specs=[
            pl.BlockSpec((tm, D_MODEL), lambda i: (i, 0)),
            ms, ms, ms,
            _const_spec((1, D_MODEL)),
            _const_spec((D_MODEL, 2 * D_A)),
            _const_spec((1, D_A)),
            _const_spec((N_GROUPS_A, CHUNK_A, CHUNK_A)),
            _const_spec((CHUNK_A, D_A)),
            _const_spec((D_A, D_MODEL)),
        ],
        out_specs=out_specs,
        out_shape=out_shape,
        scratch_shapes=[pltpu.VMEM((tm, D_A), BF16)],
        compiler_params=_cparams("parallel"),
        name="gmlp_mixer",
    )(x, shift, scale, gate, g, w_in, v_gain, w_mix, b_mix, w_out)


def _conv_kernel(*refs, tm, per_row, tiles_per_batch):
    if per_row:
        (x_ref, sh_ref, sc_ref, gt_ref, g_ref, win_ref, cw_ref, wout_ref, p1_ref, p2_ref,
         o_ref, xin_ref) = refs
    else:
        (x_ref, sh_ref, sc_ref, gt_ref, g_ref, win_ref, cw_ref, wout_ref,
         o_ref, tail_ref, carry_scr) = refs
    x = x_ref[...]
    h = _modulate(x, g_ref[...], _rows(sh_ref), _rows(sc_ref)).astype(BF16)
    bcx = _dot(h, win_ref[...])
    gate_out = bcx[:, :D_C]
    xin = bcx[:, D_C:2 * D_C] * bcx[:, 2 * D_C:]
    row = lax.broadcasted_iota(jnp.int32, (tm, D_C), 0)
    roll1 = pltpu.roll(xin, 1, 0)
    roll2 = pltpu.roll(xin, 2, 0)
    if per_row:
        pos = row % SUBLANES
        prev1 = jnp.where(pos == 0, p1_ref[...], roll1)
        prev2 = jnp.where(pos < 2, p2_ref[...], roll2)
        xin_ref[...] = xin
    else:
        @pl.when(pl.program_id(0) % tiles_per_batch == 0)
        def _():
            carry_scr[...] = jnp.zeros_like(carry_scr)
        c6 = carry_scr[SUBLANES - 2:SUBLANES - 1, :]
        c7 = carry_scr[SUBLANES - 1:SUBLANES, :]
        prev1 = jnp.where(row == 0, c7, roll1)
        prev2 = jnp.where(row == 0, c6, jnp.where(row == 1, c7, roll2))
        tail = xin[tm - SUBLANES:, :]
        carry_scr[...] = tail
        tail_ref[0] = tail
    y = cw_ref[0:1, :] * prev2 + cw_ref[1:2, :] * prev1 + cw_ref[2:3, :] * xin
    out = _dot((gate_out * y).astype(BF16), wout_ref[...])
    o_ref[...] = x + _rows(gt_ref) * out


def _conv_layer(x, shift, scale, gate, g, w_in, conv_w, w_out, per_row, rows_per_batch, fills=None):
    T = x.shape[0]
    tm = 512
    tpb = rows_per_batch // tm if not per_row else 1
    ms = _mod_spec(per_row, tm, tpb, 1)
    in_specs = [
        pl.BlockSpec((tm, D_MODEL), lambda i: (i, 0)),
        ms, ms, ms,
        _const_spec((1, D_MODEL)),
        _const_spec((D_MODEL, 3 * D_C)),
        _const_spec((CONV_W, D_C)),
        _const_spec((D_C, D_MODEL)),
    ]
    args = [x, shift, scale, gate, g, w_in, conv_w, w_out]
    out_shape = [jax.ShapeDtypeStruct((T, D_MODEL), F32)]
    out_specs = [pl.BlockSpec((tm, D_MODEL), lambda i: (i, 0))]
    scratch = []
    if per_row:
        in_specs += [pl.BlockSpec((tm, D_C), lambda i: (i, 0))] * 2
        args += list(fills)
        out_shape.append(jax.ShapeDtypeStruct((T, D_C), F32))
        out_specs.append(pl.BlockSpec((tm, D_C), lambda i: (i, 0)))
        sem = "parallel"
    else:
        nb = T // rows_per_batch
        out_shape.append(jax.ShapeDtypeStruct((nb, SUBLANES, D_C), F32))
        out_specs.append(pl.BlockSpec((1, SUBLANES, D_C), lambda i: (i // tpb, 0, 0)))
        scratch.append(pltpu.VMEM((SUBLANES, D_C), F32))
        sem = "arbitrary"
    return pl.pallas_call(
        functools.partial(_conv_kernel, tm=tm, per_row=per_row, tiles_per_batch=tpb),
        grid=(T // tm,),
        in_specs=in_specs,
        out_specs=out_specs,
        out_shape=out_shape,
        scratch_shapes=scratch,
        compiler_params=_cparams(sem),
        name="conv_mixer",
    )(*args)


def _qkv_kernel(x_ref, sh_ref, sc_ref, g_ref, w_ref, qg_ref, kg_ref, q_ref, k_ref, v_ref):
    h = _modulate(x_ref[...], g_ref[...], _rows(sh_ref), _rows(sc_ref)).astype(BF16)
    qkv = _dot(h, w_ref[...])
    hd = HEAD_DIM_B
    for hh in range(N_HEADS_B):
        q = qkv[:, hh * hd:(hh + 1) * hd]
        k = qkv[:, D_MODEL + hh * hd:D_MODEL + (hh + 1) * hd]
        q_ref[:, hh * hd:(hh + 1) * hd] = (
            q * lax.rsqrt(jnp.mean(q * q, axis=-1, keepdims=True) + EPS) * qg_ref[...])
        k_ref[:, hh * hd:(hh + 1) * hd] = (
            k * lax.rsqrt(jnp.mean(k * k, axis=-1, keepdims=True) + EPS) * kg_ref[...])
    v_ref[...] = qkv[:, 2 * D_MODEL:]


def _qkv_layer(x, shift, scale, g, w_qkv, q_gain, k_gain, per_row, rows_per_batch):
    T = x.shape[0]
    tm = 512
    tpb = rows_per_batch // tm if not per_row else 1
    ms = _mod_spec(per_row, tm, tpb, 1)
    row_spec = pl.BlockSpec((tm, D_MODEL), lambda i: (i, 0))
    return pl.pallas_call(
        _qkv_kernel,
        grid=(T // tm,),
        in_specs=[row_spec, ms, ms,
                  _const_spec((1, D_MODEL)),
                  _const_spec((D_MODEL, 3 * D_MODEL)),
                  _const_spec((1, HEAD_DIM_B)),
                  _const_spec((1, HEAD_DIM_B))],
        out_specs=[row_spec, row_spec, row_spec],
        out_shape=[jax.ShapeDtypeStruct((T, D_MODEL), F32)] * 3,
        compiler_params=_cparams("parallel"),
        name="moba_qkv",
    )(x, shift, scale, g, w_qkv, q_gain, k_gain)


def _topk_mask_t(gate_t, n_cand, blk_of_row, elig):
    gm = jnp.where(elig, gate_t, MASK_VALUE)
    cnt = jnp.zeros(gate_t.shape, F32)
    for c in n_cand:
        r, cb = c
        row = gm[r:r + 1, :]
        beats = jnp.where(row > gm, 1.0, jnp.where(row == gm, jnp.where(cb < blk_of_row, 1.0, 0.0), 0.0))
        cnt = cnt + beats
    return jnp.where(elig, jnp.where(cnt < TOPK_B, 1.0, 0.0), 0.0)


def _moba_prompt_kernel(q_ref, k_ref, v_ref, bias_ref, o_ref, kb_scr, vb_scr, kmean_scr, *, seq):
    qb = pl.program_id(2)
    nblk = seq // BLOCK_B
    hd = HEAD_DIM_B

    @pl.when(qb == 0)
    def _():
        k = k_ref[...]
        kb_scr[...] = k.astype(BF16)
        vb_scr[...] = v_ref[...].astype(BF16)
        kmean_scr[...] = jnp.sum(k.reshape(nblk, BLOCK_B, hd), axis=1) / BLOCK_B

    q = q_ref[...]
    qs = (q * (1.0 / math.sqrt(hd))).astype(BF16)

    gate_t = _dot_nt(kmean_scr[...], q, precision=HIGHEST)
    blk = lax.broadcasted_iota(jnp.int32, (nblk, BLOCK_B), 0)
    sel_t = _topk_mask_t(gate_t, [(j, j) for j in range(nblk)], blk, blk < qb)
    sel_pad = jnp.concatenate([sel_t, jnp.zeros((LANES - nblk, BLOCK_B), F32)], axis=0)
    sel = sel_pad.T
    lane = lax.broadcasted_iota(jnp.int32, (BLOCK_B, LANES), 1)

    q0 = pl.multiple_of(qb * BLOCK_B, BLOCK_B)
    s = _dot_nt(qs, kb_scr[pl.ds(q0, BLOCK_B), :]) + bias_ref[0, 0]
    r_i = lax.broadcasted_iota(jnp.int32, (BLOCK_B, BLOCK_B), 0)
    c_i = lax.broadcasted_iota(jnp.int32, (BLOCK_B, BLOCK_B), 1)
    s = jnp.where(r_i >= c_i, s, MASK_VALUE)
    m0 = jnp.max(s, axis=-1, keepdims=True)
    p = jnp.exp(s - m0)
    l0 = jnp.sum(p, axis=-1, keepdims=True)
    acc0 = _dot(p.astype(BF16), vb_scr[pl.ds(q0, BLOCK_B), :])

    n_far = bias_ref.shape[1] - 1

    def body(kb, carry):
        m, l, acc = carry
        k0 = pl.multiple_of(kb * BLOCK_B, BLOCK_B)
        s = _dot_nt(qs, kb_scr[pl.ds(k0, BLOCK_B), :]) + bias_ref[0, jnp.minimum(qb - kb, n_far)]
        col = jnp.sum(jnp.where(lane == kb, sel, 0.0), axis=-1, keepdims=True)
        s = jnp.where(col > 0.5, s, MASK_VALUE)
        m_new = jnp.maximum(m, jnp.max(s, axis=-1, keepdims=True))
        alpha = jnp.exp(m - m_new)
        p = jnp.exp(s - m_new)
        l = alpha * l + jnp.sum(p, axis=-1, keepdims=True)
        acc = alpha * acc + _dot(p.astype(BF16), vb_scr[pl.ds(k0, BLOCK_B), :])
        return m_new, l, acc

    m, l, acc = lax.fori_loop(0, qb, body, (m0, l0, acc0))
    o_ref[...] = (acc / l).astype(o_ref.dtype)


def _moba_prompt_attention(q, k, v, bias_tiles, batch, seq):
    nq = seq // BLOCK_B
    hd = HEAD_DIM_B
    n_delta = bias_tiles.shape[1]
    return pl.pallas_call(
        functools.partial(_moba_prompt_kernel, seq=seq),
        grid=(batch, N_HEADS_B, nq),
        in_specs=[
            pl.BlockSpec((BLOCK_B, hd), lambda b, h, i: (b * nq + i, h)),
            pl.BlockSpec((seq, hd), lambda b, h, i: (b, h)),
            pl.BlockSpec((seq, hd), lambda b, h, i: (b, h)),
            pl.BlockSpec((1, n_delta, BLOCK_B, BLOCK_B), lambda b, h, i: (h, 0, 0, 0)),
        ],
        out_specs=pl.BlockSpec((BLOCK_B, hd), lambda b, h, i: (b * nq + i, h)),
        out_shape=jax.ShapeDtypeStruct((batch * seq, D_MODEL), BF16),
        scratch_shapes=[pltpu.VMEM((seq, hd), BF16), pltpu.VMEM((seq, hd), BF16),
                        pltpu.VMEM((seq // BLOCK_B, hd), F32)],
        compiler_params=_cparams("parallel", "parallel", "arbitrary"),
        name="moba_prompt_attn",
    )(q, k, v, bias_tiles)


def _moba_sample_kernel(pt_ref, q_ref, kn_ref, vn_ref, *refs, pps, n_pages, t_new):
    kc_refs = refs[:pps]
    vc_refs = refs[pps:2 * pps]
    bias_ref = refs[2 * pps]
    o_ref = refs[2 * pps + 1]
    qbd_scr, qbf_scr, m_scr, l_scr, o_scr, ksum_scr = refs[2 * pps + 2:]
    del pt_ref
    s_id = pl.program_id(1)
    H, hd = N_HEADS_B, HEAD_DIM_B
    nrow = H * t_new

    @pl.when(s_id == 0)
    def _():
        q8 = q_ref[0]
        qt = jnp.concatenate([q8] * H, axis=0)
        r_i = lax.broadcasted_iota(jnp.int32, (nrow, H * hd), 0)
        c_i = lax.broadcasted_iota(jnp.int32, (nrow, H * hd), 1)
        qbd = jnp.where(r_i // t_new == c_i // hd, qt, 0.0)
        qbd_scr[...] = qbd
        qbf_scr[...] = (qbd * (1.0 / math.sqrt(hd))).astype(BF16)

    def partial_attn(kp, vp, bias, valid):
        s = _dot_nt(kp.astype(BF16), qbf_scr[...]) + bias
        if valid is not None:
            s = jnp.where(valid, s, MASK_VALUE)
        m = jnp.max(s, axis=0, keepdims=True)
        e = jnp.exp(s - m)
        l = jnp.sum(e, axis=0, keepdims=True)
        o_full = _dot_tn(e.astype(BF16), vp.astype(BF16))
        o = jnp.concatenate(
            [o_full[h * t_new:(h + 1) * t_new, h * hd:(h + 1) * hd] for h in range(H)], axis=0)
        return m, l, o

    for i in range(pps):
        p = s_id * pps + i
        kp = kc_refs[i][0]
        ksum_scr[pl.ds(p, 1), :] = jnp.sum(kp, axis=0, keepdims=True)
        b0 = pl.multiple_of(p * PAGE_SIZE, PAGE_SIZE)
        m, l, o = partial_attn(kp, vc_refs[i][0], bias_ref[pl.ds(b0, PAGE_SIZE), :], None)
        m_scr[pl.ds(p, 1), :] = m
        l_scr[pl.ds(p, 1), :] = l
        o_scr[p] = o

    @pl.when(s_id == pl.num_programs(1) - 1)
    def _():
        pages_per_blk = BLOCK_B // PAGE_SIZE
        n_own = kn_ref.shape[1]
        key_i = lax.broadcasted_iota(jnp.int32, (n_own, nrow), 0)
        qry_i = lax.broadcasted_iota(jnp.int32, (n_own, nrow), 1) % t_new
        past = n_pages * PAGE_SIZE
        m_o, l_o, o_o = partial_attn(kn_ref[0], vn_ref[0], bias_ref[past:past + n_own, :], key_i <= qry_i)

        ks = ksum_scr[...]
        pg = lax.broadcasted_iota(jnp.int32, ks.shape, 0)
        partner = jnp.where(pg % pages_per_blk == 0, pltpu.roll(ks, n_pages - 1, 0), pltpu.roll(ks, 1, 0))
        kmean = (ks + partner) / BLOCK_B
        gate_t = _dot_nt(kmean, qbd_scr[...], precision=HIGHEST)
        pblk = lax.broadcasted_iota(jnp.int32, (n_pages, nrow), 0) // pages_per_blk
        sel = _topk_mask_t(gate_t, [(pages_per_blk * j, j) for j in range(n_pages // pages_per_blk)],
                           pblk, pblk >= 0)

        m_all = m_scr[...]
        l_all = l_scr[...]
        m_fin = jnp.maximum(jnp.max(jnp.where(sel > 0.5, m_all, MASK_VALUE), axis=0, keepdims=True), m_o)
        w = jnp.where(sel > 0.5, jnp.exp(m_all - m_fin), 0.0)
        w_o = jnp.exp(m_o - m_fin)
        l_fin = jnp.sum(w * l_all, axis=0, keepdims=True) + w_o * l_o
        stack = jnp.concatenate([w, w_o, l_fin], axis=0)
        stack = jnp.concatenate([stack, jnp.zeros((stack.shape[0], LANES - nrow), F32)], axis=1)
        stack = jnp.concatenate([stack, jnp.zeros((LANES - stack.shape[0], LANES), F32)], axis=0)
        st = stack.T
        acc = st[:nrow, n_pages:n_pages + 1] * o_o
        for pp in range(n_pages):
            acc = acc + st[:nrow, pp:pp + 1] * o_scr[pp]
        out = acc / st[:nrow, n_pages + 1:n_pages + 2]
        for h in range(H):
            o_ref[0, :, h * hd:(h + 1) * hd] = out[h * t_new:(h + 1) * t_new, :]


def _moba_sample_attention(q, k_new, v_new, cache_k, cache_v, page_table, bias_tab, t_new):
    nb, n_pages = page_table.shape
    pps = 4
    n_steps = n_pages // pps
    H, hd = N_HEADS_B, HEAD_DIM_B
    nrow = H * t_new
    n_pool = cache_k.shape[0]
    ck = cache_k.reshape(n_pool, PAGE_SIZE, H * hd)
    cv = cache_v.reshape(n_pool, PAGE_SIZE, H * hd)
    n_own = k_new.shape[1]

    def page_spec(i):
        return pl.BlockSpec((1, PAGE_SIZE, H * hd),
                            lambda b, s, pt: (pt[b * n_pages + s * pps + i], 0, 0))

    row_spec = pl.BlockSpec((1, t_new, H * hd), lambda b, s, pt: (b, 0, 0))
    own_spec = pl.BlockSpec((1, n_own, H * hd), lambda b, s, pt: (b, 0, 0))
    grid_spec = pltpu.PrefetchScalarGridSpec(
        num_scalar_prefetch=1,
        grid=(nb, n_steps),
        in_specs=[row_spec, own_spec, own_spec]
                 + [page_spec(i) for i in range(pps)] + [page_spec(i) for i in range(pps)]
                 + [pl.BlockSpec(bias_tab.shape, lambda b, s, pt: (0, 0))],
        out_specs=row_spec,
        scratch_shapes=[
            pltpu.VMEM((nrow, H * hd), F32),
            pltpu.VMEM((nrow, H * hd), BF16),
            pltpu.VMEM((n_pages, nrow), F32),
            pltpu.VMEM((n_pages, nrow), F32),
            pltpu.VMEM((n_pages, nrow, hd), F32),
            pltpu.VMEM((n_pages, H * hd), F32),
        ],
    )
    return pl.pallas_call(
        functools.partial(_moba_sample_kernel, pps=pps, n_pages=n_pages, t_new=t_new),
        grid_spec=grid_spec,
        out_shape=jax.ShapeDtypeStruct((nb, t_new, H * hd), F32),
        compiler_params=_cparams("parallel", "arbitrary"),
        name="moba_sample_attn",
    )(page_table.reshape(-1), q, k_new, v_new, *([ck] * pps), *([cv] * pps), bias_tab)


def _proj_kernel(x_ref, gt_ref, a_ref, w_ref, o_ref):
    o_ref[...] = x_ref[...] + _rows(gt_ref) * _dot(a_ref[...].astype(BF16), w_ref[...])


def _proj_residual(x, gate, a, w, per_row, rows_per_batch):
    T = x.shape[0]
    tm = 512
    tpb = rows_per_batch // tm if not per_row else 1
    row_spec = pl.BlockSpec((tm, D_MODEL), lambda i: (i, 0))
    return pl.pallas_call(
        _proj_kernel,
        grid=(T // tm,),
        in_specs=[row_spec, _mod_spec(per_row, tm, tpb, 1), row_spec,
                  _const_spec((D_MODEL, D_MODEL))],
        out_specs=row_spec,
        out_shape=jax.ShapeDtypeStruct((T, D_MODEL), F32),
        compiler_params=_cparams("parallel"),
        name="moba_out_proj",
    )(x, gate, a, w)


def _rel_bucket(dist):
    max_exact = N_BUCKETS // 2
    d = jnp.maximum(dist, 0)
    log_ratio = jnp.log(jnp.maximum(d, 1).astype(jnp.float32) / max_exact) / math.log(MAX_DIST / max_exact)
    large = jnp.minimum(max_exact + (log_ratio * (N_BUCKETS - max_exact)).astype(jnp.int32), N_BUCKETS - 1)
    return jnp.where(d < max_exact, d, large)


def _bias_tables(rel_bias, seq, past_len, t_new, n_own):
    d_max = max(seq, past_len + t_new)
    tb = rel_bias[_rel_bucket(jnp.arange(d_max, dtype=jnp.int32))]
    n_delta = -(-(MAX_DIST + BLOCK_B) // BLOCK_B) + 1
    r = np.arange(BLOCK_B)
    idx = np.arange(n_delta)[:, None, None] * BLOCK_B + r[None, :, None] - r[None, None, :]
    tiles = jnp.transpose(tb[np.maximum(idx, 0)], (3, 0, 1, 2))
    kpos = np.arange(past_len)[:, None]
    qr = np.arange(t_new)[None, :]
    idx_past = past_len + qr - kpos
    idx_own = np.maximum(qr - np.arange(n_own)[:, None], 0)
    idx_s = np.concatenate([idx_past, idx_own], axis=0)
    tab = jnp.transpose(tb[idx_s], (0, 2, 1)).reshape(past_len + n_own, N_HEADS_B * t_new)
    return tiles, tab


def kernel(x_prompt, x_sample, cache_k, cache_v, state_conv, page_table, c_prompt, c_sample, rel_bias, norm_mix, norm_mlp, w_mod, b_mod, w_up, w_down, a_w_in, a_v_gain, a_w_s, a_b_s, a_w_out, b_w_qkv, b_q_gain, b_k_gain, b_w_out, c_w_in, c_conv, c_w_out):
    B, S, D = x_prompt.shape
    DB, T, _ = x_sample.shape
    n_pages = page_table.shape[1]
    past_len = n_pages * PAGE_SIZE
    assert S % BLOCK_B == 0 and past_len % BLOCK_B == 0 and T == SUBLANES
    assert MAX_DIST % BLOCK_B == 0

    xp = x_prompt.reshape(B * S, D)
    xs = x_sample.reshape(DB * T, D)

    n_c = B + DB
    n_c_pad = -(-n_c // SUBLANES) * SUBLANES
    c_all = jnp.concatenate([c_prompt, c_sample, jnp.zeros((n_c_pad - n_c, D), F32)], axis=0)
    mods = _ada_all(c_all, w_mod, b_mod)

    n_own = 2 * SUBLANES
    bias_tiles, bias_tab = _bias_tables(rel_bias, S, past_len, T, n_own)

    ci = np.arange(CHUNK_A)
    tril_p = (ci[:, None] >= ci[None, :])
    tril_s = tril_p & ((ci[:, None] // T) == (ci[None, :] // T))

    k_p, v_p, k_s, v_s, conv_p, conv_s, chunkv_s = [], [], [], [], [], [], []
    for i in range(DEPTH):
        kind, j = i % N_MIXERS, i // N_MIXERS
        mp = [m.reshape(B, 1, D) for m in jnp.split(mods[i, :B], 6, axis=-1)]
        ms = [jnp.repeat(m, T, axis=0) for m in jnp.split(mods[i, B:B + DB], 6, axis=-1)]
        g_mix = norm_mix[i].reshape(1, D)
        g_mlp = norm_mlp[i].reshape(1, D)
        if kind == 0:
            w_in = a_w_in[j].astype(BF16)
            w_out = a_w_out[j].astype(BF16)
            vg = a_v_gain[j].reshape(1, D_A)
            wmix_p = jnp.where(tril_p, a_w_s[j], 0.0).astype(BF16)
            bmix_p = jnp.repeat(jnp.transpose(a_b_s[j]), GROUP_A, axis=1)
            ws_t = jnp.tile(a_w_s[j][:, :T, :T], (1, CHUNK_A // T, CHUNK_A // T))
            wmix_s = jnp.where(tril_s, ws_t, 0.0).astype(BF16)
            bmix_s = jnp.repeat(jnp.tile(jnp.transpose(a_b_s[j][:, :T]), (CHUNK_A // T, 1)), GROUP_A, axis=1)
            (xp,) = _gmlp_layer(xp, mp[0], mp[1], mp[2], g_mix, w_in, vg, wmix_p, bmix_p, w_out,
                                False, S, False)
            xs, v_new = _gmlp_layer(xs, ms[0], ms[1], ms[2], g_mix, w_in, vg, wmix_s, bmix_s, w_out,
                                    True, T, True)
            chunkv_s.append(v_new.reshape(DB, T, D_A))
        elif kind == 1:
            w_qkv = b_w_qkv[j].astype(BF16)
            w_out = b_w_out[j].astype(BF16)
            qg = b_q_gain[j].reshape(1, HEAD_DIM_B)
            kg = b_k_gain[j].reshape(1, HEAD_DIM_B)
            qp, kp, vp = _qkv_layer(xp, mp[0], mp[1], g_mix, w_qkv, qg, kg, False, S)
            qs, ks, vs = _qkv_layer(xs, ms[0], ms[1], g_mix, w_qkv, qg, kg, True, T)
            op = _moba_prompt_attention(qp, kp, vp, bias_tiles, B, S)
            pad = ((0, 0), (0, n_own - T), (0, 0))
            os_ = _moba_sample_attention(
                qs.reshape(DB, T, D), jnp.pad(ks.reshape(DB, T, D), pad), jnp.pad(vs.reshape(DB, T, D), pad),
                cache_k[j], cache_v[j], page_table, bias_tab, T)
            xp = _proj_residual(xp, mp[2], op, w_out, False, S)
            xs = _proj_residual(xs, ms[2], os_.reshape(DB * T, D), w_out, True, T)
            k_p.append(kp.reshape(B, S, N_HEADS_B, HEAD_DIM_B))
            v_p.append(vp.reshape(B, S, N_HEADS_B, HEAD_DIM_B))
            k_s.append(ks.reshape(DB, T, N_HEADS_B, HEAD_DIM_B))
            v_s.append(vs.reshape(DB, T, N_HEADS_B, HEAD_DIM_B))
        else:
            w_in = c_w_in[j].astype(BF16)
            w_out = c_w_out[j].astype(BF16)
            st = state_conv[j]
            zrow = jnp.zeros((DB, T - 1, D_C), F32)
            p1 = jnp.concatenate([st[:, 1:2], zrow], axis=1).reshape(DB * T, D_C)
            p2 = jnp.concatenate([st, zrow[:, 1:]], axis=1).reshape(DB * T, D_C)
            xp, tail_p = _conv_layer(xp, mp[0], mp[1], mp[2], g_mix, w_in, c_conv[j], w_out, False, S)
            xs, xin_s = _conv_layer(xs, ms[0], ms[1], ms[2], g_mix, w_in, c_conv[j], w_out, True, T,
                                    fills=(p1, p2))
            conv_p.append(tail_p[:, SUBLANES - (CONV_W - 1):, :])
            conv_s.append(xin_s.reshape(DB, T, D_C)[:, T - (CONV_W - 1):, :])
        wu = w_up[i].astype(BF16)
        wd = w_down[i].astype(BF16)
        xp = _mlp_layer(xp, mp[3], mp[4], mp[5], g_mlp, wu, wd, False, S)
        xs = _mlp_layer(xs, ms[3], ms[4], ms[5], g_mlp, wu, wd, True, T)
    return (xp.reshape(B, S, D), xs.reshape(DB, T, D), jnp.stack(k_p), jnp.stack(v_p), jnp.stack(k_s),
            jnp.stack(v_s), jnp.stack(conv_p), jnp.stack(conv_s), jnp.stack(chunkv_s))
```

```python
import functools
import math

import jax
import jax.numpy as jnp
import numpy as np
from jax import lax
from jax.experimental import pallas as pl
from jax.experimental.pallas import tpu as pltpu

D_MODEL = 1024
DEPTH = 4
N_MIXERS = 3
D_A = 2 * D_MODEL
N_GROUPS_A = 8
GROUP_A = D_A // N_GROUPS_A
CHUNK_A = 128
HEAD_DIM_B = 128
N_HEADS_B = D_MODEL // HEAD_DIM_B
BLOCK_B = 256
TOPK_B = 3
N_BUCKETS = 32
MAX_DIST = 1024
D_C = D_MODEL
CONV_W = 3
D_FF = 4 * D_MODEL
EPS = 1e-6
MASK_VALUE = -1e30
PAGE_SIZE = 128

SUBLANES = 8
LANES = 128
VMEM_LIMIT_BYTES = 56 * 1024 * 1024

F32 = jnp.float32
BF16 = jnp.bfloat16
HIGHEST = lax.Precision.HIGHEST


def _cparams(*sem):
    return pltpu.CompilerParams(dimension_semantics=sem, vmem_limit_bytes=VMEM_LIMIT_BYTES)


def _rows(ref):
    v = ref[...]
    return v.reshape(v.shape[-2], v.shape[-1])


def _modulate(x, g, shift, scale):
    y = x * lax.rsqrt(jnp.mean(x * x, axis=-1, keepdims=True) + EPS)
    return (y * g) * (1.0 + scale) + shift


def _gelu_tanh(x):
    c = math.sqrt(2.0 / math.pi)
    return x * (0.5 * (1.0 + jnp.tanh(c * (x + 0.044715 * (x * x * x)))))


def _dot(a, b):
    return jnp.dot(a, b, preferred_element_type=F32)


def _dot_nt(a, b, **kw):
    return lax.dot_general(a, b, (((1,), (1,)), ((), ())), preferred_element_type=F32, **kw)


def _dot_tn(a, b):
    return lax.dot_general(a, b, (((0,), (0,)), ((), ())), preferred_element_type=F32)


def _mod_spec(per_row, tm, tiles_per_batch, ngrid):
    if per_row:
        if ngrid == 1:
            return pl.BlockSpec((tm, D_MODEL), lambda i: (i, 0))
        return pl.BlockSpec((tm, D_MODEL), lambda i, f: (i, 0))
    if ngrid == 1:
        return pl.BlockSpec((1, 1, D_MODEL), lambda i: (i // tiles_per_batch, 0, 0))
    return pl.BlockSpec((1, 1, D_MODEL), lambda i, f: (i // tiles_per_batch, 0, 0))


def _const_spec(shape):
    nd = len(shape)
    return pl.BlockSpec(shape, lambda i: (0,) * nd, pipeline_mode=pl.Buffered(1))


def _mod_kernel(c_ref, w_ref, b_ref, o_ref):
    c = c_ref[...]
    sc = (c * jax.nn.sigmoid(c)).astype(BF16)
    o_ref[0] = _dot(sc, w_ref[0].astype(BF16)) + b_ref[0]


def _ada_all(c_all, w_mod, b_mod):
    nrow = c_all.shape[0]
    tn = 1536
    nn = (6 * D_MODEL) // tn
    return pl.pallas_call(
        _mod_kernel,
        grid=(DEPTH, nn),
        in_specs=[
            pl.BlockSpec((nrow, D_MODEL), lambda l, n: (0, 0)),
            pl.BlockSpec((1, D_MODEL, tn), lambda l, n: (l, 0, n)),
            pl.BlockSpec((1, 1, tn), lambda l, n: (l, 0, n)),
        ],
        out_specs=pl.BlockSpec((1, nrow, tn), lambda l, n: (l, 0, n)),
        out_shape=jax.ShapeDtypeStruct((DEPTH, nrow, 6 * D_MODEL), F32),
        compiler_params=_cparams("arbitrary", "arbitrary"),
        name="ada_mod",
    )(c_all, w_mod, b_mod.reshape(DEPTH, 1, 6 * D_MODEL))


def _mlp_kernel(x_ref, sh_ref, sc_ref, gt_ref, g_ref, wu_ref, wd_ref, o_ref, h_scr, acc_scr):
    f = pl.program_id(1)

    @pl.when(f == 0)
    def _():
        h = _modulate(x_ref[...], g_ref[...], _rows(sh_ref), _rows(sc_ref))
        h_scr[...] = h.astype(BF16)
        acc_scr[...] = jnp.zeros_like(acc_scr)

    a = jnp.maximum(_dot(h_scr[...], wu_ref[...]), 0.0)
    acc_scr[...] += _dot((a * a).astype(BF16), wd_ref[...])

    @pl.when(f == pl.num_programs(1) - 1)
    def _():
        o_ref[...] = x_ref[...] + _rows(gt_ref) * acc_scr[...]


def _mlp_layer(x, shift, scale, gate, g, w_up, w_down, per_row, rows_per_batch):
    T = x.shape[0]
    tm = 1024
    tf = 512
    tpb = rows_per_batch // tm if not per_row else 1
    ms = _mod_spec(per_row, tm, tpb, 2)
    return pl.pallas_call(
        _mlp_kernel,
        grid=(T // tm, D_FF // tf),
        in_specs=[
            pl.BlockSpec((tm, D_MODEL), lambda i, f: (i, 0)),
            ms, ms, ms,
            pl.BlockSpec((1, D_MODEL), lambda i, f: (0, 0)),
            pl.BlockSpec((D_MODEL, tf), lambda i, f: (0, f)),
            pl.BlockSpec((tf, D_MODEL), lambda i, f: (f, 0)),
        ],
        out_specs=pl.BlockSpec((tm, D_MODEL), lambda i, f: (i, 0)),
        out_shape=jax.ShapeDtypeStruct((T, D_MODEL), F32),
        scratch_shapes=[pltpu.VMEM((tm, D_MODEL), BF16), pltpu.VMEM((tm, D_MODEL), F32)],
        compiler_params=_cparams("parallel", "arbitrary"),
        name="mlp",
    )(x, shift, scale, gate, g, w_up, w_down)


def _gmlp_kernel(x_ref, sh_ref, sc_ref, gt_ref, g_ref, win_ref, vg_ref, wmix_ref, bmix_ref, wout_ref,
                 *out_refs, tm, emit_v):
    o_ref = out_refs[0]
    gs_scr = out_refs[-1]
    x = x_ref[...]
    h = _modulate(x, g_ref[...], _rows(sh_ref), _rows(sc_ref)).astype(BF16)
    z = _gelu_tanh(_dot(h, win_ref[...]))
    u = z[:, :D_A]
    v = z[:, D_A:]
    v = v * lax.rsqrt(jnp.mean(v * v, axis=-1, keepdims=True) + EPS) * vg_ref[...]
    if emit_v:
        out_refs[1][...] = v
    vb = v.astype(BF16)
    for c in range(tm // CHUNK_A):
        r0 = c * CHUNK_A
        for gi in range(N_GROUPS_A):
            c0 = gi * GROUP_A
            s = _dot(wmix_ref[gi], vb[r0:r0 + CHUNK_A, c0:c0 + GROUP_A])
            s = s + bmix_ref[:, c0:c0 + GROUP_A]
            gs_scr[r0:r0 + CHUNK_A, c0:c0 + GROUP_A] = (u[r0:r0 + CHUNK_A, c0:c0 + GROUP_A] * s).astype(BF16)
    y = _dot(gs_scr[...], wout_ref[...])
    o_ref[...] = x + _rows(gt_ref) * y


def _gmlp_layer(x, shift, scale, gate, g, w_in, v_gain, w_mix, b_mix, w_out, per_row, rows_per_batch,
                emit_v):
    T = x.shape[0]
    tm = 512
    tpb = rows_per_batch // tm if not per_row else 1
    ms = _mod_spec(per_row, tm, tpb, 1)
    out_shape = [jax.ShapeDtypeStruct((T, D_MODEL), F32)]
    out_specs = [pl.BlockSpec((tm, D_MODEL), lambda i: (i, 0))]
    if emit_v:
        out_shape.append(jax.ShapeDtypeStruct((T, D_A), F32))
        out_specs.append(pl.BlockSpec((tm, D_A), lambda i: (i, 0)))
    return pl.pallas_call(
        functools.partial(_gmlp_kernel, tm=tm, emit_v=emit_v),
        grid=(T // tm,),
        in_specs=[
            pl.BlockSpec((tm, D_MODEL), lambda i: (i, 0)),
            ms, ms, ms,
            _const_spec((1, D_MODEL)),
            _const_spec((D_MODEL, 2 * D_A)),
            _const_spec((1, D_A)),
            _const_spec((N_GROUPS_A, CHUNK_A, CHUNK_A)),
            _const_spec((CHUNK_A, D_A)),
            _const_spec((D_A, D_MODEL)),
        ],
        out_specs=out_specs,
        out_shape=out_shape,
        scratch_shapes=[pltpu.VMEM((tm, D_A), BF16)],
        compiler_params=_cparams("parallel"),
        name="gmlp_mixer",
    )(x, shift, scale, gate, g, w_in, v_gain, w_mix, b_mix, w_out)


def _conv_kernel(*refs, tm, per_row, tiles_per_batch):
    if per_row:
        (x_ref, sh_ref, sc_ref, gt_ref, g_ref, win_ref, cw_ref, wout_ref, p1_ref, p2_ref,
         o_ref, xin_ref) = refs
    else:
        (x_ref, sh_ref, sc_ref, gt_ref, g_ref, win_ref, cw_ref, wout_ref,
         o_ref, tail_ref, carry_scr) = refs
    x = x_ref[...]
    h = _modulate(x, g_ref[...], _rows(sh_ref), _rows(sc_ref)).astype(BF16)
    bcx = _dot(h, win_ref[...])
    gate_out = bcx[:, :D_C]
    xin = bcx[:, D_C:2 * D_C] * bcx[:, 2 * D_C:]
    row = lax.broadcasted_iota(jnp.int32, (tm, D_C), 0)
    roll1 = pltpu.roll(xin, 1, 0)
    roll2 = pltpu.roll(xin, 2, 0)
    if per_row:
        pos = row % SUBLANES
        prev1 = jnp.where(pos == 0, p1_ref[...], roll1)
        prev2 = jnp.where(pos < 2, p2_ref[...], roll2)
        xin_ref[...] = xin
    else:
        @pl.when(pl.program_id(0) % tiles_per_batch == 0)
        def _():
            carry_scr[...] = jnp.zeros_like(carry_scr)
        c6 = carry_scr[SUBLANES - 2:SUBLANES - 1, :]
        c7 = carry_scr[SUBLANES - 1:SUBLANES, :]
        prev1 = jnp.where(row == 0, c7, roll1)
        prev2 = jnp.where(row == 0, c6, jnp.where(row == 1, c7, roll2))
        tail = xin[tm - SUBLANES:, :]
        carry_scr[...] = tail
        tail_ref[0] = tail
    y = cw_ref[0:1, :] * prev2 + cw_ref[1:2, :] * prev1 + cw_ref[2:3, :] * xin
    out = _dot((gate_out * y).astype(BF16), wout_ref[...])
    o_ref[...] = x + _rows(gt_ref) * out


def _conv_layer(x, shift, scale, gate, g, w_in, conv_w, w_out, per_row, rows_per_batch, fills=None):
    T = x.shape[0]
    tm = 512
    tpb = rows_per_batch // tm if not per_row else 1
    ms = _mod_spec(per_row, tm, tpb, 1)
    in_specs = [
        pl.BlockSpec((tm, D_MODEL), lambda i: (i, 0)),
        ms, ms, ms,
        _const_spec((1, D_MODEL)),
        _const_spec((D_MODEL, 3 * D_C)),
        _const_spec((CONV_W, D_C)),
        _const_spec((D_C, D_MODEL)),
    ]
    args = [x, shift, scale, gate, g, w_in, conv_w, w_out]
    out_shape = [jax.ShapeDtypeStruct((T, D_MODEL), F32)]
    out_specs = [pl.BlockSpec((tm, D_MODEL), lambda i: (i, 0))]
    scratch = []
    if per_row:
        in_specs += [pl.BlockSpec((tm, D_C), lambda i: (i, 0))] * 2
        args += list(fills)
        out_shape.append(jax.ShapeDtypeStruct((T, D_C), F32))
        out_specs.append(pl.BlockSpec((tm, D_C), lambda i: (i, 0)))
        sem = "parallel"
    else:
        nb = T // rows_per_batch
        out_shape.append(jax.ShapeDtypeStruct((nb, SUBLANES, D_C), F32))
        out_specs.append(pl.BlockSpec((1, SUBLANES, D_C), lambda i: (i // tpb, 0, 0)))
        scratch.append(pltpu.VMEM((SUBLANES, D_C), F32))
        sem = "arbitrary"
    return pl.pallas_call(
        functools.partial(_conv_kernel, tm=tm, per_row=per_row, tiles_per_batch=tpb),
        grid=(T // tm,),
        in_specs=in_specs,
        out_specs=out_specs,
        out_shape=out_shape,
        scratch_shapes=scratch,
        compiler_params=_cparams(sem),
        name="conv_mixer",
    )(*args)


def _qkv_kernel(x_ref, sh_ref, sc_ref, g_ref, w_ref, qg_ref, kg_ref, q_ref, k_ref, v_ref, *extra,
                tm, emit_attn_inputs):
    h = _modulate(x_ref[...], g_ref[...], _rows(sh_ref), _rows(sc_ref)).astype(BF16)
    qkv = _dot(h, w_ref[...])
    hd = HEAD_DIM_B
    for hh in range(N_HEADS_B):
        cols = slice(hh * hd, (hh + 1) * hd)
        q = qkv[:, hh * hd:(hh + 1) * hd]
        k = qkv[:, D_MODEL + hh * hd:D_MODEL + (hh + 1) * hd]
        q_ref[:, cols] = q * lax.rsqrt(jnp.mean(q * q, axis=-1, keepdims=True) + EPS) * qg_ref[...]
        kn = k * lax.rsqrt(jnp.mean(k * k, axis=-1, keepdims=True) + EPS) * kg_ref[...]
        k_ref[:, cols] = kn
        if emit_attn_inputs:
            kb_ref, vb_ref, km_ref = extra
            kb_ref[:, cols] = kn.astype(BF16)
            for c in range(tm // BLOCK_B):
                km_ref[c, :, cols] = jnp.sum(kn[c * BLOCK_B:(c + 1) * BLOCK_B], axis=0, keepdims=True) * (
                    1.0 / BLOCK_B)
    v = qkv[:, 2 * D_MODEL:]
    v_ref[...] = v
    if emit_attn_inputs:
        extra[1][...] = v.astype(BF16)


def _qkv_layer(x, shift, scale, g, w_qkv, q_gain, k_gain, per_row, rows_per_batch, emit_attn_inputs):
    T = x.shape[0]
    tm = 512
    tpb = rows_per_batch // tm if not per_row else 1
    ms = _mod_spec(per_row, tm, tpb, 1)
    row_spec = pl.BlockSpec((tm, D_MODEL), lambda i: (i, 0))
    out_specs = [row_spec, row_spec, row_spec]
    out_shape = [jax.ShapeDtypeStruct((T, D_MODEL), F32)] * 3
    if emit_attn_inputs:
        out_specs += [row_spec, row_spec,
                      pl.BlockSpec((tm // BLOCK_B, 1, D_MODEL), lambda i: (i, 0, 0))]
        out_shape += [jax.ShapeDtypeStruct((T, D_MODEL), BF16)] * 2
        out_shape += [jax.ShapeDtypeStruct((T // BLOCK_B, 1, D_MODEL), F32)]
    return pl.pallas_call(
        functools.partial(_qkv_kernel, tm=tm, emit_attn_inputs=emit_attn_inputs),
        grid=(T // tm,),
        in_specs=[row_spec, ms, ms,
                  _const_spec((1, D_MODEL)),
                  _const_spec((D_MODEL, 3 * D_MODEL)),
                  _const_spec((1, HEAD_DIM_B)),
                  _const_spec((1, HEAD_DIM_B))],
        out_specs=out_specs,
        out_shape=out_shape,
        compiler_params=_cparams("parallel"),
        name="moba_qkv",
    )(x, shift, scale, g, w_qkv, q_gain, k_gain)


def _topk_mask_t(gate_t, cands, blk_of_row, elig):
    gm = jnp.where(elig, gate_t, MASK_VALUE)
    cnt = jnp.zeros(gate_t.shape, F32)
    for r, cb in cands:
        row = gm[r:r + 1, :]
        beats = jnp.where(row > gm, 1.0, jnp.where(row == gm, jnp.where(cb < blk_of_row, 1.0, 0.0), 0.0))
        cnt = cnt + beats
    return jnp.where(elig, jnp.where(cnt < TOPK_B, 1.0, 0.0), 0.0)


def _moba_prompt_kernel(q_ref, kb_ref, vb_ref, km_ref, bias_ref, o_ref, qaug_scr, *, nblk, nh):
    qb = pl.program_id(2)
    hd = HEAD_DIM_B
    n_far = bias_ref.shape[1] - 1
    lane = lax.broadcasted_iota(jnp.int32, (BLOCK_B, LANES), 1)
    blk = lax.broadcasted_iota(jnp.int32, (nblk, BLOCK_B), 0)
    ones_blk = jnp.ones((BLOCK_B, hd), BF16)
    q0 = pl.multiple_of(qb * BLOCK_B, BLOCK_B)
    r_i = lax.broadcasted_iota(jnp.int32, (BLOCK_B, BLOCK_B), 0)
    c_i = lax.broadcasted_iota(jnp.int32, (BLOCK_B, BLOCK_B), 1)
    causal = r_i >= c_i

    def onehot(b):
        return jnp.where(lane == b, 1.0, 0.0).astype(BF16)

    carry0 = []
    oh_own = onehot(qb)
    for h in range(nh):
        cols = slice(h * hd, (h + 1) * hd)
        q = q_ref[:, cols]
        gate_t = _dot_nt(km_ref[:, 0, cols], q, precision=HIGHEST)
        sel_t = _topk_mask_t(gate_t, [(j, j) for j in range(nblk)], blk, blk < qb)
        neg_t = jnp.where(blk == qb, 0.0, jnp.where(sel_t > 0.5, 0.0, MASK_VALUE))
        neg = jnp.concatenate([neg_t, jnp.zeros((LANES - nblk, BLOCK_B), F32)], axis=0).T
        q_aug = jnp.concatenate([(q * (1.0 / math.sqrt(hd))).astype(BF16), neg.astype(BF16)], axis=1)
        qaug_scr[h] = q_aug
        k_aug = jnp.concatenate([kb_ref[pl.ds(q0, BLOCK_B), cols], oh_own], axis=1)
        s = _dot_nt(q_aug, k_aug) + bias_ref[h, 0]
        s = jnp.where(causal, s, MASK_VALUE)
        m0 = jnp.max(s, axis=-1, keepdims=True)
        p = jnp.exp(s - m0)
        v_aug = jnp.concatenate([vb_ref[pl.ds(q0, BLOCK_B), cols], ones_blk], axis=1)
        carry0 += [m0, _dot(p.astype(BF16), v_aug)]

    def body(kb, carry):
        k0 = pl.multiple_of(kb * BLOCK_B, BLOCK_B)
        oh = onehot(kb)
        delta = jnp.minimum(qb - kb, n_far)
        out = []
        for h in range(nh):
            cols = slice(h * hd, (h + 1) * hd)
            m, acc = carry[2 * h], carry[2 * h + 1]
            k_aug = jnp.concatenate([kb_ref[pl.ds(k0, BLOCK_B), cols], oh], axis=1)
            s = _dot_nt(qaug_scr[h], k_aug) + bias_ref[h, delta]
            m_new = jnp.maximum(m, jnp.max(s, axis=-1, keepdims=True))
            p = jnp.exp(s - m_new)
            v_aug = jnp.concatenate([vb_ref[pl.ds(k0, BLOCK_B), cols], ones_blk], axis=1)
            acc = jnp.exp(m - m_new) * acc + _dot(p.astype(BF16), v_aug)
            out += [m_new, acc]
        return tuple(out)

    carry = lax.fori_loop(0, qb, body, tuple(carry0))
    for h in range(nh):
        acc = carry[2 * h + 1]
        o_ref[:, h * hd:(h + 1) * hd] = (acc[:, :hd] / acc[:, hd:]).astype(o_ref.dtype)


def _moba_prompt_attention(q, kb, vb, kmean, bias_tiles, batch, seq):
    nq = seq // BLOCK_B
    hd = HEAD_DIM_B
    nh = 4
    n_delta = bias_tiles.shape[1]
    one = pl.Buffered(1)
    return pl.pallas_call(
        functools.partial(_moba_prompt_kernel, nblk=nq, nh=nh),
        grid=(batch, N_HEADS_B // nh, nq),
        in_specs=[
            pl.BlockSpec((BLOCK_B, nh * hd), lambda b, g, i: (b * nq + i, g)),
            pl.BlockSpec((seq, nh * hd), lambda b, g, i: (b, g), pipeline_mode=one),
            pl.BlockSpec((seq, nh * hd), lambda b, g, i: (b, g), pipeline_mode=one),
            pl.BlockSpec((nq, 1, nh * hd), lambda b, g, i: (b, 0, g)),
            pl.BlockSpec((nh, n_delta, BLOCK_B, BLOCK_B), lambda b, g, i: (g, 0, 0, 0), pipeline_mode=one),
        ],
        out_specs=pl.BlockSpec((BLOCK_B, nh * hd), lambda b, g, i: (b * nq + i, g)),
        out_shape=jax.ShapeDtypeStruct((batch * seq, D_MODEL), BF16),
        scratch_shapes=[pltpu.VMEM((nh, BLOCK_B, 2 * hd), BF16)],
        compiler_params=_cparams("parallel", "parallel", "arbitrary"),
        name="moba_prompt_attn",
    )(q, kb, vb, kmean, bias_tiles)


def _moba_sample_kernel(pt_ref, q_ref, kn_ref, vn_ref, *refs, pps, n_pages, t_new):
    kc_refs = refs[:pps]
    vc_refs = refs[pps:2 * pps]
    bias_ref = refs[2 * pps]
    o_ref = refs[2 * pps + 1]
    qbd_scr, qbf_scr, m_scr, l_scr, o_scr, ksum_scr = refs[2 * pps + 2:]
    del pt_ref
    s_id = pl.program_id(1)
    H, hd = N_HEADS_B, HEAD_DIM_B
    nrow = H * t_new
    ppb = BLOCK_B // PAGE_SIZE
    n_blk = n_pages // ppb

    @pl.when(s_id == 0)
    def _():
        q8 = q_ref[0]
        qt = jnp.concatenate([q8] * H, axis=0)
        r_i = lax.broadcasted_iota(jnp.int32, (nrow, H * hd), 0)
        c_i = lax.broadcasted_iota(jnp.int32, (nrow, H * hd), 1)
        qbd = jnp.where(r_i // t_new == c_i // hd, qt, 0.0)
        qbd_scr[...] = qbd
        qbf_scr[...] = (qbd * (1.0 / math.sqrt(hd))).astype(BF16)

    def partial_attn(kp, vp, bias, valid):
        s = _dot_nt(kp.astype(BF16), qbf_scr[...]) + bias
        if valid is not None:
            s = jnp.where(valid, s, MASK_VALUE)
        m = jnp.max(s, axis=0, keepdims=True)
        e = jnp.exp(s - m)
        l = jnp.sum(e, axis=0, keepdims=True)
        o_full = _dot_tn(e.astype(BF16), vp.astype(BF16))
        o = jnp.concatenate(
            [o_full[h * t_new:(h + 1) * t_new, h * hd:(h + 1) * hd] for h in range(H)], axis=0)
        return m, l, o

    for i in range(pps // ppb):
        blk = s_id * (pps // ppb) + i
        kp = jnp.concatenate([kc_refs[i * ppb + u][0, 0] for u in range(ppb)], axis=0)
        vp = jnp.concatenate([vc_refs[i * ppb + u][0, 0] for u in range(ppb)], axis=0)
        ksum_scr[pl.ds(blk, 1), :] = jnp.sum(kp, axis=0, keepdims=True)
        b0 = pl.multiple_of(blk * BLOCK_B, BLOCK_B)
        m, l, o = partial_attn(kp, vp, bias_ref[pl.ds(b0, BLOCK_B), :], None)
        m_scr[pl.ds(blk, 1), :] = m
        l_scr[pl.ds(blk, 1), :] = l
        o_scr[blk] = o

    @pl.when(s_id == pl.num_programs(1) - 1)
    def _():
        n_own = kn_ref.shape[1]
        key_i = lax.broadcasted_iota(jnp.int32, (n_own, nrow), 0)
        qry_i = lax.broadcasted_iota(jnp.int32, (n_own, nrow), 1) % t_new
        past = n_pages * PAGE_SIZE
        m_o, l_o, o_o = partial_attn(kn_ref[0], vn_ref[0], bias_ref[past:past + n_own, :], key_i <= qry_i)

        kmean = ksum_scr[...] * (1.0 / BLOCK_B)
        gate_t = _dot_nt(kmean, qbd_scr[...], precision=HIGHEST)
        bi = lax.broadcasted_iota(jnp.int32, (n_blk, nrow), 0)
        sel = _topk_mask_t(gate_t, [(j, j) for j in range(n_blk)], bi, bi >= 0)

        m_all = m_scr[...]
        l_all = l_scr[...]
        m_fin = jnp.maximum(jnp.max(jnp.where(sel > 0.5, m_all, MASK_VALUE), axis=0, keepdims=True), m_o)
        w = jnp.where(sel > 0.5, jnp.exp(m_all - m_fin), 0.0)
        w_o = jnp.exp(m_o - m_fin)
        l_fin = jnp.sum(w * l_all, axis=0, keepdims=True) + w_o * l_o
        stack = jnp.concatenate([w, w_o, l_fin], axis=0)
        stack = jnp.concatenate([stack, jnp.zeros((stack.shape[0], LANES - nrow), F32)], axis=1)
        stack = jnp.concatenate([stack, jnp.zeros((LANES - stack.shape[0], LANES), F32)], axis=0)
        st = stack.T
        acc = st[:nrow, n_blk:n_blk + 1] * o_o
        for pp in range(n_blk):
            acc = acc + st[:nrow, pp:pp + 1] * o_scr[pp]
        out = acc / st[:nrow, n_blk + 1:n_blk + 2]
        for h in range(H):
            o_ref[0, :, h * hd:(h + 1) * hd] = out[h * t_new:(h + 1) * t_new, :]


def _moba_sample_attention(q, k_new, v_new, cache_k, cache_v, layer, page_table, bias_tab, t_new):
    nb, n_pages = page_table.shape
    pps = 4
    n_steps = n_pages // pps
    H, hd = N_HEADS_B, HEAD_DIM_B
    nrow = H * t_new
    n_layers, n_pool = cache_k.shape[:2]
    ck = cache_k.reshape(n_layers, n_pool, PAGE_SIZE, H * hd)
    cv = cache_v.reshape(n_layers, n_pool, PAGE_SIZE, H * hd)
    n_own = k_new.shape[1]
    n_blk = n_pages * PAGE_SIZE // BLOCK_B

    def page_spec(i):
        return pl.BlockSpec((1, 1, PAGE_SIZE, H * hd),
                            lambda b, s, pt: (layer, pt[b * n_pages + s * pps + i], 0, 0))

    row_spec = pl.BlockSpec((1, t_new, H * hd), lambda b, s, pt: (b, 0, 0))
    own_spec = pl.BlockSpec((1, n_own, H * hd), lambda b, s, pt: (b, 0, 0))
    grid_spec = pltpu.PrefetchScalarGridSpec(
        num_scalar_prefetch=1,
        grid=(nb, n_steps),
        in_specs=[row_spec, own_spec, own_spec]
                 + [page_spec(i) for i in range(pps)] + [page_spec(i) for i in range(pps)]
                 + [pl.BlockSpec(bias_tab.shape, lambda b, s, pt: (0, 0))],
        out_specs=row_spec,
        scratch_shapes=[
            pltpu.VMEM((nrow, H * hd), F32),
            pltpu.VMEM((nrow, H * hd), BF16),
            pltpu.VMEM((n_blk, nrow), F32),
            pltpu.VMEM((n_blk, nrow), F32),
            pltpu.VMEM((n_blk, nrow, hd), F32),
            pltpu.VMEM((n_blk, H * hd), F32),
        ],
    )
    return pl.pallas_call(
        functools.partial(_moba_sample_kernel, pps=pps, n_pages=n_pages, t_new=t_new),
        grid_spec=grid_spec,
        out_shape=jax.ShapeDtypeStruct((nb, t_new, H * hd), F32),
        compiler_params=_cparams("parallel", "arbitrary"),
        name="moba_sample_attn",
    )(page_table.reshape(-1), q, k_new, v_new, *([ck] * pps), *([cv] * pps), bias_tab)


def _proj_kernel(x_ref, gt_ref, a_ref, w_ref, o_ref):
    o_ref[...] = x_ref[...] + _rows(gt_ref) * _dot(a_ref[...].astype(BF16), w_ref[...])


def _proj_residual(x, gate, a, w, per_row, rows_per_batch):
    T = x.shape[0]
    tm = 512
    tpb = rows_per_batch // tm if not per_row else 1
    row_spec = pl.BlockSpec((tm, D_MODEL), lambda i: (i, 0))
    return pl.pallas_call(
        _proj_kernel,
        grid=(T // tm,),
        in_specs=[row_spec, _mod_spec(per_row, tm, tpb, 1), row_spec,
                  _const_spec((D_MODEL, D_MODEL))],
        out_specs=row_spec,
        out_shape=jax.ShapeDtypeStruct((T, D_MODEL), F32),
        compiler_params=_cparams("parallel"),
        name="moba_out_proj",
    )(x, gate, a, w)


def _rel_bucket(dist):
    max_exact = N_BUCKETS // 2
    d = jnp.maximum(dist, 0)
    log_ratio = jnp.log(jnp.maximum(d, 1).astype(jnp.float32) / max_exact) / math.log(MAX_DIST / max_exact)
    large = jnp.minimum(max_exact + (log_ratio * (N_BUCKETS - max_exact)).astype(jnp.int32), N_BUCKETS - 1)
    return jnp.where(d < max_exact, d, large)


def _bias_tables(rel_bias, seq, past_len, t_new, n_own):
    d_max = max(seq, past_len + t_new)
    tb = rel_bias[_rel_bucket(jnp.arange(d_max, dtype=jnp.int32))]
    n_delta = -(-(MAX_DIST + BLOCK_B) // BLOCK_B) + 1
    n_u = n_delta * BLOCK_B
    padded = jnp.concatenate([jnp.broadcast_to(tb[:1], (BLOCK_B - 1, N_HEADS_B)), tb[:n_u]], axis=0)
    rev = jnp.transpose(padded[::-1])
    starts = (n_u - 1) - jnp.arange(n_u, dtype=jnp.int32)
    rows = jax.vmap(lambda s: lax.dynamic_slice_in_dim(rev, s, BLOCK_B, axis=1))(starts)
    tiles = jnp.transpose(rows, (1, 0, 2)).reshape(N_HEADS_B, n_delta, BLOCK_B, BLOCK_B)
    kpos = np.arange(past_len)[:, None]
    qr = np.arange(t_new)[None, :]
    idx_past = past_len + qr - kpos
    idx_own = np.maximum(qr - np.arange(n_own)[:, None], 0)
    idx_s = np.concatenate([idx_past, idx_own], axis=0)
    tab = jnp.transpose(tb[idx_s], (0, 2, 1)).reshape(past_len + n_own, N_HEADS_B * t_new)
    return tiles, tab


def kernel(x_prompt, x_sample, cache_k, cache_v, state_conv, page_table, c_prompt, c_sample, rel_bias, norm_mix, norm_mlp, w_mod, b_mod, w_up, w_down, a_w_in, a_v_gain, a_w_s, a_b_s, a_w_out, b_w_qkv, b_q_gain, b_k_gain, b_w_out, c_w_in, c_conv, c_w_out):
    B, S, D = x_prompt.shape
    DB, T, _ = x_sample.shape
    n_pages = page_table.shape[1]
    past_len = n_pages * PAGE_SIZE
    assert S % BLOCK_B == 0 and past_len % BLOCK_B == 0 and T == SUBLANES
    assert MAX_DIST % BLOCK_B == 0

    xp = x_prompt.reshape(B * S, D)
    xs = x_sample.reshape(DB * T, D)

    n_c = B + DB
    n_c_pad = -(-n_c // SUBLANES) * SUBLANES
    c_all = jnp.concatenate([c_prompt, c_sample, jnp.zeros((n_c_pad - n_c, D), F32)], axis=0)
    mods = _ada_all(c_all, w_mod, b_mod)

    n_own = 2 * SUBLANES
    bias_tiles, bias_tab = _bias_tables(rel_bias, S, past_len, T, n_own)

    ci = np.arange(CHUNK_A)
    tril_p = (ci[:, None] >= ci[None, :])
    tril_s = tril_p & ((ci[:, None] // T) == (ci[None, :] // T))

    k_p, v_p, k_s, v_s, conv_p, conv_s, chunkv_s = [], [], [], [], [], [], []
    for i in range(DEPTH):
        kind, j = i % N_MIXERS, i // N_MIXERS
        mp = [m.reshape(B, 1, D) for m in jnp.split(mods[i, :B], 6, axis=-1)]
        ms = [jnp.repeat(m, T, axis=0) for m in jnp.split(mods[i, B:B + DB], 6, axis=-1)]
        g_mix = norm_mix[i].reshape(1, D)
        g_mlp = norm_mlp[i].reshape(1, D)
        if kind == 0:
            w_in = a_w_in[j].astype(BF16)
            w_out = a_w_out[j].astype(BF16)
            vg = a_v_gain[j].reshape(1, D_A)
            wmix_p = jnp.where(tril_p, a_w_s[j], 0.0).astype(BF16)
            bmix_p = jnp.repeat(jnp.transpose(a_b_s[j]), GROUP_A, axis=1)
            ws_t = jnp.tile(a_w_s[j][:, :T, :T], (1, CHUNK_A // T, CHUNK_A // T))
            wmix_s = jnp.where(tril_s, ws_t, 0.0).astype(BF16)
            bmix_s = jnp.repeat(jnp.tile(jnp.transpose(a_b_s[j][:, :T]), (CHUNK_A // T, 1)), GROUP_A, axis=1)
            (xp,) = _gmlp_layer(xp, mp[0], mp[1], mp[2], g_mix, w_in, vg, wmix_p, bmix_p, w_out,
                                False, S, False)
            xs, v_new = _gmlp_layer(xs, ms[0], ms[1], ms[2], g_mix, w_in, vg, wmix_s, bmix_s, w_out,
                                    True, T, True)
            chunkv_s.append(v_new.reshape(DB, T, D_A))
        elif kind == 1:
            w_qkv = b_w_qkv[j].astype(BF16)
            w_out = b_w_out[j].astype(BF16)
            qg = b_q_gain[j].reshape(1, HEAD_DIM_B)
            kg = b_k_gain[j].reshape(1, HEAD_DIM_B)
            qp, kp, vp, kbp, vbp, kmean_p = _qkv_layer(xp, mp[0], mp[1], g_mix, w_qkv, qg, kg, False, S, True)
            qs, ks, vs = _qkv_layer(xs, ms[0], ms[1], g_mix, w_qkv, qg, kg, True, T, False)
            op = _moba_prompt_attention(qp, kbp, vbp, kmean_p, bias_tiles, B, S)
            pad = ((0, 0), (0, n_own - T), (0, 0))
            os_ = _moba_sample_attention(
                qs.reshape(DB, T, D), jnp.pad(ks.reshape(DB, T, D), pad), jnp.pad(vs.reshape(DB, T, D), pad),
                cache_k, cache_v, j, page_table, bias_tab, T)
            xp = _proj_residual(xp, mp[2], op, w_out, False, S)
            xs = _proj_residual(xs, ms[2], os_.reshape(DB * T, D), w_out, True, T)
            k_p.append(kp.reshape(B, S, N_HEADS_B, HEAD_DIM_B))
            v_p.append(vp.reshape(B, S, N_HEADS_B, HEAD_DIM_B))
            k_s.append(ks.reshape(DB, T, N_HEADS_B, HEAD_DIM_B))
            v_s.append(vs.reshape(DB, T, N_HEADS_B, HEAD_DIM_B))
        else:
            w_in = c_w_in[j].astype(BF16)
            w_out = c_w_out[j].astype(BF16)
            st = state_conv[j]
            zrow = jnp.zeros((DB, T - 1, D_C), F32)
            p1 = jnp.concatenate([st[:, 1:2], zrow], axis=1).reshape(DB * T, D_C)
            p2 = jnp.concatenate([st, zrow[:, 1:]], axis=1).reshape(DB * T, D_C)
            xp, tail_p = _conv_layer(xp, mp[0], mp[1], mp[2], g_mix, w_in, c_conv[j], w_out, False, S)
            xs, xin_s = _conv_layer(xs, ms[0], ms[1], ms[2], g_mix, w_in, c_conv[j], w_out, True, T,
                                    fills=(p1, p2))
            conv_p.append(tail_p[:, SUBLANES - (CONV_W - 1):, :])
            conv_s.append(xin_s.reshape(DB, T, D_C)[:, T - (CONV_W - 1):, :])
        wu = w_up[i].astype(BF16)
        wd = w_down[i].astype(BF16)
        xp = _mlp_layer(xp, mp[3], mp[4], mp[5], g_mlp, wu, wd, False, S)
        xs = _mlp_layer(xs, ms[3], ms[4], ms[5], g_mlp, wu, wd, True, T)
    return (xp.reshape(B, S, D), xs.reshape(DB, T, D), jnp.stack(k_p), jnp.stack(v_p), jnp.stack(k_s),
            jnp.stack(v_s), jnp.stack(conv_p), jnp.stack(conv_s), jnp.stack(chunkv_s))
```

```python
import functools
import math

import jax
import jax.numpy as jnp
import numpy as np
from jax import lax
from jax.experimental import pallas as pl
from jax.experimental.pallas import tpu as pltpu

D_MODEL = 1024
DEPTH = 4
N_MIXERS = 3
D_A = 2 * D_MODEL
N_GROUPS_A = 8
GROUP_A = D_A // N_GROUPS_A
CHUNK_A = 128
HEAD_DIM_B = 128
N_HEADS_B = D_MODEL // HEAD_DIM_B
BLOCK_B = 256
TOPK_B = 3
N_BUCKETS = 32
MAX_DIST = 1024
D_C = D_MODEL
CONV_W = 3
D_FF = 4 * D_MODEL
EPS = 1e-6
MASK_VALUE = -1e30
LOG2E = math.log2(math.e)
PAGE_SIZE = 128

SUBLANES = 8
LANES = 128
VMEM_LIMIT_BYTES = 56 * 1024 * 1024

F32 = jnp.float32
BF16 = jnp.bfloat16
HIGHEST = lax.Precision.HIGHEST


def _cparams(*sem):
    return pltpu.CompilerParams(dimension_semantics=sem, vmem_limit_bytes=VMEM_LIMIT_BYTES)


def _rows(ref):
    v = ref[...]
    return v.reshape(v.shape[-2], v.shape[-1])


def _modulate(x, g, shift, scale):
    y = x * lax.rsqrt(jnp.mean(x * x, axis=-1, keepdims=True) + EPS)
    return (y * g) * (1.0 + scale) + shift


def _gelu_tanh(x):
    c = math.sqrt(2.0 / math.pi)
    return x * (0.5 * (1.0 + jnp.tanh(c * (x + 0.044715 * (x * x * x)))))


def _dot(a, b):
    return jnp.dot(a, b, preferred_element_type=F32)


def _dot_nt(a, b, **kw):
    return lax.dot_general(a, b, (((1,), (1,)), ((), ())), preferred_element_type=F32, **kw)


def _dot_tn(a, b):
    return lax.dot_general(a, b, (((0,), (0,)), ((), ())), preferred_element_type=F32)


def _mod_spec(per_row, tm, tiles_per_batch, ngrid):
    if per_row:
        if ngrid == 1:
            return pl.BlockSpec((tm, D_MODEL), lambda i: (i, 0))
        return pl.BlockSpec((tm, D_MODEL), lambda i, f: (i, 0))
    if ngrid == 1:
        return pl.BlockSpec((1, 1, D_MODEL), lambda i: (i // tiles_per_batch, 0, 0))
    return pl.BlockSpec((1, 1, D_MODEL), lambda i, f: (i // tiles_per_batch, 0, 0))


def _const_spec(shape):
    nd = len(shape)
    return pl.BlockSpec(shape, lambda i: (0,) * nd, pipeline_mode=pl.Buffered(1))


def _mod_kernel(c_ref, w_ref, b_ref, o_ref):
    c = c_ref[...]
    sc = (c * jax.nn.sigmoid(c)).astype(BF16)
    o_ref[0] = _dot(sc, w_ref[0].astype(BF16)) + b_ref[0]


def _ada_all(c_all, w_mod, b_mod):
    nrow = c_all.shape[0]
    tn = 1536
    nn = (6 * D_MODEL) // tn
    return pl.pallas_call(
        _mod_kernel,
        grid=(DEPTH, nn),
        in_specs=[
            pl.BlockSpec((nrow, D_MODEL), lambda l, n: (0, 0)),
            pl.BlockSpec((1, D_MODEL, tn), lambda l, n: (l, 0, n)),
            pl.BlockSpec((1, 1, tn), lambda l, n: (l, 0, n)),
        ],
        out_specs=pl.BlockSpec((1, nrow, tn), lambda l, n: (l, 0, n)),
        out_shape=jax.ShapeDtypeStruct((DEPTH, nrow, 6 * D_MODEL), F32),
        compiler_params=_cparams("arbitrary", "arbitrary"),
        name="ada_mod",
    )(c_all, w_mod, b_mod.reshape(DEPTH, 1, 6 * D_MODEL))


def _mlp_kernel(x_ref, sh_ref, sc_ref, gt_ref, g_ref, wu_ref, wd_ref, o_ref, h_scr, acc_scr):
    f = pl.program_id(1)

    @pl.when(f == 0)
    def _():
        h = _modulate(x_ref[...], g_ref[...], _rows(sh_ref), _rows(sc_ref))
        h_scr[...] = h.astype(BF16)
        acc_scr[...] = jnp.zeros_like(acc_scr)

    a = jnp.maximum(_dot(h_scr[...], wu_ref[...]), 0.0)
    acc_scr[...] += _dot((a * a).astype(BF16), wd_ref[...])

    @pl.when(f == pl.num_programs(1) - 1)
    def _():
        o_ref[...] = x_ref[...] + _rows(gt_ref) * acc_scr[...]


def _mlp_layer(x, shift, scale, gate, g, w_up, w_down, per_row, rows_per_batch):
    T = x.shape[0]
    tm = 1024
    tf = 512
    tpb = rows_per_batch // tm if not per_row else 1
    ms = _mod_spec(per_row, tm, tpb, 2)
    return pl.pallas_call(
        _mlp_kernel,
        grid=(T // tm, D_FF // tf),
        in_specs=[
            pl.BlockSpec((tm, D_MODEL), lambda i, f: (i, 0)),
            ms, ms, ms,
            pl.BlockSpec((1, D_MODEL), lambda i, f: (0, 0)),
            pl.BlockSpec((D_MODEL, tf), lambda i, f: (0, f)),
            pl.BlockSpec((tf, D_MODEL), lambda i, f: (f, 0)),
        ],
        out_specs=pl.BlockSpec((tm, D_MODEL), lambda i, f: (i, 0)),
        out_shape=jax.ShapeDtypeStruct((T, D_MODEL), F32),
        scratch_shapes=[pltpu.VMEM((tm, D_MODEL), BF16), pltpu.VMEM((tm, D_MODEL), F32)],
        compiler_params=_cparams("parallel", "arbitrary"),
        name="mlp",
    )(x, shift, scale, gate, g, w_up, w_down)


def _gmlp_kernel(x_ref, sh_ref, sc_ref, gt_ref, g_ref, win_ref, vg_ref, wmix_ref, bmix_ref, wout_ref,
                 *out_refs, tm, emit_v):
    o_ref = out_refs[0]
    gs_scr = out_refs[-1]
    x = x_ref[...]
    h = _modulate(x, g_ref[...], _rows(sh_ref), _rows(sc_ref)).astype(BF16)
    z = _gelu_tanh(_dot(h, win_ref[...]))
    u = z[:, :D_A]
    v = z[:, D_A:]
    v = v * lax.rsqrt(jnp.mean(v * v, axis=-1, keepdims=True) + EPS) * vg_ref[...]
    if emit_v:
        out_refs[1][...] = v
    vb = v.astype(BF16)
    for c in range(tm // CHUNK_A):
        r0 = c * CHUNK_A
        for gi in range(N_GROUPS_A):
            c0 = gi * GROUP_A
            s = _dot(wmix_ref[gi], vb[r0:r0 + CHUNK_A, c0:c0 + GROUP_A])
            s = s + bmix_ref[:, c0:c0 + GROUP_A]
            gs_scr[r0:r0 + CHUNK_A, c0:c0 + GROUP_A] = (u[r0:r0 + CHUNK_A, c0:c0 + GROUP_A] * s).astype(BF16)
    y = _dot(gs_scr[...], wout_ref[...])
    o_ref[...] = x + _rows(gt_ref) * y


def _gmlp_layer(x, shift, scale, gate, g, w_in, v_gain, w_mix, b_mix, w_out, per_row, rows_per_batch,
                emit_v):
    T = x.shape[0]
    tm = 512
    tpb = rows_per_batch // tm if not per_row else 1
    ms = _mod_spec(per_row, tm, tpb, 1)
    out_shape = [jax.ShapeDtypeStruct((T, D_MODEL), F32)]
    out_specs = [pl.BlockSpec((tm, D_MODEL), lambda i: (i, 0))]
    if emit_v:
        out_shape.append(jax.ShapeDtypeStruct((T, D_A), F32))
        out_specs.append(pl.BlockSpec((tm, D_A), lambda i: (i, 0)))
    return pl.pallas_call(
        functools.partial(_gmlp_kernel, tm=tm, emit_v=emit_v),
        grid=(T // tm,),
        in_specs=[
            pl.BlockSpec((tm, D_MODEL), lambda i: (i, 0)),
            ms, ms, ms,
            _const_spec((1, D_MODEL)),
            _const_spec((D_MODEL, 2 * D_A)),
            _const_spec((1, D_A)),
            _const_spec((N_GROUPS_A, CHUNK_A, CHUNK_A)),
            _const_spec((CHUNK_A, D_A)),
            _const_spec((D_A, D_MODEL)),
        ],
        out_specs=out_specs,
        out_shape=out_shape,
        scratch_shapes=[pltpu.VMEM((tm, D_A), BF16)],
        compiler_params=_cparams("parallel"),
        name="gmlp_mixer",
    )(x, shift, scale, gate, g, w_in, v_gain, w_mix, b_mix, w_out)


def _conv_kernel(*refs, tm, per_row, tiles_per_batch):
    if per_row:
        (x_ref, sh_ref, sc_ref, gt_ref, g_ref, win_ref, cw_ref, wout_ref, p1_ref, p2_ref,
         o_ref, xin_ref) = refs
    else:
        (x_ref, sh_ref, sc_ref, gt_ref, g_ref, win_ref, cw_ref, wout_ref,
         o_ref, tail_ref, carry_scr) = refs
    x = x_ref[...]
    h = _modulate(x, g_ref[...], _rows(sh_ref), _rows(sc_ref)).astype(BF16)
    bcx = _dot(h, win_ref[...])
    gate_out = bcx[:, :D_C]
    xin = bcx[:, D_C:2 * D_C] * bcx[:, 2 * D_C:]
    row = lax.broadcasted_iota(jnp.int32, (tm, D_C), 0)
    roll1 = pltpu.roll(xin, 1, 0)
    roll2 = pltpu.roll(xin, 2, 0)
    if per_row:
        pos = row % SUBLANES
        prev1 = jnp.where(pos == 0, p1_ref[...], roll1)
        prev2 = jnp.where(pos < 2, p2_ref[...], roll2)
        xin_ref[...] = xin
    else:
        @pl.when(pl.program_id(0) % tiles_per_batch == 0)
        def _():
            carry_scr[...] = jnp.zeros_like(carry_scr)
        c6 = carry_scr[SUBLANES - 2:SUBLANES - 1, :]
        c7 = carry_scr[SUBLANES - 1:SUBLANES, :]
        prev1 = jnp.where(row == 0, c7, roll1)
        prev2 = jnp.where(row == 0, c6, jnp.where(row == 1, c7, roll2))
        tail = xin[tm - SUBLANES:, :]
        carry_scr[...] = tail
        tail_ref[0] = tail
    y = cw_ref[0:1, :] * prev2 + cw_ref[1:2, :] * prev1 + cw_ref[2:3, :] * xin
    out = _dot((gate_out * y).astype(BF16), wout_ref[...])
    o_ref[...] = x + _rows(gt_ref) * out


def _conv_layer(x, shift, scale, gate, g, w_in, conv_w, w_out, per_row, rows_per_batch, fills=None):
    T = x.shape[0]
    tm = 512
    tpb = rows_per_batch // tm if not per_row else 1
    ms = _mod_spec(per_row, tm, tpb, 1)
    in_specs = [
        pl.BlockSpec((tm, D_MODEL), lambda i: (i, 0)),
        ms, ms, ms,
        _const_spec((1, D_MODEL)),
        _const_spec((D_MODEL, 3 * D_C)),
        _const_spec((CONV_W, D_C)),
        _const_spec((D_C, D_MODEL)),
    ]
    args = [x, shift, scale, gate, g, w_in, conv_w, w_out]
    out_shape = [jax.ShapeDtypeStruct((T, D_MODEL), F32)]
    out_specs = [pl.BlockSpec((tm, D_MODEL), lambda i: (i, 0))]
    scratch = []
    if per_row:
        in_specs += [pl.BlockSpec((tm, D_C), lambda i: (i, 0))] * 2
        args += list(fills)
        out_shape.append(jax.ShapeDtypeStruct((T, D_C), F32))
        out_specs.append(pl.BlockSpec((tm, D_C), lambda i: (i, 0)))
        sem = "parallel"
    else:
        nb = T // rows_per_batch
        out_shape.append(jax.ShapeDtypeStruct((nb, SUBLANES, D_C), F32))
        out_specs.append(pl.BlockSpec((1, SUBLANES, D_C), lambda i: (i // tpb, 0, 0)))
        scratch.append(pltpu.VMEM((SUBLANES, D_C), F32))
        sem = "arbitrary"
    return pl.pallas_call(
        functools.partial(_conv_kernel, tm=tm, per_row=per_row, tiles_per_batch=tpb),
        grid=(T // tm,),
        in_specs=in_specs,
        out_specs=out_specs,
        out_shape=out_shape,
        scratch_shapes=scratch,
        compiler_params=_cparams(sem),
        name="conv_mixer",
    )(*args)


def _qkv_kernel(x_ref, sh_ref, sc_ref, g_ref, w_ref, qg_ref, kg_ref, q_ref, k_ref, v_ref, *extra,
                tm, emit_attn_inputs):
    h = _modulate(x_ref[...], g_ref[...], _rows(sh_ref), _rows(sc_ref)).astype(BF16)
    qkv = _dot(h, w_ref[...])
    hd = HEAD_DIM_B
    for hh in range(N_HEADS_B):
        cols = slice(hh * hd, (hh + 1) * hd)
        q = qkv[:, hh * hd:(hh + 1) * hd]
        k = qkv[:, D_MODEL + hh * hd:D_MODEL + (hh + 1) * hd]
        q_ref[:, cols] = q * lax.rsqrt(jnp.mean(q * q, axis=-1, keepdims=True) + EPS) * qg_ref[...]
        kn = k * lax.rsqrt(jnp.mean(k * k, axis=-1, keepdims=True) + EPS) * kg_ref[...]
        k_ref[:, cols] = kn
        if emit_attn_inputs:
            kb_ref, vb_ref, km_ref = extra
            kb_ref[:, cols] = kn.astype(BF16)
            for c in range(tm // BLOCK_B):
                km_ref[c, :, cols] = jnp.sum(kn[c * BLOCK_B:(c + 1) * BLOCK_B], axis=0, keepdims=True) * (
                    1.0 / BLOCK_B)
    v = qkv[:, 2 * D_MODEL:]
    v_ref[...] = v
    if emit_attn_inputs:
        extra[1][...] = v.astype(BF16)


def _qkv_layer(x, shift, scale, g, w_qkv, q_gain, k_gain, per_row, rows_per_batch, emit_attn_inputs):
    T = x.shape[0]
    tm = 512
    tpb = rows_per_batch // tm if not per_row else 1
    ms = _mod_spec(per_row, tm, tpb, 1)
    row_spec = pl.BlockSpec((tm, D_MODEL), lambda i: (i, 0))
    out_specs = [row_spec, row_spec, row_spec]
    out_shape = [jax.ShapeDtypeStruct((T, D_MODEL), F32)] * 3
    if emit_attn_inputs:
        out_specs += [row_spec, row_spec,
                      pl.BlockSpec((tm // BLOCK_B, 1, D_MODEL), lambda i: (i, 0, 0))]
        out_shape += [jax.ShapeDtypeStruct((T, D_MODEL), BF16)] * 2
        out_shape += [jax.ShapeDtypeStruct((T // BLOCK_B, 1, D_MODEL), F32)]
    return pl.pallas_call(
        functools.partial(_qkv_kernel, tm=tm, emit_attn_inputs=emit_attn_inputs),
        grid=(T // tm,),
        in_specs=[row_spec, ms, ms,
                  _const_spec((1, D_MODEL)),
                  _const_spec((D_MODEL, 3 * D_MODEL)),
                  _const_spec((1, HEAD_DIM_B)),
                  _const_spec((1, HEAD_DIM_B))],
        out_specs=out_specs,
        out_shape=out_shape,
        compiler_params=_cparams("parallel"),
        name="moba_qkv",
    )(x, shift, scale, g, w_qkv, q_gain, k_gain)


def _topk_mask_t(gate_t, cands, blk_of_row, elig):
    gm = jnp.where(elig, gate_t, MASK_VALUE)
    cnt = jnp.zeros(gate_t.shape, F32)
    for r, cb in cands:
        row = gm[r:r + 1, :]
        beats = jnp.where(row > gm, 1.0, jnp.where(row == gm, jnp.where(cb < blk_of_row, 1.0, 0.0), 0.0))
        cnt = cnt + beats
    return jnp.where(elig, jnp.where(cnt < TOPK_B, 1.0, 0.0), 0.0)


def _moba_prompt_kernel(q_ref, kb_ref, vb_ref, km_ref, bias_ref, o_ref, qaug_scr, *, nblk, nh):
    qb = pl.program_id(2)
    hd = HEAD_DIM_B
    W = BLOCK_B
    n_near = bias_ref.shape[2] // W
    dead_lane = LANES - 1
    lane = lax.broadcasted_iota(jnp.int32, (W, LANES), 1)
    blk = lax.broadcasted_iota(jnp.int32, (nblk, W), 0)
    r_i = lax.broadcasted_iota(jnp.int32, (W, W), 0)
    c_i = lax.broadcasted_iota(jnp.int32, (W, W), 1)
    causal = r_i >= c_i

    def onehot(b, valid):
        return jnp.where(lane == jnp.where(valid, b, dead_lane), 1.0, 0.0).astype(BF16)

    def rows_of(b):
        return pl.ds(pl.multiple_of(b * W, W), W)

    near_blk = [jnp.maximum(qb - d, 0) for d in range(n_near)]
    oh_near = jnp.concatenate([onehot(qb - d, qb - d >= 0) for d in range(n_near)], axis=0)
    ones_near = jnp.ones((n_near * W, hd), BF16)
    carry0 = []
    for h in range(nh):
        cols = slice(h * hd, (h + 1) * hd)
        q = q_ref[:, cols]
        gate_t = _dot_nt(km_ref[:, 0, cols].astype(BF16), q.astype(BF16))
        sel_t = _topk_mask_t(gate_t, [(j, j) for j in range(nblk)], blk, blk < qb)
        neg_t = jnp.where(blk == qb, 0.0, jnp.where(sel_t > 0.5, 0.0, MASK_VALUE))
        neg = jnp.concatenate([neg_t, jnp.zeros((LANES - nblk - 1, W), F32),
                               jnp.full((1, W), MASK_VALUE, F32)], axis=0).T
        q_aug = jnp.concatenate([(q * (LOG2E / math.sqrt(hd))).astype(BF16), neg.astype(BF16)], axis=1)
        qaug_scr[h] = q_aug
        k_near = jnp.concatenate([kb_ref[rows_of(b), cols] for b in near_blk], axis=0)
        v_near = jnp.concatenate([vb_ref[rows_of(b), cols] for b in near_blk], axis=0)
        s = _dot_nt(q_aug, jnp.concatenate([k_near, oh_near], axis=1)) + bias_ref[h]
        s = jnp.concatenate([jnp.where(causal, s[:, :W], MASK_VALUE), s[:, W:]], axis=1)
        m0 = jnp.max(s, axis=-1, keepdims=True)
        p = jnp.exp2(s - m0)
        carry0 += [m0, _dot(p.astype(BF16), jnp.concatenate([v_near, ones_near], axis=1))]

    n_far = jnp.maximum(qb - (n_near - 1), 0)
    ones_pair = jnp.ones((2 * W, hd), BF16)

    def body(i, carry):
        b0 = 2 * i
        b1 = b0 + 1
        oh = jnp.concatenate([onehot(b0, True), onehot(b1, b1 < n_far)], axis=0)
        out = []
        for h in range(nh):
            cols = slice(h * hd, (h + 1) * hd)
            m, acc = carry[2 * h], carry[2 * h + 1]
            k_pair = jnp.concatenate([kb_ref[rows_of(b0), cols], kb_ref[rows_of(b1), cols]], axis=0)
            v_pair = jnp.concatenate([vb_ref[rows_of(b0), cols], vb_ref[rows_of(b1), cols]], axis=0)
            s = _dot_nt(qaug_scr[h], jnp.concatenate([k_pair, oh], axis=1))
            m_new = jnp.maximum(m, jnp.max(s, axis=-1, keepdims=True))
            p = jnp.exp2(s - m_new)
            acc = jnp.exp2(m - m_new) * acc + _dot(p.astype(BF16), jnp.concatenate([v_pair, ones_pair], axis=1))
            out += [m_new, acc]
        return tuple(out)

    carry = lax.fori_loop(0, (n_far + 1) // 2, body, tuple(carry0))
    for h in range(nh):
        acc = carry[2 * h + 1]
        o_ref[:, h * hd:(h + 1) * hd] = (acc[:, :hd] / acc[:, hd:]).astype(o_ref.dtype)


def _moba_prompt_attention(q, kb, vb, kmean, bias_near, batch, seq):
    nq = seq // BLOCK_B
    hd = HEAD_DIM_B
    nh = 4
    one = pl.Buffered(1)
    return pl.pallas_call(
        functools.partial(_moba_prompt_kernel, nblk=nq, nh=nh),
        grid=(batch, N_HEADS_B // nh, nq),
        in_specs=[
            pl.BlockSpec((BLOCK_B, nh * hd), lambda b, g, i: (b * nq + i, g)),
            pl.BlockSpec((seq, nh * hd), lambda b, g, i: (b, g), pipeline_mode=one),
            pl.BlockSpec((seq, nh * hd), lambda b, g, i: (b, g), pipeline_mode=one),
            pl.BlockSpec((nq, 1, nh * hd), lambda b, g, i: (b, 0, g)),
            pl.BlockSpec((nh,) + bias_near.shape[1:], lambda b, g, i: (g, 0, 0), pipeline_mode=one),
        ],
        out_specs=pl.BlockSpec((BLOCK_B, nh * hd), lambda b, g, i: (b * nq + i, g)),
        out_shape=jax.ShapeDtypeStruct((batch * seq, D_MODEL), BF16),
        scratch_shapes=[pltpu.VMEM((nh, BLOCK_B, 2 * hd), BF16)],
        compiler_params=_cparams("parallel", "parallel", "arbitrary"),
        name="moba_prompt_attn",
    )(q, kb, vb, kmean, bias_near)


def _moba_sample_kernel(pt_ref, q_ref, kn_ref, vn_ref, *refs, pps, n_pages, t_new):
    kc_refs = refs[:pps]
    vc_refs = refs[pps:2 * pps]
    bias_ref, bias_own_ref = refs[2 * pps:2 * pps + 2]
    o_ref = refs[2 * pps + 2]
    qs_scr, qg_scr, m_scr, l_scr, o_scr, ksum_scr = refs[2 * pps + 3:]
    del pt_ref
    s_id = pl.program_id(1)
    H, hd = N_HEADS_B, HEAD_DIM_B
    nrow = H * t_new
    ppb = BLOCK_B // PAGE_SIZE
    n_blk = n_pages // ppb
    blk_rows = BLOCK_B * H

    @pl.when(s_id == 0)
    def _():
        q8 = q_ref[0]
        qall = jnp.concatenate([q8[:, h * hd:(h + 1) * hd] for h in range(H)], axis=0)
        qg_scr[...] = qall.astype(BF16)
        qs_scr[...] = (qall * (1.0 / math.sqrt(hd))).astype(BF16)

    own_head = (lax.broadcasted_iota(jnp.int32, (H, nrow), 0)
                == lax.broadcasted_iota(jnp.int32, (H, nrow), 1) // t_new)

    def partial_attn(k_rows, v_rows, bias, valid):
        n = k_rows.shape[0]
        g = _dot_nt(k_rows.astype(BF16), qs_scr[...]) + bias
        if valid is not None:
            g = jnp.where(valid, g, MASK_VALUE)
        g3 = g.reshape(n // H, H, nrow)
        m_t = jnp.max(g3, axis=0)
        e3 = jnp.exp(g3 - jnp.where(own_head, m_t, 0.0)[None])
        l_t = jnp.sum(e3, axis=0)
        o = _dot_tn(e3.reshape(n, nrow).astype(BF16), v_rows.astype(BF16))
        return m_t, l_t, o

    for i in range(pps // ppb):
        blk = s_id * (pps // ppb) + i
        k_rows = jnp.concatenate([kc_refs[i * ppb + u][0, 0].reshape(PAGE_SIZE * H, hd) for u in range(ppb)],
                                 axis=0)
        v_rows = jnp.concatenate([vc_refs[i * ppb + u][0, 0].reshape(PAGE_SIZE * H, hd) for u in range(ppb)],
                                 axis=0)
        ksum_scr[blk] = jnp.sum(k_rows.reshape(BLOCK_B, H, hd), axis=0)
        b0 = pl.multiple_of(blk * blk_rows, blk_rows)
        m_t, l_t, o = partial_attn(k_rows, v_rows, bias_ref[pl.ds(b0, blk_rows), :], None)
        m_scr[blk] = m_t
        l_scr[blk] = l_t
        o_scr[blk] = o

    @pl.when(s_id == pl.num_programs(1) - 1)
    def _():
        def head_diag(t):
            return jnp.sum(jnp.where(own_head[None], t, 0.0), axis=1)

        key_i = lax.broadcasted_iota(jnp.int32, (t_new * H, nrow), 0) // H
        qry_i = lax.broadcasted_iota(jnp.int32, (t_new * H, nrow), 1) % t_new
        m_ot, l_ot, o_o = partial_attn(kn_ref[0].reshape(t_new * H, hd), vn_ref[0].reshape(t_new * H, hd),
                                       bias_own_ref[...], key_i <= qry_i)
        m_o = head_diag(m_ot[None])
        l_o = head_diag(l_ot[None])

        kmean = (ksum_scr[...] * (1.0 / BLOCK_B)).reshape(n_blk * H, hd)
        gate = _dot_nt(kmean.astype(BF16), qg_scr[...])
        gate_t = head_diag(gate.reshape(n_blk, H, nrow))
        bi = lax.broadcasted_iota(jnp.int32, (n_blk, nrow), 0)
        sel = _topk_mask_t(gate_t, [(j, j) for j in range(n_blk)], bi, bi >= 0)

        m_all = head_diag(m_scr[...])
        l_all = head_diag(l_scr[...])
        m_fin = jnp.maximum(jnp.max(jnp.where(sel > 0.5, m_all, MASK_VALUE), axis=0, keepdims=True), m_o)
        w = jnp.where(sel > 0.5, jnp.exp(m_all - m_fin), 0.0)
        w_o = jnp.exp(m_o - m_fin)
        l_fin = jnp.sum(w * l_all, axis=0, keepdims=True) + w_o * l_o
        stack = jnp.concatenate([w, w_o, l_fin], axis=0)
        stack = jnp.concatenate([stack, jnp.zeros((stack.shape[0], LANES - nrow), F32)], axis=1)
        stack = jnp.concatenate([stack, jnp.zeros((LANES - stack.shape[0], LANES), F32)], axis=0)
        st = stack.T
        acc = st[:nrow, n_blk:n_blk + 1] * o_o
        for pp in range(n_blk):
            acc = acc + st[:nrow, pp:pp + 1] * o_scr[pp]
        out = acc / st[:nrow, n_blk + 1:n_blk + 2]
        for h in range(H):
            o_ref[0, :, h * hd:(h + 1) * hd] = out[h * t_new:(h + 1) * t_new, :]


def _moba_sample_attention(q, k_new, v_new, cache_k, cache_v, layer, page_table, bias_past, bias_own, t_new):
    nb, n_pages = page_table.shape
    pps = 4
    n_steps = n_pages // pps
    H, hd = N_HEADS_B, HEAD_DIM_B
    nrow = H * t_new
    n_blk = n_pages * PAGE_SIZE // BLOCK_B

    def page_spec(i):
        return pl.BlockSpec((1, 1, PAGE_SIZE, H, hd),
                            lambda b, s, pt: (layer, pt[b * n_pages + s * pps + i], 0, 0, 0))

    row_spec = pl.BlockSpec((1, t_new, H * hd), lambda b, s, pt: (b, 0, 0))
    own_spec = pl.BlockSpec((1, t_new, H, hd), lambda b, s, pt: (b, 0, 0, 0))
    one = pl.Buffered(1)
    grid_spec = pltpu.PrefetchScalarGridSpec(
        num_scalar_prefetch=1,
        grid=(nb, n_steps),
        in_specs=[row_spec, own_spec, own_spec]
                 + [page_spec(i) for i in range(pps)] + [page_spec(i) for i in range(pps)]
                 + [pl.BlockSpec(bias_past.shape, lambda b, s, pt: (0, 0), pipeline_mode=one),
                    pl.BlockSpec(bias_own.shape, lambda b, s, pt: (0, 0), pipeline_mode=one)],
        out_specs=row_spec,
        scratch_shapes=[
            pltpu.VMEM((nrow, hd), BF16),
            pltpu.VMEM((nrow, hd), BF16),
            pltpu.VMEM((n_blk, H, nrow), F32),
            pltpu.VMEM((n_blk, H, nrow), F32),
            pltpu.VMEM((n_blk, nrow, hd), F32),
            pltpu.VMEM((n_blk, H, hd), F32),
        ],
    )
    return pl.pallas_call(
        functools.partial(_moba_sample_kernel, pps=pps, n_pages=n_pages, t_new=t_new),
        grid_spec=grid_spec,
        out_shape=jax.ShapeDtypeStruct((nb, t_new, H * hd), F32),
        compiler_params=_cparams("parallel", "arbitrary"),
        name="moba_sample_attn",
    )(page_table.reshape(-1), q, k_new, v_new, *([cache_k] * pps), *([cache_v] * pps), bias_past, bias_own)


def _proj_kernel(x_ref, gt_ref, a_ref, w_ref, o_ref):
    o_ref[...] = x_ref[...] + _rows(gt_ref) * _dot(a_ref[...].astype(BF16), w_ref[...])


def _proj_residual(x, gate, a, w, per_row, rows_per_batch):
    T = x.shape[0]
    tm = 512
    tpb = rows_per_batch // tm if not per_row else 1
    row_spec = pl.BlockSpec((tm, D_MODEL), lambda i: (i, 0))
    return pl.pallas_call(
        _proj_kernel,
        grid=(T // tm,),
        in_specs=[row_spec, _mod_spec(per_row, tm, tpb, 1), row_spec,
                  _const_spec((D_MODEL, D_MODEL))],
        out_specs=row_spec,
        out_shape=jax.ShapeDtypeStruct((T, D_MODEL), F32),
        compiler_params=_cparams("parallel"),
        name="moba_out_proj",
    )(x, gate, a, w)


def _rel_bucket(dist):
    max_exact = N_BUCKETS // 2
    d = jnp.maximum(dist, 0)
    log_ratio = jnp.log(jnp.maximum(d, 1).astype(jnp.float32) / max_exact) / math.log(MAX_DIST / max_exact)
    large = jnp.minimum(max_exact + (log_ratio * (N_BUCKETS - max_exact)).astype(jnp.int32), N_BUCKETS - 1)
    return jnp.where(d < max_exact, d, large)


def _bias_tables(rel_bias, past_len, t_new):
    d_max = past_len + t_new
    tb = rel_bias[_rel_bucket(jnp.arange(d_max, dtype=jnp.int32))]
    n_near = MAX_DIST // BLOCK_B + 1
    r = np.arange(BLOCK_B)[:, None]
    col = np.arange(n_near * BLOCK_B)[None, :]
    dist = (col // BLOCK_B) * BLOCK_B + r - (col % BLOCK_B)
    onehot = (_rel_bucket(jnp.asarray(dist, jnp.int32))[..., None] == jnp.arange(N_BUCKETS)).astype(F32)
    near = jnp.einsum("rcb,bh->hrc", onehot, rel_bias - rel_bias[N_BUCKETS - 1], precision=HIGHEST) * LOG2E
    kpos = np.arange(past_len)[:, None]
    qr = np.arange(t_new)[None, :]
    idx_own = np.maximum(qr - np.arange(t_new)[:, None], 0)
    same_head = np.eye(N_HEADS_B, dtype=bool)[None, :, :, None]

    def per_head(idx):
        b = jnp.transpose(tb[idx], (0, 2, 1))[:, None]
        b = jnp.where(same_head, b, MASK_VALUE)
        return b.reshape(idx.shape[0] * N_HEADS_B, N_HEADS_B * t_new)

    return near, per_head(past_len + qr - kpos), per_head(idx_own)


def kernel(x_prompt, x_sample, cache_k, cache_v, state_conv, page_table, c_prompt, c_sample, rel_bias, norm_mix, norm_mlp, w_mod, b_mod, w_up, w_down, a_w_in, a_v_gain, a_w_s, a_b_s, a_w_out, b_w_qkv, b_q_gain, b_k_gain, b_w_out, c_w_in, c_conv, c_w_out):
    B, S, D = x_prompt.shape
    DB, T, _ = x_sample.shape
    n_pages = page_table.shape[1]
    past_len = n_pages * PAGE_SIZE
    assert S % BLOCK_B == 0 and past_len % BLOCK_B == 0 and T == SUBLANES
    assert MAX_DIST % BLOCK_B == 0

    xp = x_prompt.reshape(B * S, D)
    xs = x_sample.reshape(DB * T, D)

    n_c = B + DB
    n_c_pad = -(-n_c // SUBLANES) * SUBLANES
    c_all = jnp.concatenate([c_prompt, c_sample, jnp.zeros((n_c_pad - n_c, D), F32)], axis=0)
    mods = _ada_all(c_all, w_mod, b_mod)

    bias_near, bias_past, bias_own = _bias_tables(rel_bias, past_len, T)

    ci = np.arange(CHUNK_A)
    tril_p = (ci[:, None] >= ci[None, :])
    tril_s = tril_p & ((ci[:, None] // T) == (ci[None, :] // T))

    k_p, v_p, k_s, v_s, conv_p, conv_s, chunkv_s = [], [], [], [], [], [], []
    for i in range(DEPTH):
        kind, j = i % N_MIXERS, i // N_MIXERS
        mp = [m.reshape(B, 1, D) for m in jnp.split(mods[i, :B], 6, axis=-1)]
        ms = [jnp.repeat(m, T, axis=0) for m in jnp.split(mods[i, B:B + DB], 6, axis=-1)]
        g_mix = norm_mix[i].reshape(1, D)
        g_mlp = norm_mlp[i].reshape(1, D)
        if kind == 0:
            w_in = a_w_in[j].astype(BF16)
            w_out = a_w_out[j].astype(BF16)
            vg = a_v_gain[j].reshape(1, D_A)
            wmix_p = jnp.where(tril_p, a_w_s[j], 0.0).astype(BF16)
            bmix_p = jnp.repeat(jnp.transpose(a_b_s[j]), GROUP_A, axis=1)
            ws_t = jnp.tile(a_w_s[j][:, :T, :T], (1, CHUNK_A // T, CHUNK_A // T))
            wmix_s = jnp.where(tril_s, ws_t, 0.0).astype(BF16)
            bmix_s = jnp.repeat(jnp.tile(jnp.transpose(a_b_s[j][:, :T]), (CHUNK_A // T, 1)), GROUP_A, axis=1)
            (xp,) = _gmlp_layer(xp, mp[0], mp[1], mp[2], g_mix, w_in, vg, wmix_p, bmix_p, w_out,
                                False, S, False)
            xs, v_new = _gmlp_layer(xs, ms[0], ms[1], ms[2], g_mix, w_in, vg, wmix_s, bmix_s, w_out,
                                    True, T, True)
            chunkv_s.append(v_new.reshape(DB, T, D_A))
        elif kind == 1:
            w_qkv = b_w_qkv[j].astype(BF16)
            w_out = b_w_out[j].astype(BF16)
            qg = b_q_gain[j].reshape(1, HEAD_DIM_B)
            kg = b_k_gain[j].reshape(1, HEAD_DIM_B)
            qp, kp, vp, kbp, vbp, kmean_p = _qkv_layer(xp, mp[0], mp[1], g_mix, w_qkv, qg, kg, False, S, True)
            qs, ks, vs = _qkv_layer(xs, ms[0], ms[1], g_mix, w_qkv, qg, kg, True, T, False)
            op = _moba_prompt_attention(qp, kbp, vbp, kmean_p, bias_near, B, S)
            os_ = _moba_sample_attention(
                qs.reshape(DB, T, D), ks.reshape(DB, T, N_HEADS_B, HEAD_DIM_B),
                vs.reshape(DB, T, N_HEADS_B, HEAD_DIM_B), cache_k, cache_v, j, page_table,
                bias_past, bias_own, T)
            xp = _proj_residual(xp, mp[2], op, w_out, False, S)
            xs = _proj_residual(xs, ms[2], os_.reshape(DB * T, D), w_out, True, T)
            k_p.append(kp.reshape(B, S, N_HEADS_B, HEAD_DIM_B))
            v_p.append(vp.reshape(B, S, N_HEADS_B, HEAD_DIM_B))
            k_s.append(ks.reshape(DB, T, N_HEADS_B, HEAD_DIM_B))
            v_s.append(vs.reshape(DB, T, N_HEADS_B, HEAD_DIM_B))
        else:
            w_in = c_w_in[j].astype(BF16)
            w_out = c_w_out[j].astype(BF16)
            st = state_conv[j]
            zrow = jnp.zeros((DB, T - 1, D_C), F32)
            p1 = jnp.concatenate([st[:, 1:2], zrow], axis=1).reshape(DB * T, D_C)
            p2 = jnp.concatenate([st, zrow[:, 1:]], axis=1).reshape(DB * T, D_C)
            xp, tail_p = _conv_layer(xp, mp[0], mp[1], mp[2], g_mix, w_in, c_conv[j], w_out, False, S)
            xs, xin_s = _conv_layer(xs, ms[0], ms[1], ms[2], g_mix, w_in, c_conv[j], w_out, True, T,
                                    fills=(p1, p2))
            conv_p.append(tail_p[:, SUBLANES - (CONV_W - 1):, :])
            conv_s.append(xin_s.reshape(DB, T, D_C)[:, T - (CONV_W - 1):, :])
        wu = w_up[i].astype(BF16)
        wd = w_down[i].astype(BF16)
        xp = _mlp_layer(xp, mp[3], mp[4], mp[5], g_mlp, wu, wd, False, S)
        xs = _mlp_layer(xs, ms[3], ms[4], ms[5], g_mlp, wu, wd, True, T)
    return (xp.reshape(B, S, D), xs.reshape(DB, T, D), jnp.stack(k_p), jnp.stack(v_p), jnp.stack(k_s),
            jnp.stack(v_s), jnp.stack(conv_p), jnp.stack(conv_s), jnp.stack(chunkv_s))
```

```python
import functools
import math

import jax
import jax.numpy as jnp
import numpy as np
from jax import lax
from jax.experimental import pallas as pl
from jax.experimental.pallas import tpu as pltpu

D_MODEL = 1024
DEPTH = 4
N_MIXERS = 3
D_A = 2 * D_MODEL
N_GROUPS_A = 8
GROUP_A = D_A // N_GROUPS_A
CHUNK_A = 128
HEAD_DIM_B = 128
N_HEADS_B = D_MODEL // HEAD_DIM_B
BLOCK_B = 256
TOPK_B = 3
N_BUCKETS = 32
MAX_DIST = 1024
D_C = D_MODEL
CONV_W = 3
D_FF = 4 * D_MODEL
EPS = 1e-6
MASK_VALUE = -1e30
LOG2E = math.log2(math.e)
PAGE_SIZE = 128

SUBLANES = 8
LANES = 128
VMEM_LIMIT_BYTES = 56 * 1024 * 1024

F32 = jnp.float32
BF16 = jnp.bfloat16
HIGHEST = lax.Precision.HIGHEST


def _cparams(*sem):
    return pltpu.CompilerParams(dimension_semantics=sem, vmem_limit_bytes=VMEM_LIMIT_BYTES)


def _rows(ref):
    v = ref[...]
    return v.reshape(v.shape[-2], v.shape[-1])


def _modulate(x, g, shift, scale):
    y = x * lax.rsqrt(jnp.mean(x * x, axis=-1, keepdims=True) + EPS)
    return (y * g) * (1.0 + scale) + shift


def _gelu_tanh(x):
    c = math.sqrt(2.0 / math.pi)
    return x * (0.5 * (1.0 + jnp.tanh(c * (x + 0.044715 * (x * x * x)))))


def _dot(a, b):
    return jnp.dot(a, b, preferred_element_type=F32)


def _dot_nt(a, b, **kw):
    return lax.dot_general(a, b, (((1,), (1,)), ((), ())), preferred_element_type=F32, **kw)


def _dot_tn(a, b):
    return lax.dot_general(a, b, (((0,), (0,)), ((), ())), preferred_element_type=F32)


def _mod_spec(per_row, tm, tiles_per_batch, ngrid):
    if per_row:
        if ngrid == 1:
            return pl.BlockSpec((tm, D_MODEL), lambda i: (i, 0))
        return pl.BlockSpec((tm, D_MODEL), lambda i, f: (i, 0))
    if ngrid == 1:
        return pl.BlockSpec((1, 1, D_MODEL), lambda i: (i // tiles_per_batch, 0, 0))
    return pl.BlockSpec((1, 1, D_MODEL), lambda i, f: (i // tiles_per_batch, 0, 0))


def _const_spec(shape):
    nd = len(shape)
    return pl.BlockSpec(shape, lambda i: (0,) * nd, pipeline_mode=pl.Buffered(1))


def _mod_kernel(c_ref, w_ref, b_ref, o_ref):
    c = c_ref[...]
    sc = (c * jax.nn.sigmoid(c)).astype(BF16)
    o_ref[0] = _dot(sc, w_ref[0].astype(BF16)) + b_ref[0]


def _ada_all(c_all, w_mod, b_mod):
    nrow = c_all.shape[0]
    tn = 1536
    nn = (6 * D_MODEL) // tn
    return pl.pallas_call(
        _mod_kernel,
        grid=(DEPTH, nn),
        in_specs=[
            pl.BlockSpec((nrow, D_MODEL), lambda l, n: (0, 0)),
            pl.BlockSpec((1, D_MODEL, tn), lambda l, n: (l, 0, n)),
            pl.BlockSpec((1, 1, tn), lambda l, n: (l, 0, n)),
        ],
        out_specs=pl.BlockSpec((1, nrow, tn), lambda l, n: (l, 0, n)),
        out_shape=jax.ShapeDtypeStruct((DEPTH, nrow, 6 * D_MODEL), F32),
        compiler_params=_cparams("arbitrary", "arbitrary"),
        name="ada_mod",
    )(c_all, w_mod, b_mod.reshape(DEPTH, 1, 6 * D_MODEL))


def _mlp_kernel(x_ref, sh_ref, sc_ref, gt_ref, g_ref, wu_ref, wd_ref, o_ref, h_scr, acc_scr):
    f = pl.program_id(1)

    @pl.when(f == 0)
    def _():
        h = _modulate(x_ref[...], g_ref[...], _rows(sh_ref), _rows(sc_ref))
        h_scr[...] = h.astype(BF16)
        acc_scr[...] = jnp.zeros_like(acc_scr)

    a = jnp.maximum(_dot(h_scr[...], wu_ref[...]), 0.0)
    acc_scr[...] += _dot((a * a).astype(BF16), wd_ref[...])

    @pl.when(f == pl.num_programs(1) - 1)
    def _():
        o_ref[...] = x_ref[...] + _rows(gt_ref) * acc_scr[...]


def _mlp_layer(x, shift, scale, gate, g, w_up, w_down, per_row, rows_per_batch):
    T = x.shape[0]
    tm = 1024
    tf = 1024
    tpb = rows_per_batch // tm if not per_row else 1
    ms = _mod_spec(per_row, tm, tpb, 2)
    return pl.pallas_call(
        _mlp_kernel,
        grid=(T // tm, D_FF // tf),
        in_specs=[
            pl.BlockSpec((tm, D_MODEL), lambda i, f: (i, 0)),
            ms, ms, ms,
            pl.BlockSpec((1, D_MODEL), lambda i, f: (0, 0)),
            pl.BlockSpec((D_MODEL, tf), lambda i, f: (0, f)),
            pl.BlockSpec((tf, D_MODEL), lambda i, f: (f, 0)),
        ],
        out_specs=pl.BlockSpec((tm, D_MODEL), lambda i, f: (i, 0)),
        out_shape=jax.ShapeDtypeStruct((T, D_MODEL), F32),
        scratch_shapes=[pltpu.VMEM((tm, D_MODEL), BF16), pltpu.VMEM((tm, D_MODEL), F32)],
        compiler_params=_cparams("parallel", "arbitrary"),
        name="mlp",
    )(x, shift, scale, gate, g, w_up, w_down)


def _gmlp_kernel(x_ref, sh_ref, sc_ref, gt_ref, g_ref, win_ref, vg_ref, wmix_ref, bmix_ref, wout_ref,
                 *out_refs, tm, emit_v):
    o_ref = out_refs[0]
    gs_scr = out_refs[-1]
    x = x_ref[...]
    h = _modulate(x, g_ref[...], _rows(sh_ref), _rows(sc_ref)).astype(BF16)
    z = _gelu_tanh(_dot(h, win_ref[...]))
    u = z[:, :D_A]
    v = z[:, D_A:]
    v = v * lax.rsqrt(jnp.mean(v * v, axis=-1, keepdims=True) + EPS) * vg_ref[...]
    if emit_v:
        out_refs[1][...] = v
    vb = v.astype(BF16)
    for c in range(tm // CHUNK_A):
        r0 = c * CHUNK_A
        for gi in range(N_GROUPS_A):
            c0 = gi * GROUP_A
            s = _dot(wmix_ref[gi], vb[r0:r0 + CHUNK_A, c0:c0 + GROUP_A])
            s = s + bmix_ref[:, c0:c0 + GROUP_A]
            gs_scr[r0:r0 + CHUNK_A, c0:c0 + GROUP_A] = (u[r0:r0 + CHUNK_A, c0:c0 + GROUP_A] * s).astype(BF16)
    y = _dot(gs_scr[...], wout_ref[...])
    o_ref[...] = x + _rows(gt_ref) * y


def _gmlp_layer(x, shift, scale, gate, g, w_in, v_gain, w_mix, b_mix, w_out, per_row, rows_per_batch,
                emit_v):
    T = x.shape[0]
    tm = 512
    tpb = rows_per_batch // tm if not per_row else 1
    ms = _mod_spec(per_row, tm, tpb, 1)
    out_shape = [jax.ShapeDtypeStruct((T, D_MODEL), F32)]
    out_specs = [pl.BlockSpec((tm, D_MODEL), lambda i: (i, 0))]
    if emit_v:
        out_shape.append(jax.ShapeDtypeStruct((T, D_A), F32))
        out_specs.append(pl.BlockSpec((tm, D_A), lambda i: (i, 0)))
    return pl.pallas_call(
        functools.partial(_gmlp_kernel, tm=tm, emit_v=emit_v),
        grid=(T // tm,),
        in_specs=[
            pl.BlockSpec((tm, D_MODEL), lambda i: (i, 0)),
            ms, ms, ms,
            _const_spec((1, D_MODEL)),
            _const_spec((D_MODEL, 2 * D_A)),
            _const_spec((1, D_A)),
            _const_spec((N_GROUPS_A, CHUNK_A, CHUNK_A)),
            _const_spec((CHUNK_A, D_A)),
            _const_spec((D_A, D_MODEL)),
        ],
        out_specs=out_specs,
        out_shape=out_shape,
        scratch_shapes=[pltpu.VMEM((tm, D_A), BF16)],
        compiler_params=_cparams("parallel"),
        name="gmlp_mixer",
    )(x, shift, scale, gate, g, w_in, v_gain, w_mix, b_mix, w_out)


def _conv_kernel(*refs, tm, per_row, tiles_per_batch):
    if per_row:
        (x_ref, sh_ref, sc_ref, gt_ref, g_ref, win_ref, cw_ref, wout_ref, p1_ref, p2_ref,
         o_ref, xin_ref) = refs
    else:
        (x_ref, sh_ref, sc_ref, gt_ref, g_ref, win_ref, cw_ref, wout_ref,
         o_ref, tail_ref, carry_scr) = refs
    x = x_ref[...]
    h = _modulate(x, g_ref[...], _rows(sh_ref), _rows(sc_ref)).astype(BF16)
    bcx = _dot(h, win_ref[...])
    gate_out = bcx[:, :D_C]
    xin = bcx[:, D_C:2 * D_C] * bcx[:, 2 * D_C:]
    row = lax.broadcasted_iota(jnp.int32, (tm, D_C), 0)
    roll1 = pltpu.roll(xin, 1, 0)
    roll2 = pltpu.roll(xin, 2, 0)
    if per_row:
        pos = row % SUBLANES
        prev1 = jnp.where(pos == 0, p1_ref[...], roll1)
        prev2 = jnp.where(pos < 2, p2_ref[...], roll2)
        xin_ref[...] = xin
    else:
        @pl.when(pl.program_id(0) % tiles_per_batch == 0)
        def _():
            carry_scr[...] = jnp.zeros_like(carry_scr)
        c6 = carry_scr[SUBLANES - 2:SUBLANES - 1, :]
        c7 = carry_scr[SUBLANES - 1:SUBLANES, :]
        prev1 = jnp.where(row == 0, c7, roll1)
        prev2 = jnp.where(row == 0, c6, jnp.where(row == 1, c7, roll2))
        tail = xin[tm - SUBLANES:, :]
        carry_scr[...] = tail
        tail_ref[0] = tail
    y = cw_ref[0:1, :] * prev2 + cw_ref[1:2, :] * prev1 + cw_ref[2:3, :] * xin
    out = _dot((gate_out * y).astype(BF16), wout_ref[...])
    o_ref[...] = x + _rows(gt_ref) * out


def _conv_layer(x, shift, scale, gate, g, w_in, conv_w, w_out, per_row, rows_per_batch, fills=None):
    T = x.shape[0]
    tm = 512
    tpb = rows_per_batch // tm if not per_row else 1
    ms = _mod_spec(per_row, tm, tpb, 1)
    in_specs = [
        pl.BlockSpec((tm, D_MODEL), lambda i: (i, 0)),
        ms, ms, ms,
        _const_spec((1, D_MODEL)),
        _const_spec((D_MODEL, 3 * D_C)),
        _const_spec((CONV_W, D_C)),
        _const_spec((D_C, D_MODEL)),
    ]
    args = [x, shift, scale, gate, g, w_in, conv_w, w_out]
    out_shape = [jax.ShapeDtypeStruct((T, D_MODEL), F32)]
    out_specs = [pl.BlockSpec((tm, D_MODEL), lambda i: (i, 0))]
    scratch = []
    if per_row:
        in_specs += [pl.BlockSpec((tm, D_C), lambda i: (i, 0))] * 2
        args += list(fills)
        out_shape.append(jax.ShapeDtypeStruct((T, D_C), F32))
        out_specs.append(pl.BlockSpec((tm, D_C), lambda i: (i, 0)))
        sem = "parallel"
    else:
        nb = T // rows_per_batch
        out_shape.append(jax.ShapeDtypeStruct((nb, SUBLANES, D_C), F32))
        out_specs.append(pl.BlockSpec((1, SUBLANES, D_C), lambda i: (i // tpb, 0, 0)))
        scratch.append(pltpu.VMEM((SUBLANES, D_C), F32))
        sem = "arbitrary"
    return pl.pallas_call(
        functools.partial(_conv_kernel, tm=tm, per_row=per_row, tiles_per_batch=tpb),
        grid=(T // tm,),
        in_specs=in_specs,
        out_specs=out_specs,
        out_shape=out_shape,
        scratch_shapes=scratch,
        compiler_params=_cparams(sem),
        name="conv_mixer",
    )(*args)


def _qkv_kernel(x_ref, sh_ref, sc_ref, g_ref, w_ref, qg_ref, kg_ref, q_ref, k_ref, v_ref, *extra,
                tm, emit_attn_inputs):
    h = _modulate(x_ref[...], g_ref[...], _rows(sh_ref), _rows(sc_ref)).astype(BF16)
    qkv = _dot(h, w_ref[...])
    hd = HEAD_DIM_B
    for hh in range(N_HEADS_B):
        cols = slice(hh * hd, (hh + 1) * hd)
        q = qkv[:, hh * hd:(hh + 1) * hd]
        k = qkv[:, D_MODEL + hh * hd:D_MODEL + (hh + 1) * hd]
        q_ref[:, cols] = q * lax.rsqrt(jnp.mean(q * q, axis=-1, keepdims=True) + EPS) * qg_ref[...]
        kn = k * lax.rsqrt(jnp.mean(k * k, axis=-1, keepdims=True) + EPS) * kg_ref[...]
        k_ref[:, cols] = kn
        if emit_attn_inputs:
            kb_ref, vb_ref, km_ref = extra
            kb_ref[:, cols] = kn.astype(BF16)
            for c in range(tm // BLOCK_B):
                km_ref[c, :, cols] = jnp.sum(kn[c * BLOCK_B:(c + 1) * BLOCK_B], axis=0, keepdims=True) * (
                    1.0 / BLOCK_B)
    v = qkv[:, 2 * D_MODEL:]
    v_ref[...] = v
    if emit_attn_inputs:
        extra[1][...] = v.astype(BF16)


def _qkv_layer(x, shift, scale, g, w_qkv, q_gain, k_gain, per_row, rows_per_batch, emit_attn_inputs):
    T = x.shape[0]
    tm = 512
    tpb = rows_per_batch // tm if not per_row else 1
    ms = _mod_spec(per_row, tm, tpb, 1)
    row_spec = pl.BlockSpec((tm, D_MODEL), lambda i: (i, 0))
    out_specs = [row_spec, row_spec, row_spec]
    out_shape = [jax.ShapeDtypeStruct((T, D_MODEL), F32)] * 3
    if emit_attn_inputs:
        out_specs += [row_spec, row_spec,
                      pl.BlockSpec((tm // BLOCK_B, 1, D_MODEL), lambda i: (i, 0, 0))]
        out_shape += [jax.ShapeDtypeStruct((T, D_MODEL), BF16)] * 2
        out_shape += [jax.ShapeDtypeStruct((T // BLOCK_B, 1, D_MODEL), F32)]
    return pl.pallas_call(
        functools.partial(_qkv_kernel, tm=tm, emit_attn_inputs=emit_attn_inputs),
        grid=(T // tm,),
        in_specs=[row_spec, ms, ms,
                  _const_spec((1, D_MODEL)),
                  _const_spec((D_MODEL, 3 * D_MODEL)),
                  _const_spec((1, HEAD_DIM_B)),
                  _const_spec((1, HEAD_DIM_B))],
        out_specs=out_specs,
        out_shape=out_shape,
        compiler_params=_cparams("parallel"),
        name="moba_qkv",
    )(x, shift, scale, g, w_qkv, q_gain, k_gain)


def _topk_mask_t(gate_t, cands, blk_of_row, elig):
    gm = jnp.where(elig, gate_t, MASK_VALUE)
    cnt = jnp.zeros(gate_t.shape, F32)
    for r, cb in cands:
        row = gm[r:r + 1, :]
        beats = jnp.where(row > gm, 1.0, jnp.where(row == gm, jnp.where(cb < blk_of_row, 1.0, 0.0), 0.0))
        cnt = cnt + beats
    return jnp.where(elig, jnp.where(cnt < TOPK_B, 1.0, 0.0), 0.0)


def _moba_prompt_kernel(q_ref, kb_ref, vb_ref, km_ref, bias_ref, o_ref, qaug_scr, *, nblk, nh):
    qb = pl.program_id(2)
    hd = HEAD_DIM_B
    W = BLOCK_B
    n_near = bias_ref.shape[2] // W
    dead_lane = LANES - 1
    lane = lax.broadcasted_iota(jnp.int32, (W, LANES), 1)
    blk = lax.broadcasted_iota(jnp.int32, (nblk, W), 0)
    r_i = lax.broadcasted_iota(jnp.int32, (W, W), 0)
    c_i = lax.broadcasted_iota(jnp.int32, (W, W), 1)
    causal = r_i >= c_i

    def onehot(b, valid):
        return jnp.where(lane == jnp.where(valid, b, dead_lane), 1.0, 0.0).astype(BF16)

    def rows_of(b):
        return pl.ds(pl.multiple_of(b * W, W), W)

    near_blk = [jnp.maximum(qb - d, 0) for d in range(n_near)]
    oh_near = jnp.concatenate([onehot(qb - d, qb - d >= 0) for d in range(n_near)], axis=0)
    ones_near = jnp.ones((n_near * W, hd), BF16)
    heads = [slice(h * hd, (h + 1) * hd) for h in range(nh)]
    gates = [_dot_nt(km_ref[:, 0, cols].astype(BF16), q_ref[:, cols].astype(BF16)) for cols in heads]
    scores = []
    for h, cols in enumerate(heads):
        sel_t = _topk_mask_t(gates[h], [(j, j) for j in range(nblk)], blk, blk < qb)
        neg_t = jnp.where(blk == qb, 0.0, jnp.where(sel_t > 0.5, 0.0, MASK_VALUE))
        neg = jnp.concatenate([neg_t, jnp.zeros((LANES - nblk - 1, W), F32),
                               jnp.full((1, W), MASK_VALUE, F32)], axis=0).T
        q_aug = jnp.concatenate([(q_ref[:, cols] * (LOG2E / math.sqrt(hd))).astype(BF16), neg.astype(BF16)],
                                axis=1)
        qaug_scr[h] = q_aug
        k_near = jnp.concatenate([kb_ref[rows_of(b), cols] for b in near_blk], axis=0)
        scores.append(_dot_nt(q_aug, jnp.concatenate([k_near, oh_near], axis=1)))
    carry0 = []
    for h, cols in enumerate(heads):
        s = scores[h] + bias_ref[h]
        s = jnp.concatenate([jnp.where(causal, s[:, :W], MASK_VALUE), s[:, W:]], axis=1)
        m0 = jnp.max(s, axis=-1, keepdims=True)
        p = jnp.exp2(s - m0)
        v_near = jnp.concatenate([vb_ref[rows_of(b), cols] for b in near_blk], axis=0)
        carry0 += [m0, _dot(p.astype(BF16), jnp.concatenate([v_near, ones_near], axis=1))]

    n_far = jnp.maximum(qb - (n_near - 1), 0)
    ones_pair = jnp.ones((2 * W, hd), BF16)

    def body(i, carry):
        b0 = 2 * i
        b1 = b0 + 1
        oh = jnp.concatenate([onehot(b0, True), onehot(b1, b1 < n_far)], axis=0)
        pair_scores = []
        for h, cols in enumerate(heads):
            k_pair = jnp.concatenate([kb_ref[rows_of(b0), cols], kb_ref[rows_of(b1), cols]], axis=0)
            pair_scores.append(_dot_nt(qaug_scr[h], jnp.concatenate([k_pair, oh], axis=1)))
        out = []
        for h, cols in enumerate(heads):
            m, acc = carry[2 * h], carry[2 * h + 1]
            s = pair_scores[h]
            v_pair = jnp.concatenate([vb_ref[rows_of(b0), cols], vb_ref[rows_of(b1), cols]], axis=0)
            m_new = jnp.maximum(m, jnp.max(s, axis=-1, keepdims=True))
            p = jnp.exp2(s - m_new)
            acc = jnp.exp2(m - m_new) * acc + _dot(p.astype(BF16), jnp.concatenate([v_pair, ones_pair], axis=1))
            out += [m_new, acc]
        return tuple(out)

    carry = lax.fori_loop(0, (n_far + 1) // 2, body, tuple(carry0))
    for h in range(nh):
        acc = carry[2 * h + 1]
        o_ref[:, h * hd:(h + 1) * hd] = (acc[:, :hd] / acc[:, hd:]).astype(o_ref.dtype)


def _moba_prompt_attention(q, kb, vb, kmean, bias_near, batch, seq):
    nq = seq // BLOCK_B
    hd = HEAD_DIM_B
    nh = 4
    one = pl.Buffered(1)
    return pl.pallas_call(
        functools.partial(_moba_prompt_kernel, nblk=nq, nh=nh),
        grid=(batch, N_HEADS_B // nh, nq),
        in_specs=[
            pl.BlockSpec((BLOCK_B, nh * hd), lambda b, g, i: (b * nq + i, g)),
            pl.BlockSpec((seq, nh * hd), lambda b, g, i: (b, g), pipeline_mode=one),
            pl.BlockSpec((seq, nh * hd), lambda b, g, i: (b, g), pipeline_mode=one),
            pl.BlockSpec((nq, 1, nh * hd), lambda b, g, i: (b, 0, g)),
            pl.BlockSpec((nh,) + bias_near.shape[1:], lambda b, g, i: (g, 0, 0), pipeline_mode=one),
        ],
        out_specs=pl.BlockSpec((BLOCK_B, nh * hd), lambda b, g, i: (b * nq + i, g)),
        out_shape=jax.ShapeDtypeStruct((batch * seq, D_MODEL), BF16),
        scratch_shapes=[pltpu.VMEM((nh, BLOCK_B, 2 * hd), BF16)],
        compiler_params=_cparams("parallel", "parallel", "arbitrary"),
        name="moba_prompt_attn",
    )(q, kb, vb, kmean, bias_near)


def _moba_sample_kernel(pt_ref, q_ref, kn_ref, vn_ref, *refs, pps, n_pages, t_new):
    kc_refs = refs[:pps]
    vc_refs = refs[pps:2 * pps]
    bias_ref, bias_own_ref = refs[2 * pps:2 * pps + 2]
    o_ref = refs[2 * pps + 2]
    qs_scr, qs2_scr, qg_scr, m_scr, l_scr, o_scr, ksum_scr = refs[2 * pps + 3:]
    del pt_ref
    s_id = pl.program_id(1)
    H, hd = N_HEADS_B, HEAD_DIM_B
    nrow = H * t_new
    ppb = BLOCK_B // PAGE_SIZE
    assert ppb == 2
    n_blk = n_pages // ppb
    page_rows = PAGE_SIZE * H

    @pl.when(s_id == 0)
    def _():
        q8 = q_ref[0]
        qall = jnp.concatenate([q8[:, h * hd:(h + 1) * hd] for h in range(H)], axis=0)
        qg_scr[...] = qall.astype(BF16)
        qs = (qall * (1.0 / math.sqrt(hd))).astype(BF16)
        qs_scr[...] = qs
        zero = jnp.zeros((nrow, hd), BF16)
        qs2_scr[...] = jnp.concatenate([jnp.concatenate([qs, zero], axis=1),
                                        jnp.concatenate([zero, qs], axis=1)], axis=0)

    own_head = (lax.broadcasted_iota(jnp.int32, (H, nrow), 0)
                == lax.broadcasted_iota(jnp.int32, (H, nrow), 1) // t_new)
    own_head2 = (lax.broadcasted_iota(jnp.int32, (H, ppb * nrow), 0)
                 == (lax.broadcasted_iota(jnp.int32, (H, ppb * nrow), 1) % nrow) // t_new)

    def block_scores(k_pages, bias):
        k2 = jnp.concatenate(k_pages, axis=1).astype(BF16)
        return _dot_nt(k2, qs2_scr[...]) + bias

    def block_attn(g, v_pages):
        v2 = jnp.concatenate(v_pages, axis=1).astype(BF16)
        g3 = g.reshape(PAGE_SIZE, H, ppb * nrow)
        m_t = jnp.max(g3, axis=0)
        m_t = jnp.maximum(m_t, pltpu.roll(m_t, nrow, 1))
        e3 = jnp.exp(g3 - jnp.where(own_head2, m_t, 0.0)[None])
        l_t = jnp.sum(e3, axis=0)
        l_t = l_t + pltpu.roll(l_t, nrow, 1)
        o2 = _dot_tn(e3.reshape(PAGE_SIZE * H, ppb * nrow).astype(BF16), v2)
        return m_t, l_t, o2[:nrow, :hd] + o2[nrow:, hd:]

    def partial_attn(k_rows, v_rows, bias, valid):
        n = k_rows.shape[0]
        g = _dot_nt(k_rows.astype(BF16), qs_scr[...]) + bias
        if valid is not None:
            g = jnp.where(valid, g, MASK_VALUE)
        g3 = g.reshape(n // H, H, nrow)
        m_t = jnp.max(g3, axis=0)
        e3 = jnp.exp(g3 - jnp.where(own_head, m_t, 0.0)[None])
        l_t = jnp.sum(e3, axis=0)
        o = _dot_tn(e3.reshape(n, nrow).astype(BF16), v_rows.astype(BF16))
        return m_t, l_t, o

    scores = []
    for i in range(pps // ppb):
        blk = s_id * (pps // ppb) + i
        k_pages = [kc_refs[i * ppb + u][0, 0].reshape(page_rows, hd) for u in range(ppb)]
        ksum_scr[blk] = jnp.sum((k_pages[0] + k_pages[1]).reshape(PAGE_SIZE, H, hd), axis=0)
        b0 = pl.multiple_of(blk * page_rows, page_rows)
        scores.append((blk, block_scores(k_pages, bias_ref[pl.ds(b0, page_rows), :])))
    for i, (blk, g) in enumerate(scores):
        v_pages = [vc_refs[i * ppb + u][0, 0].reshape(page_rows, hd) for u in range(ppb)]
        m_t, l_t, o = block_attn(g, v_pages)
        m_scr[blk] = m_t
        l_scr[blk] = l_t
        o_scr[blk] = o

    @pl.when(s_id == pl.num_programs(1) - 1)
    def _():
        def head_diag(t):
            return jnp.sum(jnp.where(own_head[None], t[:, :, :nrow], 0.0), axis=1)

        key_i = lax.broadcasted_iota(jnp.int32, (t_new * H, nrow), 0) // H
        qry_i = lax.broadcasted_iota(jnp.int32, (t_new * H, nrow), 1) % t_new
        m_ot, l_ot, o_o = partial_attn(kn_ref[0].reshape(t_new * H, hd), vn_ref[0].reshape(t_new * H, hd),
                                       bias_own_ref[...], key_i <= qry_i)
        m_o = head_diag(m_ot[None])
        l_o = head_diag(l_ot[None])

        kmean = (ksum_scr[...] * (1.0 / BLOCK_B)).reshape(n_blk * H, hd)
        gate = _dot_nt(kmean.astype(BF16), qg_scr[...])
        gate_t = head_diag(gate.reshape(n_blk, H, nrow))
        bi = lax.broadcasted_iota(jnp.int32, (n_blk, nrow), 0)
        sel = _topk_mask_t(gate_t, [(j, j) for j in range(n_blk)], bi, bi >= 0)

        m_all = head_diag(m_scr[...])
        l_all = head_diag(l_scr[...])
        m_fin = jnp.maximum(jnp.max(jnp.where(sel > 0.5, m_all, MASK_VALUE), axis=0, keepdims=True), m_o)
        w = jnp.where(sel > 0.5, jnp.exp(m_all - m_fin), 0.0)
        w_o = jnp.exp(m_o - m_fin)
        l_fin = jnp.sum(w * l_all, axis=0, keepdims=True) + w_o * l_o
        stack = jnp.concatenate([w, w_o, l_fin], axis=0)
        stack = jnp.concatenate([stack, jnp.zeros((stack.shape[0], LANES - nrow), F32)], axis=1)
        stack = jnp.concatenate([stack, jnp.zeros((LANES - stack.shape[0], LANES), F32)], axis=0)
        st = stack.T
        acc = st[:nrow, n_blk:n_blk + 1] * o_o
        for pp in range(n_blk):
            acc = acc + st[:nrow, pp:pp + 1] * o_scr[pp]
        out = acc / st[:nrow, n_blk + 1:n_blk + 2]
        for h in range(H):
            o_ref[0, :, h * hd:(h + 1) * hd] = out[h * t_new:(h + 1) * t_new, :]


def _moba_sample_attention(q, k_new, v_new, cache_k, cache_v, layer, page_table, bias_past, bias_own, t_new):
    nb, n_pages = page_table.shape
    pps = 8
    n_steps = n_pages // pps
    H, hd = N_HEADS_B, HEAD_DIM_B
    nrow = H * t_new
    n_blk = n_pages * PAGE_SIZE // BLOCK_B

    def page_spec(i):
        return pl.BlockSpec((1, 1, PAGE_SIZE, H, hd),
                            lambda b, s, pt: (layer, pt[b * n_pages + s * pps + i], 0, 0, 0))

    row_spec = pl.BlockSpec((1, t_new, H * hd), lambda b, s, pt: (b, 0, 0))
    own_spec = pl.BlockSpec((1, t_new, H, hd), lambda b, s, pt: (b, 0, 0, 0))
    one = pl.Buffered(1)
    grid_spec = pltpu.PrefetchScalarGridSpec(
        num_scalar_prefetch=1,
        grid=(nb, n_steps),
        in_specs=[row_spec, own_spec, own_spec]
                 + [page_spec(i) for i in range(pps)] + [page_spec(i) for i in range(pps)]
                 + [pl.BlockSpec(bias_past.shape, lambda b, s, pt: (0, 0), pipeline_mode=one),
                    pl.BlockSpec(bias_own.shape, lambda b, s, pt: (0, 0), pipeline_mode=one)],
        out_specs=row_spec,
        scratch_shapes=[
            pltpu.VMEM((nrow, hd), BF16),
            pltpu.VMEM((2 * nrow, 2 * hd), BF16),
            pltpu.VMEM((nrow, hd), BF16),
            pltpu.VMEM((n_blk, H, 2 * nrow), F32),
            pltpu.VMEM((n_blk, H, 2 * nrow), F32),
            pltpu.VMEM((n_blk, nrow, hd), F32),
            pltpu.VMEM((n_blk, H, hd), F32),
        ],
    )
    return pl.pallas_call(
        functools.partial(_moba_sample_kernel, pps=pps, n_pages=n_pages, t_new=t_new),
        grid_spec=grid_spec,
        out_shape=jax.ShapeDtypeStruct((nb, t_new, H * hd), F32),
        compiler_params=_cparams("parallel", "arbitrary"),
        name="moba_sample_attn",
    )(page_table.reshape(-1), q, k_new, v_new, *([cache_k] * pps), *([cache_v] * pps), bias_past, bias_own)


def _proj_kernel(x_ref, gt_ref, a_ref, w_ref, o_ref):
    o_ref[...] = x_ref[...] + _rows(gt_ref) * _dot(a_ref[...].astype(BF16), w_ref[...])


def _proj_residual(x, gate, a, w, per_row, rows_per_batch):
    T = x.shape[0]
    tm = 512
    tpb = rows_per_batch // tm if not per_row else 1
    row_spec = pl.BlockSpec((tm, D_MODEL), lambda i: (i, 0))
    return pl.pallas_call(
        _proj_kernel,
        grid=(T // tm,),
        in_specs=[row_spec, _mod_spec(per_row, tm, tpb, 1), row_spec,
                  _const_spec((D_MODEL, D_MODEL))],
        out_specs=row_spec,
        out_shape=jax.ShapeDtypeStruct((T, D_MODEL), F32),
        compiler_params=_cparams("parallel"),
        name="moba_out_proj",
    )(x, gate, a, w)


def _rel_bucket(dist):
    max_exact = N_BUCKETS // 2
    d = jnp.maximum(dist, 0)
    log_ratio = jnp.log(jnp.maximum(d, 1).astype(jnp.float32) / max_exact) / math.log(MAX_DIST / max_exact)
    large = jnp.minimum(max_exact + (log_ratio * (N_BUCKETS - max_exact)).astype(jnp.int32), N_BUCKETS - 1)
    return jnp.where(d < max_exact, d, large)


def _bias_tables(rel_bias, past_len, t_new):
    d_max = past_len + t_new
    tb = rel_bias[_rel_bucket(jnp.arange(d_max, dtype=jnp.int32))]
    n_near = MAX_DIST // BLOCK_B + 1
    r = np.arange(BLOCK_B)[:, None]
    col = np.arange(n_near * BLOCK_B)[None, :]
    dist = (col // BLOCK_B) * BLOCK_B + r - (col % BLOCK_B)
    onehot = (_rel_bucket(jnp.asarray(dist, jnp.int32))[..., None] == jnp.arange(N_BUCKETS)).astype(F32)
    near = jnp.einsum("rcb,bh->hrc", onehot, rel_bias - rel_bias[N_BUCKETS - 1], precision=HIGHEST) * LOG2E
    kpos = np.arange(past_len)[:, None]
    qr = np.arange(t_new)[None, :]
    idx_own = np.maximum(qr - np.arange(t_new)[:, None], 0)
    same_head = np.eye(N_HEADS_B, dtype=bool)[None, :, :, None]

    def per_head(idx):
        b = jnp.transpose(tb[idx], (0, 2, 1))[:, None]
        b = jnp.where(same_head, b, MASK_VALUE)
        return b.reshape(idx.shape[0] * N_HEADS_B, N_HEADS_B * t_new)

    ppb = BLOCK_B // PAGE_SIZE
    page_rows = PAGE_SIZE * N_HEADS_B
    past = per_head(past_len + qr - kpos).reshape(past_len // BLOCK_B, ppb, page_rows, N_HEADS_B * t_new)
    past = jnp.transpose(past, (0, 2, 1, 3)).reshape(past_len // BLOCK_B * page_rows, ppb * N_HEADS_B * t_new)
    return near, past, per_head(idx_own)


def kernel(x_prompt, x_sample, cache_k, cache_v, state_conv, page_table, c_prompt, c_sample, rel_bias, norm_mix, norm_mlp, w_mod, b_mod, w_up, w_down, a_w_in, a_v_gain, a_w_s, a_b_s, a_w_out, b_w_qkv, b_q_gain, b_k_gain, b_w_out, c_w_in, c_conv, c_w_out):
    B, S, D = x_prompt.shape
    DB, T, _ = x_sample.shape
    n_pages = page_table.shape[1]
    past_len = n_pages * PAGE_SIZE
    assert S % BLOCK_B == 0 and past_len % BLOCK_B == 0 and T == SUBLANES
    assert MAX_DIST % BLOCK_B == 0

    xp = x_prompt.reshape(B * S, D)
    xs = x_sample.reshape(DB * T, D)

    n_c = B + DB
    n_c_pad = -(-n_c // SUBLANES) * SUBLANES
    c_all = jnp.concatenate([c_prompt, c_sample, jnp.zeros((n_c_pad - n_c, D), F32)], axis=0)
    mods = _ada_all(c_all, w_mod, b_mod)

    bias_near, bias_past, bias_own = _bias_tables(rel_bias, past_len, T)

    ci = np.arange(CHUNK_A)
    tril_p = (ci[:, None] >= ci[None, :])
    tril_s = tril_p & ((ci[:, None] // T) == (ci[None, :] // T))

    k_p, v_p, k_s, v_s, conv_p, conv_s, chunkv_s = [], [], [], [], [], [], []
    for i in range(DEPTH):
        kind, j = i % N_MIXERS, i // N_MIXERS
        mp = [m.reshape(B, 1, D) for m in jnp.split(mods[i, :B], 6, axis=-1)]
        ms = [jnp.repeat(m, T, axis=0) for m in jnp.split(mods[i, B:B + DB], 6, axis=-1)]
        g_mix = norm_mix[i].reshape(1, D)
        g_mlp = norm_mlp[i].reshape(1, D)
        if kind == 0:
            w_in = a_w_in[j].astype(BF16)
            w_out = a_w_out[j].astype(BF16)
            vg = a_v_gain[j].reshape(1, D_A)
            wmix_p = jnp.where(tril_p, a_w_s[j], 0.0).astype(BF16)
            bmix_p = jnp.repeat(jnp.transpose(a_b_s[j]), GROUP_A, axis=1)
            ws_t = jnp.tile(a_w_s[j][:, :T, :T], (1, CHUNK_A // T, CHUNK_A // T))
            wmix_s = jnp.where(tril_s, ws_t, 0.0).astype(BF16)
            bmix_s = jnp.repeat(jnp.tile(jnp.transpose(a_b_s[j][:, :T]), (CHUNK_A // T, 1)), GROUP_A, axis=1)
            (xp,) = _gmlp_layer(xp, mp[0], mp[1], mp[2], g_mix, w_in, vg, wmix_p, bmix_p, w_out,
                                False, S, False)
            xs, v_new = _gmlp_layer(xs, ms[0], ms[1], ms[2], g_mix, w_in, vg, wmix_s, bmix_s, w_out,
                                    True, T, True)
            chunkv_s.append(v_new.reshape(DB, T, D_A))
        elif kind == 1:
            w_qkv = b_w_qkv[j].astype(BF16)
            w_out = b_w_out[j].astype(BF16)
            qg = b_q_gain[j].reshape(1, HEAD_DIM_B)
            kg = b_k_gain[j].reshape(1, HEAD_DIM_B)
            qp, kp, vp, kbp, vbp, kmean_p = _qkv_layer(xp, mp[0], mp[1], g_mix, w_qkv, qg, kg, False, S, True)
            qs, ks, vs = _qkv_layer(xs, ms[0], ms[1], g_mix, w_qkv, qg, kg, True, T, False)
            op = _moba_prompt_attention(qp, kbp, vbp, kmean_p, bias_near, B, S)
            os_ = _moba_sample_attention(
                qs.reshape(DB, T, D), ks.reshape(DB, T, N_HEADS_B, HEAD_DIM_B),
                vs.reshape(DB, T, N_HEADS_B, HEAD_DIM_B), cache_k, cache_v, j, page_table,
                bias_past, bias_own, T)
            xp = _proj_residual(xp, mp[2], op, w_out, False, S)
            xs = _proj_residual(xs, ms[2], os_.reshape(DB * T, D), w_out, True, T)
            k_p.append(kp.reshape(B, S, N_HEADS_B, HEAD_DIM_B))
            v_p.append(vp.reshape(B, S, N_HEADS_B, HEAD_DIM_B))
            k_s.append(ks.reshape(DB, T, N_HEADS_B, HEAD_DIM_B))
            v_s.append(vs.reshape(DB, T, N_HEADS_B, HEAD_DIM_B))
        else:
            w_in = c_w_in[j].astype(BF16)
            w_out = c_w_out[j].astype(BF16)
            st = state_conv[j]
            zrow = jnp.zeros((DB, T - 1, D_C), F32)
            p1 = jnp.concatenate([st[:, 1:2], zrow], axis=1).reshape(DB * T, D_C)
            p2 = jnp.concatenate([st, zrow[:, 1:]], axis=1).reshape(DB * T, D_C)
            xp, tail_p = _conv_layer(xp, mp[0], mp[1], mp[2], g_mix, w_in, c_conv[j], w_out, False, S)
            xs, xin_s = _conv_layer(xs, ms[0], ms[1], ms[2], g_mix, w_in, c_conv[j], w_out, True, T,
                                    fills=(p1, p2))
            conv_p.append(tail_p[:, SUBLANES - (CONV_W - 1):, :])
            conv_s.append(xin_s.reshape(DB, T, D_C)[:, T - (CONV_W - 1):, :])
        wu = w_up[i].astype(BF16)
        wd = w_down[i].astype(BF16)
        xp = _mlp_layer(xp, mp[3], mp[4], mp[5], g_mlp, wu, wd, False, S)
        xs = _mlp_layer(xs, ms[3], ms[4], ms[5], g_mlp, wu, wd, True, T)
    return (xp.reshape(B, S, D), xs.reshape(DB, T, D), jnp.stack(k_p), jnp.stack(v_p), jnp.stack(k_s),
            jnp.stack(v_s), jnp.stack(conv_p), jnp.stack(conv_s), jnp.stack(chunkv_s))
```

```python
import functools
import math

import jax
import jax.numpy as jnp
import numpy as np
from jax import lax
from jax.experimental import pallas as pl
from jax.experimental.pallas import tpu as pltpu

D_MODEL = 1024
DEPTH = 4
N_MIXERS = 3
D_A = 2 * D_MODEL
N_GROUPS_A = 8
GROUP_A = D_A // N_GROUPS_A
CHUNK_A = 128
HEAD_DIM_B = 128
N_HEADS_B = D_MODEL // HEAD_DIM_B
BLOCK_B = 256
TOPK_B = 3
N_BUCKETS = 32
MAX_DIST = 1024
D_C = D_MODEL
CONV_W = 3
D_FF = 4 * D_MODEL
EPS = 1e-6
MASK_VALUE = -1e30
LOG2E = math.log2(math.e)
PAGE_SIZE = 128

SUBLANES = 8
LANES = 128
VMEM_LIMIT_BYTES = 56 * 1024 * 1024

F32 = jnp.float32
BF16 = jnp.bfloat16
HIGHEST = lax.Precision.HIGHEST


def _cparams(*sem):
    return pltpu.CompilerParams(dimension_semantics=sem, vmem_limit_bytes=VMEM_LIMIT_BYTES)


def _rows(ref):
    v = ref[...]
    n, _, d = v.shape
    if n == 1:
        return v.reshape(1, d)
    return jnp.broadcast_to(v, (n, SUBLANES, d)).reshape(n * SUBLANES, d)


def _modulate(x, g, shift, scale):
    y = x * lax.rsqrt(jnp.mean(x * x, axis=-1, keepdims=True) + EPS)
    return (y * g) * (1.0 + scale) + shift


def _gelu_tanh(x):
    c = math.sqrt(2.0 / math.pi)
    return x * (0.5 * (1.0 + jnp.tanh(c * (x + 0.044715 * (x * x * x)))))


def _dot(a, b):
    return jnp.dot(a, b, preferred_element_type=F32)


def _dot_nt(a, b, **kw):
    return lax.dot_general(a, b, (((1,), (1,)), ((), ())), preferred_element_type=F32, **kw)


def _dot_tn(a, b):
    return lax.dot_general(a, b, (((0,), (0,)), ((), ())), preferred_element_type=F32)


def _mod_spec(per_row, tm, tiles_per_batch, ngrid):
    if per_row:
        if ngrid == 1:
            return pl.BlockSpec((tm // SUBLANES, 1, D_MODEL), lambda i: (i, 0, 0))
        return pl.BlockSpec((tm // SUBLANES, 1, D_MODEL), lambda i, f: (i, 0, 0))
    if ngrid == 1:
        return pl.BlockSpec((1, 1, D_MODEL), lambda i: (i // tiles_per_batch, 0, 0))
    return pl.BlockSpec((1, 1, D_MODEL), lambda i, f: (i // tiles_per_batch, 0, 0))


def _const_spec(shape):
    nd = len(shape)
    return pl.BlockSpec(shape, lambda i: (0,) * nd, pipeline_mode=pl.Buffered(1))


def _mod_kernel(c_ref, w_ref, b_ref, o_ref):
    c = c_ref[...]
    sc = (c * jax.nn.sigmoid(c)).astype(BF16)
    o_ref[0] = _dot(sc, w_ref[0].astype(BF16)) + b_ref[0]


def _ada_all(c_all, w_mod, b_mod):
    nrow = c_all.shape[0]
    tn = 1536
    nn = (6 * D_MODEL) // tn
    return pl.pallas_call(
        _mod_kernel,
        grid=(DEPTH, nn),
        in_specs=[
            pl.BlockSpec((nrow, D_MODEL), lambda l, n: (0, 0)),
            pl.BlockSpec((1, D_MODEL, tn), lambda l, n: (l, 0, n)),
            pl.BlockSpec((1, 1, tn), lambda l, n: (l, 0, n)),
        ],
        out_specs=pl.BlockSpec((1, nrow, tn), lambda l, n: (l, 0, n)),
        out_shape=jax.ShapeDtypeStruct((DEPTH, nrow, 6 * D_MODEL), F32),
        compiler_params=_cparams("arbitrary", "arbitrary"),
        name="ada_mod",
    )(c_all, w_mod, b_mod.reshape(DEPTH, 1, 6 * D_MODEL))


def _mlp_kernel(x_ref, sh_ref, sc_ref, gt_ref, g_ref, wu_ref, wd_ref, o_ref, h_scr, acc_scr):
    f = pl.program_id(1)

    @pl.when(f == 0)
    def _():
        h = _modulate(x_ref[...], g_ref[...], _rows(sh_ref), _rows(sc_ref))
        h_scr[...] = h.astype(BF16)
        acc_scr[...] = jnp.zeros_like(acc_scr)

    a = jnp.maximum(_dot(h_scr[...], wu_ref[...]), 0.0)
    acc_scr[...] += _dot((a * a).astype(BF16), wd_ref[...])

    @pl.when(f == pl.num_programs(1) - 1)
    def _():
        o_ref[...] = x_ref[...] + _rows(gt_ref) * acc_scr[...]


def _mlp_layer(x, shift, scale, gate, g, w_up, w_down, per_row, rows_per_batch):
    T = x.shape[0]
    tm = 1024
    tf = 1024
    tpb = rows_per_batch // tm if not per_row else 1
    ms = _mod_spec(per_row, tm, tpb, 2)
    return pl.pallas_call(
        _mlp_kernel,
        grid=(T // tm, D_FF // tf),
        in_specs=[
            pl.BlockSpec((tm, D_MODEL), lambda i, f: (i, 0)),
            ms, ms, ms,
            pl.BlockSpec((1, D_MODEL), lambda i, f: (0, 0)),
            pl.BlockSpec((D_MODEL, tf), lambda i, f: (0, f)),
            pl.BlockSpec((tf, D_MODEL), lambda i, f: (f, 0)),
        ],
        out_specs=pl.BlockSpec((tm, D_MODEL), lambda i, f: (i, 0)),
        out_shape=jax.ShapeDtypeStruct((T, D_MODEL), F32),
        scratch_shapes=[pltpu.VMEM((tm, D_MODEL), BF16), pltpu.VMEM((tm, D_MODEL), F32)],
        compiler_params=_cparams("parallel", "arbitrary"),
        name="mlp",
    )(x, shift, scale, gate, g, w_up, w_down)


def _gmlp_kernel(x_ref, sh_ref, sc_ref, gt_ref, g_ref, win_ref, vg_ref, wmix_ref, bmix_ref, wout_ref,
                 *out_refs, tm, emit_v):
    o_ref = out_refs[0]
    gs_scr = out_refs[-1]
    x = x_ref[...]
    h = _modulate(x, g_ref[...], _rows(sh_ref), _rows(sc_ref)).astype(BF16)
    z = _gelu_tanh(_dot(h, win_ref[...]))
    u = z[:, :D_A]
    v = z[:, D_A:]
    v = v * lax.rsqrt(jnp.mean(v * v, axis=-1, keepdims=True) + EPS) * vg_ref[...]
    if emit_v:
        out_refs[1][...] = v
    vb = v.astype(BF16)
    for c in range(tm // CHUNK_A):
        r0 = c * CHUNK_A
        for gi in range(N_GROUPS_A):
            c0 = gi * GROUP_A
            s = _dot(wmix_ref[gi], vb[r0:r0 + CHUNK_A, c0:c0 + GROUP_A])
            s = s + bmix_ref[:, c0:c0 + GROUP_A]
            gs_scr[r0:r0 + CHUNK_A, c0:c0 + GROUP_A] = (u[r0:r0 + CHUNK_A, c0:c0 + GROUP_A] * s).astype(BF16)
    y = _dot(gs_scr[...], wout_ref[...])
    o_ref[...] = x + _rows(gt_ref) * y


def _gmlp_layer(x, shift, scale, gate, g, w_in, v_gain, w_mix, b_mix, w_out, per_row, rows_per_batch,
                emit_v):
    T = x.shape[0]
    tm = 512
    tpb = rows_per_batch // tm if not per_row else 1
    ms = _mod_spec(per_row, tm, tpb, 1)
    out_shape = [jax.ShapeDtypeStruct((T, D_MODEL), F32)]
    out_specs = [pl.BlockSpec((tm, D_MODEL), lambda i: (i, 0))]
    if emit_v:
        out_shape.append(jax.ShapeDtypeStruct((T, D_A), F32))
        out_specs.append(pl.BlockSpec((tm, D_A), lambda i: (i, 0)))
    return pl.pallas_call(
        functools.partial(_gmlp_kernel, tm=tm, emit_v=emit_v),
        grid=(T // tm,),
        in_specs=[
            pl.BlockSpec((tm, D_MODEL), lambda i: (i, 0)),
            ms, ms, ms,
            _const_spec((1, D_MODEL)),
            _const_spec((D_MODEL, 2 * D_A)),
            _const_spec((1, D_A)),
            _const_spec((N_GROUPS_A, CHUNK_A, CHUNK_A)),
            _const_spec((CHUNK_A, D_A)),
            _const_spec((D_A, D_MODEL)),
        ],
        out_specs=out_specs,
        out_shape=out_shape,
        scratch_shapes=[pltpu.VMEM((tm, D_A), BF16)],
        compiler_params=_cparams("parallel"),
        name="gmlp_mixer",
    )(x, shift, scale, gate, g, w_in, v_gain, w_mix, b_mix, w_out)


def _conv_kernel(*refs, tm, per_row, tiles_per_batch):
    if per_row:
        (x_ref, sh_ref, sc_ref, gt_ref, g_ref, win_ref, cw_ref, wout_ref, p1_ref, p2_ref,
         o_ref, xin_ref) = refs
    else:
        (x_ref, sh_ref, sc_ref, gt_ref, g_ref, win_ref, cw_ref, wout_ref,
         o_ref, tail_ref, carry_scr) = refs
    x = x_ref[...]
    h = _modulate(x, g_ref[...], _rows(sh_ref), _rows(sc_ref)).astype(BF16)
    bcx = _dot(h, win_ref[...])
    gate_out = bcx[:, :D_C]
    xin = bcx[:, D_C:2 * D_C] * bcx[:, 2 * D_C:]
    row = lax.broadcasted_iota(jnp.int32, (tm, D_C), 0)
    roll1 = pltpu.roll(xin, 1, 0)
    roll2 = pltpu.roll(xin, 2, 0)
    if per_row:
        pos = row % SUBLANES
        prev1 = jnp.where(pos == 0, p1_ref[...], roll1)
        prev2 = jnp.where(pos < 2, p2_ref[...], roll2)
        xin_ref[...] = xin
    else:
        @pl.when(pl.program_id(0) % tiles_per_batch == 0)
        def _():
            carry_scr[...] = jnp.zeros_like(carry_scr)
        c6 = carry_scr[SUBLANES - 2:SUBLANES - 1, :]
        c7 = carry_scr[SUBLANES - 1:SUBLANES, :]
        prev1 = jnp.where(row == 0, c7, roll1)
        prev2 = jnp.where(row == 0, c6, jnp.where(row == 1, c7, roll2))
        tail = xin[tm - SUBLANES:, :]
        carry_scr[...] = tail
        tail_ref[0] = tail
    y = cw_ref[0:1, :] * prev2 + cw_ref[1:2, :] * prev1 + cw_ref[2:3, :] * xin
    out = _dot((gate_out * y).astype(BF16), wout_ref[...])
    o_ref[...] = x + _rows(gt_ref) * out


def _conv_layer(x, shift, scale, gate, g, w_in, conv_w, w_out, per_row, rows_per_batch, fills=None):
    T = x.shape[0]
    tm = 512
    tpb = rows_per_batch // tm if not per_row else 1
    ms = _mod_spec(per_row, tm, tpb, 1)
    in_specs = [
        pl.BlockSpec((tm, D_MODEL), lambda i: (i, 0)),
        ms, ms, ms,
        _const_spec((1, D_MODEL)),
        _const_spec((D_MODEL, 3 * D_C)),
        _const_spec((CONV_W, D_C)),
        _const_spec((D_C, D_MODEL)),
    ]
    args = [x, shift, scale, gate, g, w_in, conv_w, w_out]
    out_shape = [jax.ShapeDtypeStruct((T, D_MODEL), F32)]
    out_specs = [pl.BlockSpec((tm, D_MODEL), lambda i: (i, 0))]
    scratch = []
    if per_row:
        in_specs += [pl.BlockSpec((tm, D_C), lambda i: (i, 0))] * 2
        args += list(fills)
        out_shape.append(jax.ShapeDtypeStruct((T, D_C), F32))
        out_specs.append(pl.BlockSpec((tm, D_C), lambda i: (i, 0)))
        sem = "parallel"
    else:
        nb = T // rows_per_batch
        out_shape.append(jax.ShapeDtypeStruct((nb, SUBLANES, D_C), F32))
        out_specs.append(pl.BlockSpec((1, SUBLANES, D_C), lambda i: (i // tpb, 0, 0)))
        scratch.append(pltpu.VMEM((SUBLANES, D_C), F32))
        sem = "arbitrary"
    return pl.pallas_call(
        functools.partial(_conv_kernel, tm=tm, per_row=per_row, tiles_per_batch=tpb),
        grid=(T // tm,),
        in_specs=in_specs,
        out_specs=out_specs,
        out_shape=out_shape,
        scratch_shapes=scratch,
        compiler_params=_cparams(sem),
        name="conv_mixer",
    )(*args)


def _qkv_kernel(x_ref, sh_ref, sc_ref, g_ref, w_ref, qg_ref, kg_ref, q_ref, k_ref, v_ref, *extra,
                tm, emit_attn_inputs):
    h = _modulate(x_ref[...], g_ref[...], _rows(sh_ref), _rows(sc_ref)).astype(BF16)
    qkv = _dot(h, w_ref[...])
    hd = HEAD_DIM_B
    for hh in range(N_HEADS_B):
        cols = slice(hh * hd, (hh + 1) * hd)
        q = qkv[:, hh * hd:(hh + 1) * hd]
        k = qkv[:, D_MODEL + hh * hd:D_MODEL + (hh + 1) * hd]
        q_ref[:, cols] = q * lax.rsqrt(jnp.mean(q * q, axis=-1, keepdims=True) + EPS) * qg_ref[...]
        kn = k * lax.rsqrt(jnp.mean(k * k, axis=-1, keepdims=True) + EPS) * kg_ref[...]
        k_ref[:, cols] = kn
        if emit_attn_inputs:
            kb_ref, vb_ref, km_ref = extra
            kb_ref[:, cols] = kn.astype(BF16)
            for c in range(tm // BLOCK_B):
                km_ref[c, :, cols] = jnp.sum(kn[c * BLOCK_B:(c + 1) * BLOCK_B], axis=0, keepdims=True) * (
                    1.0 / BLOCK_B)
    v = qkv[:, 2 * D_MODEL:]
    v_ref[...] = v
    if emit_attn_inputs:
        extra[1][...] = v.astype(BF16)


def _qkv_layer(x, shift, scale, g, w_qkv, q_gain, k_gain, per_row, rows_per_batch, emit_attn_inputs):
    T = x.shape[0]
    tm = 512
    tpb = rows_per_batch // tm if not per_row else 1
    ms = _mod_spec(per_row, tm, tpb, 1)
    row_spec = pl.BlockSpec((tm, D_MODEL), lambda i: (i, 0))
    out_specs = [row_spec, row_spec, row_spec]
    out_shape = [jax.ShapeDtypeStruct((T, D_MODEL), F32)] * 3
    if emit_attn_inputs:
        out_specs += [row_spec, row_spec,
                      pl.BlockSpec((tm // BLOCK_B, 1, D_MODEL), lambda i: (i, 0, 0))]
        out_shape += [jax.ShapeDtypeStruct((T, D_MODEL), BF16)] * 2
        out_shape += [jax.ShapeDtypeStruct((T // BLOCK_B, 1, D_MODEL), F32)]
    return pl.pallas_call(
        functools.partial(_qkv_kernel, tm=tm, emit_attn_inputs=emit_attn_inputs),
        grid=(T // tm,),
        in_specs=[row_spec, ms, ms,
                  _const_spec((1, D_MODEL)),
                  _const_spec((D_MODEL, 3 * D_MODEL)),
                  _const_spec((1, HEAD_DIM_B)),
                  _const_spec((1, HEAD_DIM_B))],
        out_specs=out_specs,
        out_shape=out_shape,
        compiler_params=_cparams("parallel"),
        name="moba_qkv",
    )(x, shift, scale, g, w_qkv, q_gain, k_gain)


def _topk_mask_t(gate_t, cands, blk_of_row, elig):
    gm = jnp.where(elig, gate_t, MASK_VALUE)
    cnt = jnp.zeros(gate_t.shape, F32)
    for r, cb in cands:
        row = gm[r:r + 1, :]
        beats = jnp.where(row > gm, 1.0, jnp.where(row == gm, jnp.where(cb < blk_of_row, 1.0, 0.0), 0.0))
        cnt = cnt + beats
    return jnp.where(elig, jnp.where(cnt < TOPK_B, 1.0, 0.0), 0.0)


def _moba_prompt_kernel(q_ref, kb_ref, vb_ref, km_ref, bias_ref, o_ref, qaug_scr, *, nblk, nh):
    qb = pl.program_id(2)
    hd = HEAD_DIM_B
    W = BLOCK_B
    n_near = bias_ref.shape[2] // W
    dead_lane = LANES - 1
    lane = lax.broadcasted_iota(jnp.int32, (W, LANES), 1)
    blk = lax.broadcasted_iota(jnp.int32, (nblk, W), 0)
    r_i = lax.broadcasted_iota(jnp.int32, (W, W), 0)
    c_i = lax.broadcasted_iota(jnp.int32, (W, W), 1)
    causal = r_i >= c_i

    def onehot(b, valid):
        return jnp.where(lane == jnp.where(valid, b, dead_lane), 1.0, 0.0).astype(BF16)

    def rows_of(b):
        return pl.ds(pl.multiple_of(b * W, W), W)

    near_blk = [jnp.maximum(qb - d, 0) for d in range(n_near)]
    oh_near = jnp.concatenate([onehot(qb - d, qb - d >= 0) for d in range(n_near)], axis=0)
    ones_near = jnp.ones((n_near * W, hd), BF16)
    heads = [slice(h * hd, (h + 1) * hd) for h in range(nh)]
    gates = [_dot_nt(km_ref[:, 0, cols].astype(BF16), q_ref[:, cols].astype(BF16)) for cols in heads]
    scores = []
    for h, cols in enumerate(heads):
        sel_t = _topk_mask_t(gates[h], [(j, j) for j in range(nblk)], blk, blk < qb)
        neg_t = jnp.where(blk == qb, 0.0, jnp.where(sel_t > 0.5, 0.0, MASK_VALUE))
        neg = jnp.concatenate([neg_t, jnp.zeros((LANES - nblk - 1, W), F32),
                               jnp.full((1, W), MASK_VALUE, F32)], axis=0).T
        q_aug = jnp.concatenate([(q_ref[:, cols] * (LOG2E / math.sqrt(hd))).astype(BF16), neg.astype(BF16)],
                                axis=1)
        qaug_scr[h] = q_aug
        k_near = jnp.concatenate([kb_ref[rows_of(b), cols] for b in near_blk], axis=0)
        scores.append(_dot_nt(q_aug, jnp.concatenate([k_near, oh_near], axis=1)))
    carry0 = []
    for h, cols in enumerate(heads):
        s = scores[h] + bias_ref[h]
        s = jnp.concatenate([jnp.where(causal, s[:, :W], MASK_VALUE), s[:, W:]], axis=1)
        m0 = jnp.max(s, axis=-1, keepdims=True)
        p = jnp.exp2(s - m0)
        v_near = jnp.concatenate([vb_ref[rows_of(b), cols] for b in near_blk], axis=0)
        carry0 += [m0, _dot(p.astype(BF16), jnp.concatenate([v_near, ones_near], axis=1))]

    n_far = jnp.maximum(qb - (n_near - 1), 0)
    ones_pair = jnp.ones((2 * W, hd), BF16)

    def body(i, carry):
        b0 = 2 * i
        b1 = b0 + 1
        oh = jnp.concatenate([onehot(b0, True), onehot(b1, b1 < n_far)], axis=0)
        pair_scores = []
        for h, cols in enumerate(heads):
            k_pair = jnp.concatenate([kb_ref[rows_of(b0), cols], kb_ref[rows_of(b1), cols]], axis=0)
            pair_scores.append(_dot_nt(qaug_scr[h], jnp.concatenate([k_pair, oh], axis=1)))
        out = []
        for h, cols in enumerate(heads):
            m, acc = carry[2 * h], carry[2 * h + 1]
            s = pair_scores[h]
            v_pair = jnp.concatenate([vb_ref[rows_of(b0), cols], vb_ref[rows_of(b1), cols]], axis=0)
            m_new = jnp.maximum(m, jnp.max(s, axis=-1, keepdims=True))
            p = jnp.exp2(s - m_new)
            acc = jnp.exp2(m - m_new) * acc + _dot(p.astype(BF16), jnp.concatenate([v_pair, ones_pair], axis=1))
            out += [m_new, acc]
        return tuple(out)

    carry = lax.fori_loop(0, (n_far + 1) // 2, body, tuple(carry0))
    for h in range(nh):
        acc = carry[2 * h + 1]
        o_ref[:, h * hd:(h + 1) * hd] = (acc[:, :hd] / acc[:, hd:]).astype(o_ref.dtype)


def _moba_prompt_attention(q, kb, vb, kmean, bias_near, batch, seq):
    nq = seq // BLOCK_B
    hd = HEAD_DIM_B
    nh = 8
    one = pl.Buffered(1)
    return pl.pallas_call(
        functools.partial(_moba_prompt_kernel, nblk=nq, nh=nh),
        grid=(batch, N_HEADS_B // nh, nq),
        in_specs=[
            pl.BlockSpec((BLOCK_B, nh * hd), lambda b, g, i: (b * nq + i, g)),
            pl.BlockSpec((seq, nh * hd), lambda b, g, i: (b, g), pipeline_mode=one),
            pl.BlockSpec((seq, nh * hd), lambda b, g, i: (b, g), pipeline_mode=one),
            pl.BlockSpec((nq, 1, nh * hd), lambda b, g, i: (b, 0, g)),
            pl.BlockSpec((nh,) + bias_near.shape[1:], lambda b, g, i: (g, 0, 0), pipeline_mode=one),
        ],
        out_specs=pl.BlockSpec((BLOCK_B, nh * hd), lambda b, g, i: (b * nq + i, g)),
        out_shape=jax.ShapeDtypeStruct((batch * seq, D_MODEL), BF16),
        scratch_shapes=[pltpu.VMEM((nh, BLOCK_B, 2 * hd), BF16)],
        compiler_params=_cparams("parallel", "parallel", "arbitrary"),
        name="moba_prompt_attn",
    )(q, kb, vb, kmean, bias_near)


def _moba_sample_kernel(pt_ref, q_ref, kn_ref, vn_ref, *refs, pps, n_pages, t_new):
    kc_refs = refs[:pps]
    vc_refs = refs[pps:2 * pps]
    bias_ref, bias_own_ref = refs[2 * pps:2 * pps + 2]
    o_ref = refs[2 * pps + 2]
    qs_scr, qs2_scr, qg_scr, m_scr, l_scr, o_scr, ksum_scr = refs[2 * pps + 3:]
    del pt_ref
    s_id = pl.program_id(1)
    H, hd = N_HEADS_B, HEAD_DIM_B
    nrow = H * t_new
    ppb = BLOCK_B // PAGE_SIZE
    assert ppb == 2
    n_blk = n_pages // ppb
    page_rows = PAGE_SIZE * H

    @pl.when(s_id == 0)
    def _():
        q8 = q_ref[0]
        qall = jnp.concatenate([q8[:, h * hd:(h + 1) * hd] for h in range(H)], axis=0)
        qg_scr[...] = qall.astype(BF16)
        qs = (qall * (1.0 / math.sqrt(hd))).astype(BF16)
        qs_scr[...] = qs
        zero = jnp.zeros((nrow, hd), BF16)
        qs2_scr[...] = jnp.concatenate([jnp.concatenate([qs, zero], axis=1),
                                        jnp.concatenate([zero, qs], axis=1)], axis=0)

    own_head = (lax.broadcasted_iota(jnp.int32, (H, nrow), 0)
                == lax.broadcasted_iota(jnp.int32, (H, nrow), 1) // t_new)
    own_head2 = (lax.broadcasted_iota(jnp.int32, (H, ppb * nrow), 0)
                 == (lax.broadcasted_iota(jnp.int32, (H, ppb * nrow), 1) % nrow) // t_new)

    def block_scores(k_pages, bias):
        k2 = jnp.concatenate(k_pages, axis=1).astype(BF16)
        return _dot_nt(k2, qs2_scr[...]) + bias

    def block_attn(g, v_pages):
        v2 = jnp.concatenate(v_pages, axis=1).astype(BF16)
        g3 = g.reshape(PAGE_SIZE, H, ppb * nrow)
        m_t = jnp.max(g3, axis=0)
        m_t = jnp.maximum(m_t, pltpu.roll(m_t, nrow, 1))
        e3 = jnp.exp(g3 - jnp.where(own_head2, m_t, 0.0)[None])
        l_t = jnp.sum(e3, axis=0)
        l_t = l_t + pltpu.roll(l_t, nrow, 1)
        o2 = _dot_tn(e3.reshape(PAGE_SIZE * H, ppb * nrow).astype(BF16), v2)
        return m_t, l_t, o2[:nrow, :hd] + o2[nrow:, hd:]

    def partial_attn(k_rows, v_rows, bias, valid):
        n = k_rows.shape[0]
        g = _dot_nt(k_rows.astype(BF16), qs_scr[...]) + bias
        if valid is not None:
            g = jnp.where(valid, g, MASK_VALUE)
        g3 = g.reshape(n // H, H, nrow)
        m_t = jnp.max(g3, axis=0)
        e3 = jnp.exp(g3 - jnp.where(own_head, m_t, 0.0)[None])
        l_t = jnp.sum(e3, axis=0)
        o = _dot_tn(e3.reshape(n, nrow).astype(BF16), v_rows.astype(BF16))
        return m_t, l_t, o

    scores = []
    for i in range(pps // ppb):
        blk = s_id * (pps // ppb) + i
        k_pages = [kc_refs[i * ppb + u][0, 0].reshape(page_rows, hd) for u in range(ppb)]
        ksum_scr[blk] = jnp.sum((k_pages[0] + k_pages[1]).reshape(PAGE_SIZE, H, hd), axis=0)
        b0 = pl.multiple_of(blk * page_rows, page_rows)
        scores.append((blk, block_scores(k_pages, bias_ref[pl.ds(b0, page_rows), :])))
    for i, (blk, g) in enumerate(scores):
        v_pages = [vc_refs[i * ppb + u][0, 0].reshape(page_rows, hd) for u in range(ppb)]
        m_t, l_t, o = block_attn(g, v_pages)
        m_scr[blk] = m_t
        l_scr[blk] = l_t
        o_scr[blk] = o

    @pl.when(s_id == pl.num_programs(1) - 1)
    def _():
        def head_diag(t):
            return jnp.sum(jnp.where(own_head[None], t[:, :, :nrow], 0.0), axis=1)

        key_i = lax.broadcasted_iota(jnp.int32, (t_new * H, nrow), 0) // H
        qry_i = lax.broadcasted_iota(jnp.int32, (t_new * H, nrow), 1) % t_new
        m_ot, l_ot, o_o = partial_attn(kn_ref[0].reshape(t_new * H, hd), vn_ref[0].reshape(t_new * H, hd),
                                       bias_own_ref[...], key_i <= qry_i)
        m_o = head_diag(m_ot[None])
        l_o = head_diag(l_ot[None])

        kmean = (ksum_scr[...] * (1.0 / BLOCK_B)).reshape(n_blk * H, hd)
        gate = _dot_nt(kmean.astype(BF16), qg_scr[...])
        gate_t = head_diag(gate.reshape(n_blk, H, nrow))
        bi = lax.broadcasted_iota(jnp.int32, (n_blk, nrow), 0)
        sel = _topk_mask_t(gate_t, [(j, j) for j in range(n_blk)], bi, bi >= 0)

        m_all = head_diag(m_scr[...])
        l_all = head_diag(l_scr[...])
        m_fin = jnp.maximum(jnp.max(jnp.where(sel > 0.5, m_all, MASK_VALUE), axis=0, keepdims=True), m_o)
        w = jnp.where(sel > 0.5, jnp.exp(m_all - m_fin), 0.0)
        w_o = jnp.exp(m_o - m_fin)
        l_fin = jnp.sum(w * l_all, axis=0, keepdims=True) + w_o * l_o
        stack = jnp.concatenate([w, w_o, l_fin], axis=0)
        stack = jnp.concatenate([stack, jnp.zeros((stack.shape[0], LANES - nrow), F32)], axis=1)
        stack = jnp.concatenate([stack, jnp.zeros((LANES - stack.shape[0], LANES), F32)], axis=0)
        st = stack.T
        acc = st[:nrow, n_blk:n_blk + 1] * o_o
        for pp in range(n_blk):
            acc = acc + st[:nrow, pp:pp + 1] * o_scr[pp]
        out = acc / st[:nrow, n_blk + 1:n_blk + 2]
        for h in range(H):
            o_ref[0, :, h * hd:(h + 1) * hd] = out[h * t_new:(h + 1) * t_new, :]


def _moba_sample_attention(q, k_new, v_new, cache_k, cache_v, layer, page_table, bias_past, bias_own, t_new):
    nb, n_pages = page_table.shape
    pps = 8
    n_steps = n_pages // pps
    H, hd = N_HEADS_B, HEAD_DIM_B
    nrow = H * t_new
    n_blk = n_pages * PAGE_SIZE // BLOCK_B

    def page_spec(i):
        return pl.BlockSpec((1, 1, PAGE_SIZE, H, hd),
                            lambda b, s, pt: (layer, pt[b * n_pages + s * pps + i], 0, 0, 0))

    row_spec = pl.BlockSpec((1, t_new, H * hd), lambda b, s, pt: (b, 0, 0))
    own_spec = pl.BlockSpec((1, t_new, H, hd), lambda b, s, pt: (b, 0, 0, 0))
    one = pl.Buffered(1)
    grid_spec = pltpu.PrefetchScalarGridSpec(
        num_scalar_prefetch=1,
        grid=(nb, n_steps),
        in_specs=[row_spec, own_spec, own_spec]
                 + [page_spec(i) for i in range(pps)] + [page_spec(i) for i in range(pps)]
                 + [pl.BlockSpec(bias_past.shape, lambda b, s, pt: (0, 0), pipeline_mode=one),
                    pl.BlockSpec(bias_own.shape, lambda b, s, pt: (0, 0), pipeline_mode=one)],
        out_specs=row_spec,
        scratch_shapes=[
            pltpu.VMEM((nrow, hd), BF16),
            pltpu.VMEM((2 * nrow, 2 * hd), BF16),
            pltpu.VMEM((nrow, hd), BF16),
            pltpu.VMEM((n_blk, H, 2 * nrow), F32),
            pltpu.VMEM((n_blk, H, 2 * nrow), F32),
            pltpu.VMEM((n_blk, nrow, hd), F32),
            pltpu.VMEM((n_blk, H, hd), F32),
        ],
    )
    return pl.pallas_call(
        functools.partial(_moba_sample_kernel, pps=pps, n_pages=n_pages, t_new=t_new),
        grid_spec=grid_spec,
        out_shape=jax.ShapeDtypeStruct((nb, t_new, H * hd), F32),
        compiler_params=_cparams("parallel", "arbitrary"),
        name="moba_sample_attn",
    )(page_table.reshape(-1), q, k_new, v_new, *([cache_k] * pps), *([cache_v] * pps), bias_past, bias_own)


def _proj_kernel(x_ref, gt_ref, a_ref, w_ref, o_ref):
    o_ref[...] = x_ref[...] + _rows(gt_ref) * _dot(a_ref[...].astype(BF16), w_ref[...])


def _proj_residual(x, gate, a, w, per_row, rows_per_batch):
    T = x.shape[0]
    tm = 512
    tpb = rows_per_batch // tm if not per_row else 1
    row_spec = pl.BlockSpec((tm, D_MODEL), lambda i: (i, 0))
    return pl.pallas_call(
        _proj_kernel,
        grid=(T // tm,),
        in_specs=[row_spec, _mod_spec(per_row, tm, tpb, 1), row_spec,
                  _const_spec((D_MODEL, D_MODEL))],
        out_specs=row_spec,
        out_shape=jax.ShapeDtypeStruct((T, D_MODEL), F32),
        compiler_params=_cparams("parallel"),
        name="moba_out_proj",
    )(x, gate, a, w)


def _rel_bucket(dist):
    max_exact = N_BUCKETS // 2
    d = jnp.maximum(dist, 0)
    log_ratio = jnp.log(jnp.maximum(d, 1).astype(jnp.float32) / max_exact) / math.log(MAX_DIST / max_exact)
    large = jnp.minimum(max_exact + (log_ratio * (N_BUCKETS - max_exact)).astype(jnp.int32), N_BUCKETS - 1)
    return jnp.where(d < max_exact, d, large)


def _bias_lookup(dist, rel_bias, out_spec):
    onehot = (_rel_bucket(jnp.asarray(dist, jnp.int32))[..., None] == jnp.arange(N_BUCKETS)).astype(F32)
    return jnp.einsum(out_spec, onehot, rel_bias, precision=HIGHEST)


def _bias_tables(rel_bias, past_len, t_new):
    H, W = N_HEADS_B, BLOCK_B
    n_near = MAX_DIST // W + 1
    r = np.arange(W)[:, None]
    col = np.arange(n_near * W)[None, :]
    dist = (col // W) * W + r - (col % W)
    near = _bias_lookup(dist, rel_bias - rel_bias[N_BUCKETS - 1], "rcb,bh->hrc") * LOG2E
    ppb = W // PAGE_SIZE
    n_blk = past_len // W
    kpos = (np.arange(n_blk)[:, None, None] * ppb + np.arange(ppb)[None, None, :]) * PAGE_SIZE \
        + np.arange(PAGE_SIZE)[None, :, None]
    qr = np.arange(t_new)
    same_head = np.eye(H, dtype=bool)
    past = _bias_lookup(past_len + qr - kpos[..., None], rel_bias, "jturb,bh->jtuhr")
    past = jnp.where(same_head[None, None, :, None, :, None], past[:, :, None], MASK_VALUE)
    past = past.reshape(n_blk * PAGE_SIZE * H, ppb * H * t_new)
    own = _bias_lookup(np.maximum(qr[None, :] - qr[:, None], 0), rel_bias, "trb,bh->thr")
    own = jnp.where(same_head[None, :, :, None], own[:, None], MASK_VALUE).reshape(t_new * H, H * t_new)
    return near, past, own


def kernel(x_prompt, x_sample, cache_k, cache_v, state_conv, page_table, c_prompt, c_sample, rel_bias, norm_mix, norm_mlp, w_mod, b_mod, w_up, w_down, a_w_in, a_v_gain, a_w_s, a_b_s, a_w_out, b_w_qkv, b_q_gain, b_k_gain, b_w_out, c_w_in, c_conv, c_w_out):
    B, S, D = x_prompt.shape
    DB, T, _ = x_sample.shape
    n_pages = page_table.shape[1]
    past_len = n_pages * PAGE_SIZE
    assert S % BLOCK_B == 0 and past_len % BLOCK_B == 0 and T == SUBLANES
    assert MAX_DIST % BLOCK_B == 0

    xp = x_prompt.reshape(B * S, D)
    xs = x_sample.reshape(DB * T, D)

    n_c = B + DB
    n_c_pad = -(-n_c // SUBLANES) * SUBLANES
    c_all = jnp.concatenate([c_prompt, c_sample, jnp.zeros((n_c_pad - n_c, D), F32)], axis=0)
    mods = _ada_all(c_all, w_mod, b_mod)

    bias_near, bias_past, bias_own = _bias_tables(rel_bias, past_len, T)

    ci = np.arange(CHUNK_A)
    tril_p = (ci[:, None] >= ci[None, :])
    tril_s = tril_p & ((ci[:, None] // T) == (ci[None, :] // T))

    k_p, v_p, k_s, v_s, conv_p, conv_s, chunkv_s = [], [], [], [], [], [], []
    for i in range(DEPTH):
        kind, j = i % N_MIXERS, i // N_MIXERS
        mp = [m.reshape(B, 1, D) for m in jnp.split(mods[i, :B], 6, axis=-1)]
        ms = [m.reshape(DB, 1, D) for m in jnp.split(mods[i, B:B + DB], 6, axis=-1)]
        g_mix = norm_mix[i].reshape(1, D)
        g_mlp = norm_mlp[i].reshape(1, D)
        if kind == 0:
            w_in = a_w_in[j].astype(BF16)
            w_out = a_w_out[j].astype(BF16)
            vg = a_v_gain[j].reshape(1, D_A)
            wmix_p = jnp.where(tril_p, a_w_s[j], 0.0).astype(BF16)
            bmix_p = jnp.repeat(jnp.transpose(a_b_s[j]), GROUP_A, axis=1)
            ws_t = jnp.tile(a_w_s[j][:, :T, :T], (1, CHUNK_A // T, CHUNK_A // T))
            wmix_s = jnp.where(tril_s, ws_t, 0.0).astype(BF16)
            bmix_s = jnp.repeat(jnp.tile(jnp.transpose(a_b_s[j][:, :T]), (CHUNK_A // T, 1)), GROUP_A, axis=1)
            (xp,) = _gmlp_layer(xp, mp[0], mp[1], mp[2], g_mix, w_in, vg, wmix_p, bmix_p, w_out,
                                False, S, False)
            xs, v_new = _gmlp_layer(xs, ms[0], ms[1], ms[2], g_mix, w_in, vg, wmix_s, bmix_s, w_out,
                                    True, T, True)
            chunkv_s.append(v_new.reshape(DB, T, D_A))
        elif kind == 1:
            w_qkv = b_w_qkv[j].astype(BF16)
            w_out = b_w_out[j].astype(BF16)
            qg = b_q_gain[j].reshape(1, HEAD_DIM_B)
            kg = b_k_gain[j].reshape(1, HEAD_DIM_B)
            qp, kp, vp, kbp, vbp, kmean_p = _qkv_layer(xp, mp[0], mp[1], g_mix, w_qkv, qg, kg, False, S, True)
            qs, ks, vs = _qkv_layer(xs, ms[0], ms[1], g_mix, w_qkv, qg, kg, True, T, False)
            op = _moba_prompt_attention(qp, kbp, vbp, kmean_p, bias_near, B, S)
            os_ = _moba_sample_attention(
                qs.reshape(DB, T, D), ks.reshape(DB, T, N_HEADS_B, HEAD_DIM_B),
                vs.reshape(DB, T, N_HEADS_B, HEAD_DIM_B), cache_k, cache_v, j, page_table,
                bias_past, bias_own, T)
            xp = _proj_residual(xp, mp[2], op, w_out, False, S)
            xs = _proj_residual(xs, ms[2], os_.reshape(DB * T, D), w_out, True, T)
            k_p.append(kp.reshape(B, S, N_HEADS_B, HEAD_DIM_B))
            v_p.append(vp.reshape(B, S, N_HEADS_B, HEAD_DIM_B))
            k_s.append(ks.reshape(DB, T, N_HEADS_B, HEAD_DIM_B))
            v_s.append(vs.reshape(DB, T, N_HEADS_B, HEAD_DIM_B))
        else:
            w_in = c_w_in[j].astype(BF16)
            w_out = c_w_out[j].astype(BF16)
            st = state_conv[j]
            zrow = jnp.zeros((DB, T - 1, D_C), F32)
            p1 = jnp.concatenate([st[:, 1:2], zrow], axis=1).reshape(DB * T, D_C)
            p2 = jnp.concatenate([st, zrow[:, 1:]], axis=1).reshape(DB * T, D_C)
            xp, tail_p = _conv_layer(xp, mp[0], mp[1], mp[2], g_mix, w_in, c_conv[j], w_out, False, S)
            xs, xin_s = _conv_layer(xs, ms[0], ms[1], ms[2], g_mix, w_in, c_conv[j], w_out, True, T,
                                    fills=(p1, p2))
            conv_p.append(tail_p[:, SUBLANES - (CONV_W - 1):, :])
            conv_s.append(xin_s.reshape(DB, T, D_C)[:, T - (CONV_W - 1):, :])
        wu = w_up[i].astype(BF16)
        wd = w_down[i].astype(BF16)
        xp = _mlp_layer(xp, mp[3], mp[4], mp[5], g_mlp, wu, wd, False, S)
        xs = _mlp_layer(xs, ms[3], ms[4], ms[5], g_mlp, wu, wd, True, T)
    return (xp.reshape(B, S, D), xs.reshape(DB, T, D), jnp.stack(k_p), jnp.stack(v_p), jnp.stack(k_s),
            jnp.stack(v_s), jnp.stack(conv_p), jnp.stack(conv_s), jnp.stack(chunkv_s))
```

```python
import functools
import math

import jax
import jax.numpy as jnp
import numpy as np
from jax import lax
from jax.experimental import pallas as pl
from jax.experimental.pallas import tpu as pltpu

D_MODEL = 1024
DEPTH = 4
N_MIXERS = 3
D_A = 2 * D_MODEL
N_GROUPS_A = 8
GROUP_A = D_A // N_GROUPS_A
CHUNK_A = 128
HEAD_DIM_B = 128
N_HEADS_B = D_MODEL // HEAD_DIM_B
BLOCK_B = 256
TOPK_B = 3
N_BUCKETS = 32
MAX_DIST = 1024
D_C = D_MODEL
CONV_W = 3
D_FF = 4 * D_MODEL
EPS = 1e-6
MASK_VALUE = -1e30
LOG2E = math.log2(math.e)
PAGE_SIZE = 128

SUBLANES = 8
LANES = 128
VMEM_LIMIT_BYTES = 56 * 1024 * 1024

F32 = jnp.float32
BF16 = jnp.bfloat16
HIGHEST = lax.Precision.HIGHEST


def _cparams(*sem):
    return pltpu.CompilerParams(dimension_semantics=sem, vmem_limit_bytes=VMEM_LIMIT_BYTES)


def _rows(ref):
    v = ref[...]
    n, _, d = v.shape
    if n == 1:
        return v.reshape(1, d)
    return jnp.broadcast_to(v, (n, SUBLANES, d)).reshape(n * SUBLANES, d)


def _modulate(x, g, shift, scale):
    y = x * lax.rsqrt(jnp.mean(x * x, axis=-1, keepdims=True) + EPS)
    return (y * g) * (1.0 + scale) + shift


def _gelu_tanh(x):
    c = math.sqrt(2.0 / math.pi)
    return x * (0.5 * (1.0 + jnp.tanh(c * (x + 0.044715 * (x * x * x)))))


def _dot(a, b):
    return jnp.dot(a, b, preferred_element_type=F32)


def _dot_nt(a, b, **kw):
    return lax.dot_general(a, b, (((1,), (1,)), ((), ())), preferred_element_type=F32, **kw)


def _dot_tn(a, b):
    return lax.dot_general(a, b, (((0,), (0,)), ((), ())), preferred_element_type=F32)


def _mod_spec(per_row, tm, tiles_per_batch, ngrid):
    if per_row:
        if ngrid == 1:
            return pl.BlockSpec((tm // SUBLANES, 1, D_MODEL), lambda i: (i, 0, 0))
        return pl.BlockSpec((tm // SUBLANES, 1, D_MODEL), lambda i, f: (i, 0, 0))
    if ngrid == 1:
        return pl.BlockSpec((1, 1, D_MODEL), lambda i: (i // tiles_per_batch, 0, 0))
    return pl.BlockSpec((1, 1, D_MODEL), lambda i, f: (i // tiles_per_batch, 0, 0))


def _const_spec(shape):
    nd = len(shape)
    return pl.BlockSpec(shape, lambda i: (0,) * nd, pipeline_mode=pl.Buffered(1))


def _mod_kernel(c_ref, w_ref, b_ref, o_ref):
    c = c_ref[...]
    sc = (c * jax.nn.sigmoid(c)).astype(BF16)
    o_ref[0] = _dot(sc, w_ref[0].astype(BF16)) + b_ref[0]


def _ada_all(c_all, w_mod, b_mod):
    nrow = c_all.shape[0]
    tn = 1536
    nn = (6 * D_MODEL) // tn
    return pl.pallas_call(
        _mod_kernel,
        grid=(DEPTH, nn),
        in_specs=[
            pl.BlockSpec((nrow, D_MODEL), lambda l, n: (0, 0)),
            pl.BlockSpec((1, D_MODEL, tn), lambda l, n: (l, 0, n)),
            pl.BlockSpec((1, 1, tn), lambda l, n: (l, 0, n)),
        ],
        out_specs=pl.BlockSpec((1, nrow, tn), lambda l, n: (l, 0, n)),
        out_shape=jax.ShapeDtypeStruct((DEPTH, nrow, 6 * D_MODEL), F32),
        compiler_params=_cparams("arbitrary", "arbitrary"),
        name="ada_mod",
    )(c_all, w_mod, b_mod.reshape(DEPTH, 1, 6 * D_MODEL))


def _mlp_kernel(x_ref, sh_ref, sc_ref, gt_ref, g_ref, wu_ref, wd_ref, o_ref, h_scr, acc_scr):
    f = pl.program_id(1)

    @pl.when(f == 0)
    def _():
        h = _modulate(x_ref[...], g_ref[...], _rows(sh_ref), _rows(sc_ref))
        h_scr[...] = h.astype(BF16)
        acc_scr[...] = jnp.zeros_like(acc_scr)

    a = jnp.maximum(_dot(h_scr[...], wu_ref[...]), 0.0)
    acc_scr[...] += _dot((a * a).astype(BF16), wd_ref[...])

    @pl.when(f == pl.num_programs(1) - 1)
    def _():
        o_ref[...] = x_ref[...] + _rows(gt_ref) * acc_scr[...]


def _mlp_layer(x, shift, scale, gate, g, w_up, w_down, per_row, rows_per_batch):
    T = x.shape[0]
    tm = 1024
    tf = 1024
    tpb = rows_per_batch // tm if not per_row else 1
    ms = _mod_spec(per_row, tm, tpb, 2)
    return pl.pallas_call(
        _mlp_kernel,
        grid=(T // tm, D_FF // tf),
        in_specs=[
            pl.BlockSpec((tm, D_MODEL), lambda i, f: (i, 0)),
            ms, ms, ms,
            pl.BlockSpec((1, D_MODEL), lambda i, f: (0, 0)),
            pl.BlockSpec((D_MODEL, tf), lambda i, f: (0, f)),
            pl.BlockSpec((tf, D_MODEL), lambda i, f: (f, 0)),
        ],
        out_specs=pl.BlockSpec((tm, D_MODEL), lambda i, f: (i, 0)),
        out_shape=jax.ShapeDtypeStruct((T, D_MODEL), F32),
        scratch_shapes=[pltpu.VMEM((tm, D_MODEL), BF16), pltpu.VMEM((tm, D_MODEL), F32)],
        compiler_params=_cparams("parallel", "arbitrary"),
        name="mlp",
    )(x, shift, scale, gate, g, w_up, w_down)


def _gmlp_kernel(x_ref, sh_ref, sc_ref, gt_ref, g_ref, win_ref, vg_ref, wmix_ref, bmix_ref, wout_ref,
                 *out_refs, tm, emit_v):
    o_ref = out_refs[0]
    gs_scr = out_refs[-1]
    x = x_ref[...]
    h = _modulate(x, g_ref[...], _rows(sh_ref), _rows(sc_ref)).astype(BF16)
    z = _gelu_tanh(_dot(h, win_ref[...]))
    u = z[:, :D_A]
    v = z[:, D_A:]
    v = v * lax.rsqrt(jnp.mean(v * v, axis=-1, keepdims=True) + EPS) * vg_ref[...]
    if emit_v:
        out_refs[1][...] = v
    vb = v.astype(BF16)
    for c in range(tm // CHUNK_A):
        r0 = c * CHUNK_A
        for gi in range(N_GROUPS_A):
            c0 = gi * GROUP_A
            s = _dot(wmix_ref[gi], vb[r0:r0 + CHUNK_A, c0:c0 + GROUP_A])
            s = s + bmix_ref[:, c0:c0 + GROUP_A]
            gs_scr[r0:r0 + CHUNK_A, c0:c0 + GROUP_A] = (u[r0:r0 + CHUNK_A, c0:c0 + GROUP_A] * s).astype(BF16)
    y = _dot(gs_scr[...], wout_ref[...])
    o_ref[...] = x + _rows(gt_ref) * y


def _gmlp_layer(x, shift, scale, gate, g, w_in, v_gain, w_mix, b_mix, w_out, per_row, rows_per_batch,
                emit_v):
    T = x.shape[0]
    tm = 512
    tpb = rows_per_batch // tm if not per_row else 1
    ms = _mod_spec(per_row, tm, tpb, 1)
    out_shape = [jax.ShapeDtypeStruct((T, D_MODEL), F32)]
    out_specs = [pl.BlockSpec((tm, D_MODEL), lambda i: (i, 0))]
    if emit_v:
        out_shape.append(jax.ShapeDtypeStruct((T, D_A), F32))
        out_specs.append(pl.BlockSpec((tm, D_A), lambda i: (i, 0)))
    return pl.pallas_call(
        functools.partial(_gmlp_kernel, tm=tm, emit_v=emit_v),
        grid=(T // tm,),
        in_specs=[
            pl.BlockSpec((tm, D_MODEL), lambda i: (i, 0)),
            ms, ms, ms,
            _const_spec((1, D_MODEL)),
            _const_spec((D_MODEL, 2 * D_A)),
            _const_spec((1, D_A)),
            _const_spec((N_GROUPS_A, CHUNK_A, CHUNK_A)),
            _const_spec((CHUNK_A, D_A)),
            _const_spec((D_A, D_MODEL)),
        ],
        out_specs=out_specs,
        out_shape=out_shape,
        scratch_shapes=[pltpu.VMEM((tm, D_A), BF16)],
        compiler_params=_cparams("parallel"),
        name="gmlp_mixer",
    )(x, shift, scale, gate, g, w_in, v_gain, w_mix, b_mix, w_out)


def _conv_kernel(*refs, tm, per_row, tiles_per_batch):
    if per_row:
        (x_ref, sh_ref, sc_ref, gt_ref, g_ref, win_ref, cw_ref, wout_ref, p1_ref, p2_ref,
         o_ref, xin_ref) = refs
    else:
        (x_ref, sh_ref, sc_ref, gt_ref, g_ref, win_ref, cw_ref, wout_ref,
         o_ref, tail_ref, carry_scr) = refs
    x = x_ref[...]
    h = _modulate(x, g_ref[...], _rows(sh_ref), _rows(sc_ref)).astype(BF16)
    bcx = _dot(h, win_ref[...])
    gate_out = bcx[:, :D_C]
    xin = bcx[:, D_C:2 * D_C] * bcx[:, 2 * D_C:]
    row = lax.broadcasted_iota(jnp.int32, (tm, D_C), 0)
    roll1 = pltpu.roll(xin, 1, 0)
    roll2 = pltpu.roll(xin, 2, 0)
    if per_row:
        pos = row % SUBLANES
        prev1 = jnp.where(pos == 0, p1_ref[...], roll1)
        prev2 = jnp.where(pos < 2, p2_ref[...], roll2)
        xin_ref[...] = xin
    else:
        @pl.when(pl.program_id(0) % tiles_per_batch == 0)
        def _():
            carry_scr[...] = jnp.zeros_like(carry_scr)
        c6 = carry_scr[SUBLANES - 2:SUBLANES - 1, :]
        c7 = carry_scr[SUBLANES - 1:SUBLANES, :]
        prev1 = jnp.where(row == 0, c7, roll1)
        prev2 = jnp.where(row == 0, c6, jnp.where(row == 1, c7, roll2))
        tail = xin[tm - SUBLANES:, :]
        carry_scr[...] = tail
        tail_ref[0] = tail
    y = cw_ref[0:1, :] * prev2 + cw_ref[1:2, :] * prev1 + cw_ref[2:3, :] * xin
    out = _dot((gate_out * y).astype(BF16), wout_ref[...])
    o_ref[...] = x + _rows(gt_ref) * out


def _conv_layer(x, shift, scale, gate, g, w_in, conv_w, w_out, per_row, rows_per_batch, fills=None):
    T = x.shape[0]
    tm = 512
    tpb = rows_per_batch // tm if not per_row else 1
    ms = _mod_spec(per_row, tm, tpb, 1)
    in_specs = [
        pl.BlockSpec((tm, D_MODEL), lambda i: (i, 0)),
        ms, ms, ms,
        _const_spec((1, D_MODEL)),
        _const_spec((D_MODEL, 3 * D_C)),
        _const_spec((CONV_W, D_C)),
        _const_spec((D_C, D_MODEL)),
    ]
    args = [x, shift, scale, gate, g, w_in, conv_w, w_out]
    out_shape = [jax.ShapeDtypeStruct((T, D_MODEL), F32)]
    out_specs = [pl.BlockSpec((tm, D_MODEL), lambda i: (i, 0))]
    scratch = []
    if per_row:
        in_specs += [pl.BlockSpec((tm, D_C), lambda i: (i, 0))] * 2
        args += list(fills)
        out_shape.append(jax.ShapeDtypeStruct((T, D_C), F32))
        out_specs.append(pl.BlockSpec((tm, D_C), lambda i: (i, 0)))
        sem = "parallel"
    else:
        nb = T // rows_per_batch
        out_shape.append(jax.ShapeDtypeStruct((nb, SUBLANES, D_C), F32))
        out_specs.append(pl.BlockSpec((1, SUBLANES, D_C), lambda i: (i // tpb, 0, 0)))
        scratch.append(pltpu.VMEM((SUBLANES, D_C), F32))
        sem = "arbitrary"
    return pl.pallas_call(
        functools.partial(_conv_kernel, tm=tm, per_row=per_row, tiles_per_batch=tpb),
        grid=(T // tm,),
        in_specs=in_specs,
        out_specs=out_specs,
        out_shape=out_shape,
        scratch_shapes=scratch,
        compiler_params=_cparams(sem),
        name="conv_mixer",
    )(*args)


def _qkv_kernel(x_ref, sh_ref, sc_ref, g_ref, w_ref, qg_ref, kg_ref, q_ref, k_ref, v_ref, *extra,
                tm, emit_attn_inputs):
    h = _modulate(x_ref[...], g_ref[...], _rows(sh_ref), _rows(sc_ref)).astype(BF16)
    qkv = _dot(h, w_ref[...])
    hd = HEAD_DIM_B
    for hh in range(N_HEADS_B):
        cols = slice(hh * hd, (hh + 1) * hd)
        q = qkv[:, hh * hd:(hh + 1) * hd]
        k = qkv[:, D_MODEL + hh * hd:D_MODEL + (hh + 1) * hd]
        q_ref[:, cols] = q * lax.rsqrt(jnp.mean(q * q, axis=-1, keepdims=True) + EPS) * qg_ref[...]
        kn = k * lax.rsqrt(jnp.mean(k * k, axis=-1, keepdims=True) + EPS) * kg_ref[...]
        k_ref[:, cols] = kn
        if emit_attn_inputs:
            kb_ref, vb_ref, km_ref = extra
            kb_ref[:, cols] = kn.astype(BF16)
            for c in range(tm // BLOCK_B):
                km_ref[c, :, cols] = jnp.sum(kn[c * BLOCK_B:(c + 1) * BLOCK_B], axis=0, keepdims=True) * (
                    1.0 / BLOCK_B)
    v = qkv[:, 2 * D_MODEL:]
    v_ref[...] = v
    if emit_attn_inputs:
        extra[1][...] = v.astype(BF16)


def _qkv_layer(x, shift, scale, g, w_qkv, q_gain, k_gain, per_row, rows_per_batch, emit_attn_inputs):
    T = x.shape[0]
    tm = 512
    tpb = rows_per_batch // tm if not per_row else 1
    ms = _mod_spec(per_row, tm, tpb, 1)
    row_spec = pl.BlockSpec((tm, D_MODEL), lambda i: (i, 0))
    out_specs = [row_spec, row_spec, row_spec]
    out_shape = [jax.ShapeDtypeStruct((T, D_MODEL), F32)] * 3
    if emit_attn_inputs:
        out_specs += [row_spec, row_spec,
                      pl.BlockSpec((tm // BLOCK_B, 1, D_MODEL), lambda i: (i, 0, 0))]
        out_shape += [jax.ShapeDtypeStruct((T, D_MODEL), BF16)] * 2
        out_shape += [jax.ShapeDtypeStruct((T // BLOCK_B, 1, D_MODEL), F32)]
    return pl.pallas_call(
        functools.partial(_qkv_kernel, tm=tm, emit_attn_inputs=emit_attn_inputs),
        grid=(T // tm,),
        in_specs=[row_spec, ms, ms,
                  _const_spec((1, D_MODEL)),
                  _const_spec((D_MODEL, 3 * D_MODEL)),
                  _const_spec((1, HEAD_DIM_B)),
                  _const_spec((1, HEAD_DIM_B))],
        out_specs=out_specs,
        out_shape=out_shape,
        compiler_params=_cparams("parallel"),
        name="moba_qkv",
    )(x, shift, scale, g, w_qkv, q_gain, k_gain)


def _topk_mask_t(gate_t, cands, blk_of_row, elig):
    gm = jnp.where(elig, gate_t, MASK_VALUE)
    cnt = jnp.zeros(gate_t.shape, F32)
    for r, cb in cands:
        row = gm[r:r + 1, :]
        beats = jnp.where(row > gm, 1.0, jnp.where(row == gm, jnp.where(cb < blk_of_row, 1.0, 0.0), 0.0))
        cnt = cnt + beats
    return jnp.where(elig, jnp.where(cnt < TOPK_B, 1.0, 0.0), 0.0)


def _moba_prompt_kernel(x_ref, gt_ref, wout_ref, q_ref, kb_ref, vb_ref, km_ref, bias_ref, o_ref, qaug_scr, *,
                        nblk, nh):
    qb = pl.program_id(1)
    hd = HEAD_DIM_B
    W = BLOCK_B
    n_near = bias_ref.shape[2] // W
    dead_lane = LANES - 1
    lane = lax.broadcasted_iota(jnp.int32, (W, LANES), 1)
    blk = lax.broadcasted_iota(jnp.int32, (nblk, W), 0)
    r_i = lax.broadcasted_iota(jnp.int32, (W, W), 0)
    c_i = lax.broadcasted_iota(jnp.int32, (W, W), 1)
    causal = r_i >= c_i

    def onehot(b, valid):
        return jnp.where(lane == jnp.where(valid, b, dead_lane), 1.0, 0.0).astype(BF16)

    def rows_of(b):
        return pl.ds(pl.multiple_of(b * W, W), W)

    near_blk = [jnp.maximum(qb - d, 0) for d in range(n_near)]
    oh_near = jnp.concatenate([onehot(qb - d, qb - d >= 0) for d in range(n_near)], axis=0)
    ones_near = jnp.ones((n_near * W, hd), BF16)
    heads = [slice(h * hd, (h + 1) * hd) for h in range(nh)]
    gates = [_dot_nt(km_ref[:, 0, cols].astype(BF16), q_ref[:, cols].astype(BF16)) for cols in heads]
    scores = []
    for h, cols in enumerate(heads):
        sel_t = _topk_mask_t(gates[h], [(j, j) for j in range(nblk)], blk, blk < qb)
        neg_t = jnp.where(blk == qb, 0.0, jnp.where(sel_t > 0.5, 0.0, MASK_VALUE))
        neg = jnp.concatenate([neg_t, jnp.zeros((LANES - nblk - 1, W), F32),
                               jnp.full((1, W), MASK_VALUE, F32)], axis=0).T
        q_aug = jnp.concatenate([(q_ref[:, cols] * (LOG2E / math.sqrt(hd))).astype(BF16), neg.astype(BF16)],
                                axis=1)
        qaug_scr[h] = q_aug
        k_near = jnp.concatenate([kb_ref[rows_of(b), cols] for b in near_blk], axis=0)
        scores.append(_dot_nt(q_aug, jnp.concatenate([k_near, oh_near], axis=1)))
    carry0 = []
    for h, cols in enumerate(heads):
        s = scores[h] + bias_ref[h]
        s = jnp.concatenate([jnp.where(causal, s[:, :W], MASK_VALUE), s[:, W:]], axis=1)
        m0 = jnp.max(s, axis=-1, keepdims=True)
        p = jnp.exp2(s - m0)
        v_near = jnp.concatenate([vb_ref[rows_of(b), cols] for b in near_blk], axis=0)
        carry0 += [m0, _dot(p.astype(BF16), jnp.concatenate([v_near, ones_near], axis=1))]

    n_far = jnp.maximum(qb - (n_near - 1), 0)
    ones_pair = jnp.ones((2 * W, hd), BF16)

    def body(i, carry):
        b0 = 2 * i
        b1 = b0 + 1
        oh = jnp.concatenate([onehot(b0, True), onehot(b1, b1 < n_far)], axis=0)
        pair_scores = []
        for h, cols in enumerate(heads):
            k_pair = jnp.concatenate([kb_ref[rows_of(b0), cols], kb_ref[rows_of(b1), cols]], axis=0)
            pair_scores.append(_dot_nt(qaug_scr[h], jnp.concatenate([k_pair, oh], axis=1)))
        out = []
        for h, cols in enumerate(heads):
            m, acc = carry[2 * h], carry[2 * h + 1]
            s = pair_scores[h]
            v_pair = jnp.concatenate([vb_ref[rows_of(b0), cols], vb_ref[rows_of(b1), cols]], axis=0)
            m_new = jnp.maximum(m, jnp.max(s, axis=-1, keepdims=True))
            p = jnp.exp2(s - m_new)
            acc = jnp.exp2(m - m_new) * acc + _dot(p.astype(BF16), jnp.concatenate([v_pair, ones_pair], axis=1))
            out += [m_new, acc]
        return tuple(out)

    carry = lax.fori_loop(0, (n_far + 1) // 2, body, tuple(carry0))
    o = jnp.concatenate([(carry[2 * h + 1][:, :hd] / carry[2 * h + 1][:, hd:]).astype(BF16) for h in range(nh)],
                        axis=1)
    o_ref[...] = x_ref[...] + _rows(gt_ref) * _dot(o, wout_ref[...])


def _moba_prompt_attention(x, gate, w_out, q, kb, vb, kmean, bias_near, batch, seq):
    nq = seq // BLOCK_B
    hd = HEAD_DIM_B
    nh = N_HEADS_B
    one = pl.Buffered(1)
    row_spec = pl.BlockSpec((BLOCK_B, D_MODEL), lambda b, i: (b * nq + i, 0))
    return pl.pallas_call(
        functools.partial(_moba_prompt_kernel, nblk=nq, nh=nh),
        grid=(batch, nq),
        in_specs=[
            row_spec,
            pl.BlockSpec((1, 1, D_MODEL), lambda b, i: (b, 0, 0)),
            pl.BlockSpec((D_MODEL, D_MODEL), lambda b, i: (0, 0), pipeline_mode=one),
            row_spec,
            pl.BlockSpec((seq, nh * hd), lambda b, i: (b, 0), pipeline_mode=one),
            pl.BlockSpec((seq, nh * hd), lambda b, i: (b, 0), pipeline_mode=one),
            pl.BlockSpec((nq, 1, nh * hd), lambda b, i: (b, 0, 0)),
            pl.BlockSpec(bias_near.shape, lambda b, i: (0, 0, 0), pipeline_mode=one),
        ],
        out_specs=row_spec,
        out_shape=jax.ShapeDtypeStruct((batch * seq, D_MODEL), F32),
        scratch_shapes=[pltpu.VMEM((nh, BLOCK_B, 2 * hd), BF16)],
        compiler_params=_cparams("parallel", "arbitrary"),
        name="moba_prompt_attn",
    )(x, gate, w_out, q, kb, vb, kmean, bias_near)


def _moba_sample_kernel(pt_ref, q_ref, kn_ref, vn_ref, *refs, pps, n_pages, t_new):
    kc_refs = refs[:pps]
    vc_refs = refs[pps:2 * pps]
    bias_ref, bias_own_ref = refs[2 * pps:2 * pps + 2]
    o_ref = refs[2 * pps + 2]
    qs_scr, qs2_scr, qg_scr, m_scr, l_scr, o_scr, ksum_scr = refs[2 * pps + 3:]
    del pt_ref
    s_id = pl.program_id(1)
    H, hd = N_HEADS_B, HEAD_DIM_B
    nrow = H * t_new
    ppb = BLOCK_B // PAGE_SIZE
    assert ppb == 2
    n_blk = n_pages // ppb
    page_rows = PAGE_SIZE * H

    @pl.when(s_id == 0)
    def _():
        q8 = q_ref[0]
        qall = jnp.concatenate([q8[:, h * hd:(h + 1) * hd] for h in range(H)], axis=0)
        qg_scr[...] = qall.astype(BF16)
        qs = (qall * (1.0 / math.sqrt(hd))).astype(BF16)
        qs_scr[...] = qs
        zero = jnp.zeros((nrow, hd), BF16)
        qs2_scr[...] = jnp.concatenate([jnp.concatenate([qs, zero], axis=1),
                                        jnp.concatenate([zero, qs], axis=1)], axis=0)

    own_head = (lax.broadcasted_iota(jnp.int32, (H, nrow), 0)
                == lax.broadcasted_iota(jnp.int32, (H, nrow), 1) // t_new)
    own_head2 = (lax.broadcasted_iota(jnp.int32, (H, ppb * nrow), 0)
                 == (lax.broadcasted_iota(jnp.int32, (H, ppb * nrow), 1) % nrow) // t_new)

    def block_scores(k_pages, bias):
        k2 = jnp.concatenate(k_pages, axis=1).astype(BF16)
        return _dot_nt(k2, qs2_scr[...]) + bias

    def block_attn(g, v_pages):
        v2 = jnp.concatenate(v_pages, axis=1).astype(BF16)
        g3 = g.reshape(PAGE_SIZE, H, ppb * nrow)
        m_t = jnp.max(g3, axis=0)
        m_t = jnp.maximum(m_t, pltpu.roll(m_t, nrow, 1))
        e3 = jnp.exp(g3 - jnp.where(own_head2, m_t, 0.0)[None])
        l_t = jnp.sum(e3, axis=0)
        l_t = l_t + pltpu.roll(l_t, nrow, 1)
        o2 = _dot_tn(e3.reshape(PAGE_SIZE * H, ppb * nrow).astype(BF16), v2)
        return m_t, l_t, o2[:nrow, :hd] + o2[nrow:, hd:]

    def partial_attn(k_rows, v_rows, bias, valid):
        n = k_rows.shape[0]
        g = _dot_nt(k_rows.astype(BF16), qs_scr[...]) + bias
        if valid is not None:
            g = jnp.where(valid, g, MASK_VALUE)
        g3 = g.reshape(n // H, H, nrow)
        m_t = jnp.max(g3, axis=0)
        e3 = jnp.exp(g3 - jnp.where(own_head, m_t, 0.0)[None])
        l_t = jnp.sum(e3, axis=0)
        o = _dot_tn(e3.reshape(n, nrow).astype(BF16), v_rows.astype(BF16))
        return m_t, l_t, o

    scores = []
    for i in range(pps // ppb):
        blk = s_id * (pps // ppb) + i
        k_pages = [kc_refs[i * ppb + u][0, 0].reshape(page_rows, hd) for u in range(ppb)]
        ksum_scr[blk] = jnp.sum((k_pages[0] + k_pages[1]).reshape(PAGE_SIZE, H, hd), axis=0)
        b0 = pl.multiple_of(blk * page_rows, page_rows)
        scores.append((blk, block_scores(k_pages, bias_ref[pl.ds(b0, page_rows), :])))
    for i, (blk, g) in enumerate(scores):
        v_pages = [vc_refs[i * ppb + u][0, 0].reshape(page_rows, hd) for u in range(ppb)]
        m_t, l_t, o = block_attn(g, v_pages)
        m_scr[blk] = m_t
        l_scr[blk] = l_t
        o_scr[blk] = o

    @pl.when(s_id == pl.num_programs(1) - 1)
    def _():
        def head_diag(t):
            return jnp.sum(jnp.where(own_head[None], t[:, :, :nrow], 0.0), axis=1)

        key_i = lax.broadcasted_iota(jnp.int32, (t_new * H, nrow), 0) // H
        qry_i = lax.broadcasted_iota(jnp.int32, (t_new * H, nrow), 1) % t_new
        m_ot, l_ot, o_o = partial_attn(kn_ref[0].reshape(t_new * H, hd), vn_ref[0].reshape(t_new * H, hd),
                                       bias_own_ref[...], key_i <= qry_i)
        m_o = head_diag(m_ot[None])
        l_o = head_diag(l_ot[None])

        kmean = (ksum_scr[...] * (1.0 / BLOCK_B)).reshape(n_blk * H, hd)
        gate = _dot_nt(kmean.astype(BF16), qg_scr[...])
        gate_t = head_diag(gate.reshape(n_blk, H, nrow))
        bi = lax.broadcasted_iota(jnp.int32, (n_blk, nrow), 0)
        sel = _topk_mask_t(gate_t, [(j, j) for j in range(n_blk)], bi, bi >= 0)

        m_all = head_diag(m_scr[...])
        l_all = head_diag(l_scr[...])
        m_fin = jnp.maximum(jnp.max(jnp.where(sel > 0.5, m_all, MASK_VALUE), axis=0, keepdims=True), m_o)
        w = jnp.where(sel > 0.5, jnp.exp(m_all - m_fin), 0.0)
        w_o = jnp.exp(m_o - m_fin)
        l_fin = jnp.sum(w * l_all, axis=0, keepdims=True) + w_o * l_o
        stack = jnp.concatenate([w, w_o, l_fin], axis=0)
        stack = jnp.concatenate([stack, jnp.zeros((stack.shape[0], LANES - nrow), F32)], axis=1)
        stack = jnp.concatenate([stack, jnp.zeros((LANES - stack.shape[0], LANES), F32)], axis=0)
        st = stack.T
        acc = st[:nrow, n_blk:n_blk + 1] * o_o
        for pp in range(n_blk):
            acc = acc + st[:nrow, pp:pp + 1] * o_scr[pp]
        out = acc / st[:nrow, n_blk + 1:n_blk + 2]
        for h in range(H):
            o_ref[0, :, h * hd:(h + 1) * hd] = out[h * t_new:(h + 1) * t_new, :]


def _moba_sample_attention(q, k_new, v_new, cache_k, cache_v, layer, page_table, bias_past, bias_own, t_new):
    nb, n_pages = page_table.shape
    pps = 8
    n_steps = n_pages // pps
    H, hd = N_HEADS_B, HEAD_DIM_B
    nrow = H * t_new
    n_blk = n_pages * PAGE_SIZE // BLOCK_B

    def page_spec(i):
        return pl.BlockSpec((1, 1, PAGE_SIZE, H, hd),
                            lambda b, s, pt: (layer, pt[b * n_pages + s * pps + i], 0, 0, 0))

    row_spec = pl.BlockSpec((1, t_new, H * hd), lambda b, s, pt: (b, 0, 0))
    own_spec = pl.BlockSpec((1, t_new, H, hd), lambda b, s, pt: (b, 0, 0, 0))
    one = pl.Buffered(1)
    grid_spec = pltpu.PrefetchScalarGridSpec(
        num_scalar_prefetch=1,
        grid=(nb, n_steps),
        in_specs=[row_spec, own_spec, own_spec]
                 + [page_spec(i) for i in range(pps)] + [page_spec(i) for i in range(pps)]
                 + [pl.BlockSpec(bias_past.shape, lambda b, s, pt: (0, 0), pipeline_mode=one),
                    pl.BlockSpec(bias_own.shape, lambda b, s, pt: (0, 0), pipeline_mode=one)],
        out_specs=row_spec,
        scratch_shapes=[
            pltpu.VMEM((nrow, hd), BF16),
            pltpu.VMEM((2 * nrow, 2 * hd), BF16),
            pltpu.VMEM((nrow, hd), BF16),
            pltpu.VMEM((n_blk, H, 2 * nrow), F32),
            pltpu.VMEM((n_blk, H, 2 * nrow), F32),
            pltpu.VMEM((n_blk, nrow, hd), F32),
            pltpu.VMEM((n_blk, H, hd), F32),
        ],
    )
    return pl.pallas_call(
        functools.partial(_moba_sample_kernel, pps=pps, n_pages=n_pages, t_new=t_new),
        grid_spec=grid_spec,
        out_shape=jax.ShapeDtypeStruct((nb, t_new, H * hd), F32),
        compiler_params=_cparams("parallel", "arbitrary"),
        name="moba_sample_attn",
    )(page_table.reshape(-1), q, k_new, v_new, *([cache_k] * pps), *([cache_v] * pps), bias_past, bias_own)


def _proj_kernel(x_ref, gt_ref, a_ref, w_ref, o_ref):
    o_ref[...] = x_ref[...] + _rows(gt_ref) * _dot(a_ref[...].astype(BF16), w_ref[...])


def _proj_residual(x, gate, a, w, per_row, rows_per_batch):
    T = x.shape[0]
    tm = 512
    tpb = rows_per_batch // tm if not per_row else 1
    row_spec = pl.BlockSpec((tm, D_MODEL), lambda i: (i, 0))
    return pl.pallas_call(
        _proj_kernel,
        grid=(T // tm,),
        in_specs=[row_spec, _mod_spec(per_row, tm, tpb, 1), row_spec,
                  _const_spec((D_MODEL, D_MODEL))],
        out_specs=row_spec,
        out_shape=jax.ShapeDtypeStruct((T, D_MODEL), F32),
        compiler_params=_cparams("parallel"),
        name="moba_out_proj",
    )(x, gate, a, w)


def _rel_bucket(dist):
    max_exact = N_BUCKETS // 2
    d = jnp.maximum(dist, 0)
    log_ratio = jnp.log(jnp.maximum(d, 1).astype(jnp.float32) / max_exact) / math.log(MAX_DIST / max_exact)
    large = jnp.minimum(max_exact + (log_ratio * (N_BUCKETS - max_exact)).astype(jnp.int32), N_BUCKETS - 1)
    return jnp.where(d < max_exact, d, large)


def _bias_lookup(dist, rel_bias, out_spec):
    onehot = (_rel_bucket(jnp.asarray(dist, jnp.int32))[..., None] == jnp.arange(N_BUCKETS)).astype(F32)
    return jnp.einsum(out_spec, onehot, rel_bias, precision=HIGHEST)


def _bias_tables(rel_bias, past_len, t_new):
    H, W = N_HEADS_B, BLOCK_B
    n_near = MAX_DIST // W + 1
    r = np.arange(W)[:, None]
    col = np.arange(n_near * W)[None, :]
    dist = (col // W) * W + r - (col % W)
    near = _bias_lookup(dist, rel_bias - rel_bias[N_BUCKETS - 1], "rcb,bh->hrc") * LOG2E
    ppb = W // PAGE_SIZE
    shape = (past_len // W * PAGE_SIZE * H, ppb * H * t_new)
    row = lax.broadcasted_iota(jnp.int32, shape, 0)
    col = lax.broadcasted_iota(jnp.int32, shape, 1)
    kpos = ((row // (PAGE_SIZE * H)) * ppb + col // (H * t_new)) * PAGE_SIZE + (row // H) % PAGE_SIZE
    bucket = _rel_bucket(past_len + col % t_new - kpos)
    col_bias = rel_bias[:, (np.arange(shape[1]) // t_new) % H]
    past = jnp.zeros(shape, F32)
    for b in range(N_BUCKETS):
        past = jnp.where(bucket == b, col_bias[b][None, :], past)
    past = jnp.where(row % H == (col // t_new) % H, past, MASK_VALUE)
    qr = np.arange(t_new)
    same_head = np.eye(H, dtype=bool)
    own = _bias_lookup(np.maximum(qr[None, :] - qr[:, None], 0), rel_bias, "trb,bh->thr")
    own = jnp.where(same_head[None, :, :, None], own[:, None], MASK_VALUE).reshape(t_new * H, H * t_new)
    return near, past, own


def kernel(x_prompt, x_sample, cache_k, cache_v, state_conv, page_table, c_prompt, c_sample, rel_bias, norm_mix, norm_mlp, w_mod, b_mod, w_up, w_down, a_w_in, a_v_gain, a_w_s, a_b_s, a_w_out, b_w_qkv, b_q_gain, b_k_gain, b_w_out, c_w_in, c_conv, c_w_out):
    B, S, D = x_prompt.shape
    DB, T, _ = x_sample.shape
    n_pages = page_table.shape[1]
    past_len = n_pages * PAGE_SIZE
    assert S % BLOCK_B == 0 and past_len % BLOCK_B == 0 and T == SUBLANES
    assert MAX_DIST % BLOCK_B == 0

    xp = x_prompt.reshape(B * S, D)
    xs = x_sample.reshape(DB * T, D)

    n_c = B + DB
    n_c_pad = -(-n_c // SUBLANES) * SUBLANES
    c_all = jnp.concatenate([c_prompt, c_sample, jnp.zeros((n_c_pad - n_c, D), F32)], axis=0)
    mods = _ada_all(c_all, w_mod, b_mod)

    bias_near, bias_past, bias_own = _bias_tables(rel_bias, past_len, T)

    ci = np.arange(CHUNK_A)
    tril_p = (ci[:, None] >= ci[None, :])
    tril_s = tril_p & ((ci[:, None] // T) == (ci[None, :] // T))

    k_p, v_p, k_s, v_s, conv_p, conv_s, chunkv_s = [], [], [], [], [], [], []
    for i in range(DEPTH):
        kind, j = i % N_MIXERS, i // N_MIXERS
        mp = [m.reshape(B, 1, D) for m in jnp.split(mods[i, :B], 6, axis=-1)]
        ms = [m.reshape(DB, 1, D) for m in jnp.split(mods[i, B:B + DB], 6, axis=-1)]
        g_mix = norm_mix[i].reshape(1, D)
        g_mlp = norm_mlp[i].reshape(1, D)
        if kind == 0:
            w_in = a_w_in[j].astype(BF16)
            w_out = a_w_out[j].astype(BF16)
            vg = a_v_gain[j].reshape(1, D_A)
            wmix_p = jnp.where(tril_p, a_w_s[j], 0.0).astype(BF16)
            bmix_p = jnp.repeat(jnp.transpose(a_b_s[j]), GROUP_A, axis=1)
            ws_t = jnp.tile(a_w_s[j][:, :T, :T], (1, CHUNK_A // T, CHUNK_A // T))
            wmix_s = jnp.where(tril_s, ws_t, 0.0).astype(BF16)
            bmix_s = jnp.repeat(jnp.tile(jnp.transpose(a_b_s[j][:, :T]), (CHUNK_A // T, 1)), GROUP_A, axis=1)
            (xp,) = _gmlp_layer(xp, mp[0], mp[1], mp[2], g_mix, w_in, vg, wmix_p, bmix_p, w_out,
                                False, S, False)
            xs, v_new = _gmlp_layer(xs, ms[0], ms[1], ms[2], g_mix, w_in, vg, wmix_s, bmix_s, w_out,
                                    True, T, True)
            chunkv_s.append(v_new.reshape(DB, T, D_A))
        elif kind == 1:
            w_qkv = b_w_qkv[j].astype(BF16)
            w_out = b_w_out[j].astype(BF16)
            qg = b_q_gain[j].reshape(1, HEAD_DIM_B)
            kg = b_k_gain[j].reshape(1, HEAD_DIM_B)
            qp, kp, vp, kbp, vbp, kmean_p = _qkv_layer(xp, mp[0], mp[1], g_mix, w_qkv, qg, kg, False, S, True)
            qs, ks, vs = _qkv_layer(xs, ms[0], ms[1], g_mix, w_qkv, qg, kg, True, T, False)
            xp = _moba_prompt_attention(xp, mp[2], w_out, qp, kbp, vbp, kmean_p, bias_near, B, S)
            os_ = _moba_sample_attention(
                qs.reshape(DB, T, D), ks.reshape(DB, T, N_HEADS_B, HEAD_DIM_B),
                vs.reshape(DB, T, N_HEADS_B, HEAD_DIM_B), cache_k, cache_v, j, page_table,
                bias_past, bias_own, T)
            xs = _proj_residual(xs, ms[2], os_.reshape(DB * T, D), w_out, True, T)
            k_p.append(kp.reshape(B, S, N_HEADS_B, HEAD_DIM_B))
            v_p.append(vp.reshape(B, S, N_HEADS_B, HEAD_DIM_B))
            k_s.append(ks.reshape(DB, T, N_HEADS_B, HEAD_DIM_B))
            v_s.append(vs.reshape(DB, T, N_HEADS_B, HEAD_DIM_B))
        else:
            w_in = c_w_in[j].astype(BF16)
            w_out = c_w_out[j].astype(BF16)
            st = state_conv[j]
            zrow = jnp.zeros((DB, T - 1, D_C), F32)
            p1 = jnp.concatenate([st[:, 1:2], zrow], axis=1).reshape(DB * T, D_C)
            p2 = jnp.concatenate([st, zrow[:, 1:]], axis=1).reshape(DB * T, D_C)
            xp, tail_p = _conv_layer(xp, mp[0], mp[1], mp[2], g_mix, w_in, c_conv[j], w_out, False, S)
            xs, xin_s = _conv_layer(xs, ms[0], ms[1], ms[2], g_mix, w_in, c_conv[j], w_out, True, T,
                                    fills=(p1, p2))
            conv_p.append(tail_p[:, SUBLANES - (CONV_W - 1):, :])
            conv_s.append(xin_s.reshape(DB, T, D_C)[:, T - (CONV_W - 1):, :])
        wu = w_up[i].astype(BF16)
        wd = w_down[i].astype(BF16)
        xp = _mlp_layer(xp, mp[3], mp[4], mp[5], g_mlp, wu, wd, False, S)
        xs = _mlp_layer(xs, ms[3], ms[4], ms[5], g_mlp, wu, wd, True, T)
    return (xp.reshape(B, S, D), xs.reshape(DB, T, D), jnp.stack(k_p), jnp.stack(v_p), jnp.stack(k_s),
            jnp.stack(v_s), jnp.stack(conv_p), jnp.stack(conv_s), jnp.stack(chunkv_s))
```

```python
import functools
import math

import jax
import jax.numpy as jnp
import numpy as np
from jax import lax
from jax.experimental import pallas as pl
from jax.experimental.pallas import tpu as pltpu

D_MODEL = 1024
DEPTH = 4
N_MIXERS = 3
D_A = 2 * D_MODEL
N_GROUPS_A = 8
GROUP_A = D_A // N_GROUPS_A
CHUNK_A = 128
HEAD_DIM_B = 128
N_HEADS_B = D_MODEL // HEAD_DIM_B
BLOCK_B = 256
TOPK_B = 3
N_BUCKETS = 32
MAX_DIST = 1024
D_C = D_MODEL
CONV_W = 3
D_FF = 4 * D_MODEL
EPS = 1e-6
MASK_VALUE = -1e30
LOG2E = math.log2(math.e)
PAGE_SIZE = 128

SUBLANES = 8
LANES = 128
VMEM_LIMIT_BYTES = 56 * 1024 * 1024

TM_MLP = 1024
TF_MLP = 2048
TM_MIXER = 512
TN_MOD = 1536
PAGES_PER_STEP = 16

F32 = jnp.float32
BF16 = jnp.bfloat16
HIGHEST = lax.Precision.HIGHEST


def _cparams(*sem):
    return pltpu.CompilerParams(dimension_semantics=sem, vmem_limit_bytes=VMEM_LIMIT_BYTES)


def _rows(ref):
    v = ref[...]
    n, _, d = v.shape
    if n == 1:
        return v.reshape(1, d)
    return jnp.broadcast_to(v, (n, SUBLANES, d)).reshape(n * SUBLANES, d)


def _modulate(x, g, shift, scale):
    y = x * lax.rsqrt(jnp.mean(x * x, axis=-1, keepdims=True) + EPS)
    return (y * g) * (1.0 + scale) + shift


def _gelu_tanh(x):
    c = math.sqrt(2.0 / math.pi)
    return x * (0.5 * (1.0 + jnp.tanh(c * (x + 0.044715 * (x * x * x)))))


def _dot(a, b):
    return jnp.dot(a, b, preferred_element_type=F32)


def _dot_nt(a, b, **kw):
    return lax.dot_general(a, b, (((1,), (1,)), ((), ())), preferred_element_type=F32, **kw)


def _dot_tn(a, b):
    return lax.dot_general(a, b, (((0,), (0,)), ((), ())), preferred_element_type=F32)


def _mod_spec(per_row, tm, tiles_per_batch, ngrid):
    if per_row:
        if ngrid == 1:
            return pl.BlockSpec((tm // SUBLANES, 1, D_MODEL), lambda i: (i, 0, 0))
        return pl.BlockSpec((tm // SUBLANES, 1, D_MODEL), lambda i, f: (i, 0, 0))
    if ngrid == 1:
        return pl.BlockSpec((1, 1, D_MODEL), lambda i: (i // tiles_per_batch, 0, 0))
    return pl.BlockSpec((1, 1, D_MODEL), lambda i, f: (i // tiles_per_batch, 0, 0))


def _const_spec(shape):
    nd = len(shape)
    return pl.BlockSpec(shape, lambda i: (0,) * nd, pipeline_mode=pl.Buffered(1))


def _mod_kernel(c_ref, w_ref, b_ref, o_ref):
    c = c_ref[...]
    sc = (c * jax.nn.sigmoid(c)).astype(BF16)
    o_ref[0] = _dot(sc, w_ref[0].astype(BF16)) + b_ref[0]


def _ada_all(c_all, w_mod, b_mod):
    nrow = c_all.shape[0]
    tn = TN_MOD
    nn = (6 * D_MODEL) // tn
    return pl.pallas_call(
        _mod_kernel,
        grid=(DEPTH, nn),
        in_specs=[
            pl.BlockSpec((nrow, D_MODEL), lambda l, n: (0, 0)),
            pl.BlockSpec((1, D_MODEL, tn), lambda l, n: (l, 0, n)),
            pl.BlockSpec((1, 1, tn), lambda l, n: (l, 0, n)),
        ],
        out_specs=pl.BlockSpec((1, nrow, tn), lambda l, n: (l, 0, n)),
        out_shape=jax.ShapeDtypeStruct((DEPTH, nrow, 6 * D_MODEL), F32),
        compiler_params=_cparams("arbitrary", "arbitrary"),
        name="ada_mod",
    )(c_all, w_mod, b_mod.reshape(DEPTH, 1, 6 * D_MODEL))


def _mlp_kernel(x_ref, sh_ref, sc_ref, gt_ref, g_ref, wu_ref, wd_ref, o_ref, h_scr, acc_scr):
    f = pl.program_id(1)

    @pl.when(f == 0)
    def _():
        h = _modulate(x_ref[...], g_ref[...], _rows(sh_ref), _rows(sc_ref))
        h_scr[...] = h.astype(BF16)
        acc_scr[...] = jnp.zeros_like(acc_scr)

    a = jnp.maximum(_dot(h_scr[...], wu_ref[...]), 0.0)
    acc_scr[...] += _dot((a * a).astype(BF16), wd_ref[...])

    @pl.when(f == pl.num_programs(1) - 1)
    def _():
        o_ref[...] = x_ref[...] + _rows(gt_ref) * acc_scr[...]


def _mlp_layer(x, shift, scale, gate, g, w_up, w_down, per_row, rows_per_batch):
    T = x.shape[0]
    tm = TM_MLP
    tf = TF_MLP
    tpb = rows_per_batch // tm if not per_row else 1
    ms = _mod_spec(per_row, tm, tpb, 2)
    return pl.pallas_call(
        _mlp_kernel,
        grid=(T // tm, D_FF // tf),
        in_specs=[
            pl.BlockSpec((tm, D_MODEL), lambda i, f: (i, 0)),
            ms, ms, ms,
            pl.BlockSpec((1, D_MODEL), lambda i, f: (0, 0)),
            pl.BlockSpec((D_MODEL, tf), lambda i, f: (0, f)),
            pl.BlockSpec((tf, D_MODEL), lambda i, f: (f, 0)),
        ],
        out_specs=pl.BlockSpec((tm, D_MODEL), lambda i, f: (i, 0)),
        out_shape=jax.ShapeDtypeStruct((T, D_MODEL), F32),
        scratch_shapes=[pltpu.VMEM((tm, D_MODEL), BF16), pltpu.VMEM((tm, D_MODEL), F32)],
        compiler_params=_cparams("parallel", "arbitrary"),
        name="mlp",
    )(x, shift, scale, gate, g, w_up, w_down)


def _gmlp_kernel(x_ref, sh_ref, sc_ref, gt_ref, g_ref, win_ref, vg_ref, wmix_ref, bmix_ref, wout_ref,
                 *out_refs, tm, emit_v):
    o_ref = out_refs[0]
    gs_scr = out_refs[-1]
    x = x_ref[...]
    h = _modulate(x, g_ref[...], _rows(sh_ref), _rows(sc_ref)).astype(BF16)
    z = _gelu_tanh(_dot(h, win_ref[...]))
    u = z[:, :D_A]
    v = z[:, D_A:]
    v = v * lax.rsqrt(jnp.mean(v * v, axis=-1, keepdims=True) + EPS) * vg_ref[...]
    if emit_v:
        out_refs[1][...] = v
    vb = v.astype(BF16)
    for c in range(tm // CHUNK_A):
        r0 = c * CHUNK_A
        for gi in range(N_GROUPS_A):
            c0 = gi * GROUP_A
            s = _dot(wmix_ref[gi], vb[r0:r0 + CHUNK_A, c0:c0 + GROUP_A])
            s = s + bmix_ref[:, c0:c0 + GROUP_A]
            gs_scr[r0:r0 + CHUNK_A, c0:c0 + GROUP_A] = (u[r0:r0 + CHUNK_A, c0:c0 + GROUP_A] * s).astype(BF16)
    y = _dot(gs_scr[...], wout_ref[...])
    o_ref[...] = x + _rows(gt_ref) * y


def _gmlp_layer(x, shift, scale, gate, g, w_in, v_gain, w_mix, b_mix, w_out, per_row, rows_per_batch,
                emit_v):
    T = x.shape[0]
    tm = TM_MIXER
    tpb = rows_per_batch // tm if not per_row else 1
    ms = _mod_spec(per_row, tm, tpb, 1)
    out_shape = [jax.ShapeDtypeStruct((T, D_MODEL), F32)]
    out_specs = [pl.BlockSpec((tm, D_MODEL), lambda i: (i, 0))]
    if emit_v:
        out_shape.append(jax.ShapeDtypeStruct((T, D_A), F32))
        out_specs.append(pl.BlockSpec((tm, D_A), lambda i: (i, 0)))
    return pl.pallas_call(
        functools.partial(_gmlp_kernel, tm=tm, emit_v=emit_v),
        grid=(T // tm,),
        in_specs=[
            pl.BlockSpec((tm, D_MODEL), lambda i: (i, 0)),
            ms, ms, ms,
            _const_spec((1, D_MODEL)),
            _const_spec((D_MODEL, 2 * D_A)),
            _const_spec((1, D_A)),
            _const_spec((N_GROUPS_A, CHUNK_A, CHUNK_A)),
            _const_spec((CHUNK_A, D_A)),
            _const_spec((D_A, D_MODEL)),
        ],
        out_specs=out_specs,
        out_shape=out_shape,
        scratch_shapes=[pltpu.VMEM((tm, D_A), BF16)],
        compiler_params=_cparams("parallel"),
        name="gmlp_mixer",
    )(x, shift, scale, gate, g, w_in, v_gain, w_mix, b_mix, w_out)


def _conv_kernel(*refs, tm, per_row, tiles_per_batch):
    if per_row:
        (x_ref, sh_ref, sc_ref, gt_ref, g_ref, win_ref, cw_ref, wout_ref, p1_ref, p2_ref,
         o_ref, xin_ref) = refs
    else:
        (x_ref, sh_ref, sc_ref, gt_ref, g_ref, win_ref, cw_ref, wout_ref,
         o_ref, tail_ref, carry_scr) = refs
    x = x_ref[...]
    h = _modulate(x, g_ref[...], _rows(sh_ref), _rows(sc_ref)).astype(BF16)
    bcx = _dot(h, win_ref[...])
    gate_out = bcx[:, :D_C]
    xin = bcx[:, D_C:2 * D_C] * bcx[:, 2 * D_C:]
    row = lax.broadcasted_iota(jnp.int32, (tm, D_C), 0)
    roll1 = pltpu.roll(xin, 1, 0)
    roll2 = pltpu.roll(xin, 2, 0)
    if per_row:
        pos = row % SUBLANES
        prev1 = jnp.where(pos == 0, p1_ref[...], roll1)
        prev2 = jnp.where(pos < 2, p2_ref[...], roll2)
        xin_ref[...] = xin
    else:
        @pl.when(pl.program_id(0) % tiles_per_batch == 0)
        def _():
            carry_scr[...] = jnp.zeros_like(carry_scr)
        c6 = carry_scr[SUBLANES - 2:SUBLANES - 1, :]
        c7 = carry_scr[SUBLANES - 1:SUBLANES, :]
        prev1 = jnp.where(row == 0, c7, roll1)
        prev2 = jnp.where(row == 0, c6, jnp.where(row == 1, c7, roll2))
        tail = xin[tm - SUBLANES:, :]
        carry_scr[...] = tail
        tail_ref[0] = tail
    y = cw_ref[0:1, :] * prev2 + cw_ref[1:2, :] * prev1 + cw_ref[2:3, :] * xin
    out = _dot((gate_out * y).astype(BF16), wout_ref[...])
    o_ref[...] = x + _rows(gt_ref) * out


def _conv_layer(x, shift, scale, gate, g, w_in, conv_w, w_out, per_row, rows_per_batch, fills=None):
    T = x.shape[0]
    tm = TM_MIXER
    tpb = rows_per_batch // tm if not per_row else 1
    ms = _mod_spec(per_row, tm, tpb, 1)
    in_specs = [
        pl.BlockSpec((tm, D_MODEL), lambda i: (i, 0)),
        ms, ms, ms,
        _const_spec((1, D_MODEL)),
        _const_spec((D_MODEL, 3 * D_C)),
        _const_spec((CONV_W, D_C)),
        _const_spec((D_C, D_MODEL)),
    ]
    args = [x, shift, scale, gate, g, w_in, conv_w, w_out]
    out_shape = [jax.ShapeDtypeStruct((T, D_MODEL), F32)]
    out_specs = [pl.BlockSpec((tm, D_MODEL), lambda i: (i, 0))]
    scratch = []
    if per_row:
        in_specs += [pl.BlockSpec((tm, D_C), lambda i: (i, 0))] * 2
        args += list(fills)
        out_shape.append(jax.ShapeDtypeStruct((T, D_C), F32))
        out_specs.append(pl.BlockSpec((tm, D_C), lambda i: (i, 0)))
        sem = "parallel"
    else:
        nb = T // rows_per_batch
        out_shape.append(jax.ShapeDtypeStruct((nb, SUBLANES, D_C), F32))
        out_specs.append(pl.BlockSpec((1, SUBLANES, D_C), lambda i: (i // tpb, 0, 0)))
        scratch.append(pltpu.VMEM((SUBLANES, D_C), F32))
        sem = "arbitrary"
    return pl.pallas_call(
        functools.partial(_conv_kernel, tm=tm, per_row=per_row, tiles_per_batch=tpb),
        grid=(T // tm,),
        in_specs=in_specs,
        out_specs=out_specs,
        out_shape=out_shape,
        scratch_shapes=scratch,
        compiler_params=_cparams(sem),
        name="conv_mixer",
    )(*args)


def _qkv_kernel(x_ref, sh_ref, sc_ref, g_ref, w_ref, qg_ref, kg_ref, q_ref, k_ref, v_ref, *extra,
                tm, emit_attn_inputs):
    h = _modulate(x_ref[...], g_ref[...], _rows(sh_ref), _rows(sc_ref)).astype(BF16)
    qkv = _dot(h, w_ref[...])
    hd = HEAD_DIM_B
    for hh in range(N_HEADS_B):
        cols = slice(hh * hd, (hh + 1) * hd)
        q = qkv[:, hh * hd:(hh + 1) * hd]
        k = qkv[:, D_MODEL + hh * hd:D_MODEL + (hh + 1) * hd]
        q_ref[:, cols] = q * lax.rsqrt(jnp.mean(q * q, axis=-1, keepdims=True) + EPS) * qg_ref[...]
        kn = k * lax.rsqrt(jnp.mean(k * k, axis=-1, keepdims=True) + EPS) * kg_ref[...]
        k_ref[:, cols] = kn
        if emit_attn_inputs:
            kb_ref, vb_ref, km_ref = extra
            kb_ref[:, cols] = kn.astype(BF16)
            for c in range(tm // BLOCK_B):
                km_ref[c, :, cols] = jnp.sum(kn[c * BLOCK_B:(c + 1) * BLOCK_B], axis=0, keepdims=True) * (
                    1.0 / BLOCK_B)
    v = qkv[:, 2 * D_MODEL:]
    v_ref[...] = v
    if emit_attn_inputs:
        extra[1][...] = v.astype(BF16)


def _qkv_layer(x, shift, scale, g, w_qkv, q_gain, k_gain, per_row, rows_per_batch, emit_attn_inputs):
    T = x.shape[0]
    tm = TM_MIXER
    tpb = rows_per_batch // tm if not per_row else 1
    ms = _mod_spec(per_row, tm, tpb, 1)
    row_spec = pl.BlockSpec((tm, D_MODEL), lambda i: (i, 0))
    out_specs = [row_spec, row_spec, row_spec]
    out_shape = [jax.ShapeDtypeStruct((T, D_MODEL), F32)] * 3
    if emit_attn_inputs:
        out_specs += [row_spec, row_spec,
                      pl.BlockSpec((tm // BLOCK_B, 1, D_MODEL), lambda i: (i, 0, 0))]
        out_shape += [jax.ShapeDtypeStruct((T, D_MODEL), BF16)] * 2
        out_shape += [jax.ShapeDtypeStruct((T // BLOCK_B, 1, D_MODEL), F32)]
    return pl.pallas_call(
        functools.partial(_qkv_kernel, tm=tm, emit_attn_inputs=emit_attn_inputs),
        grid=(T // tm,),
        in_specs=[row_spec, ms, ms,
                  _const_spec((1, D_MODEL)),
                  _const_spec((D_MODEL, 3 * D_MODEL)),
                  _const_spec((1, HEAD_DIM_B)),
                  _const_spec((1, HEAD_DIM_B))],
        out_specs=out_specs,
        out_shape=out_shape,
        compiler_params=_cparams("parallel"),
        name="moba_qkv",
    )(x, shift, scale, g, w_qkv, q_gain, k_gain)


def _topk_mask_t(gate_t, cands, blk_of_row, elig):
    gm = jnp.where(elig, gate_t, MASK_VALUE)
    cnt = jnp.zeros(gate_t.shape, F32)
    for r, cb in cands:
        row = gm[r:r + 1, :]
        beats = jnp.where(row > gm, 1.0, jnp.where(row == gm, jnp.where(cb < blk_of_row, 1.0, 0.0), 0.0))
        cnt = cnt + beats
    return jnp.where(elig, jnp.where(cnt < TOPK_B, 1.0, 0.0), 0.0)


def _moba_prompt_kernel(x_ref, gt_ref, wout_ref, q_ref, kb_ref, vb_ref, km_ref, bias_ref, o_ref, qaug_scr, *,
                        nblk, nh):
    qb = pl.program_id(1)
    hd = HEAD_DIM_B
    W = BLOCK_B
    n_near = bias_ref.shape[2] // W
    dead_lane = LANES - 1
    lane = lax.broadcasted_iota(jnp.int32, (W, LANES), 1)
    blk = lax.broadcasted_iota(jnp.int32, (nblk, W), 0)
    r_i = lax.broadcasted_iota(jnp.int32, (W, W), 0)
    c_i = lax.broadcasted_iota(jnp.int32, (W, W), 1)
    causal = r_i >= c_i

    def onehot(b, valid):
        return jnp.where(lane == jnp.where(valid, b, dead_lane), 1.0, 0.0).astype(BF16)

    def rows_of(b):
        return pl.ds(pl.multiple_of(b * W, W), W)

    near_blk = [jnp.maximum(qb - d, 0) for d in range(n_near)]
    oh_near = jnp.concatenate([onehot(qb - d, qb - d >= 0) for d in range(n_near)], axis=0)
    ones_near = jnp.ones((n_near * W, hd), BF16)
    heads = [slice(h * hd, (h + 1) * hd) for h in range(nh)]
    gates = [_dot_nt(km_ref[:, 0, cols].astype(BF16), q_ref[:, cols].astype(BF16)) for cols in heads]
    scores = []
    for h, cols in enumerate(heads):
        sel_t = _topk_mask_t(gates[h], [(j, j) for j in range(nblk)], blk, blk < qb)
        neg_t = jnp.where(blk == qb, 0.0, jnp.where(sel_t > 0.5, 0.0, MASK_VALUE))
        neg = jnp.concatenate([neg_t, jnp.zeros((LANES - nblk - 1, W), F32),
                               jnp.full((1, W), MASK_VALUE, F32)], axis=0).T
        q_aug = jnp.concatenate([(q_ref[:, cols] * (LOG2E / math.sqrt(hd))).astype(BF16), neg.astype(BF16)],
                                axis=1)
        qaug_scr[h] = q_aug
        k_near = jnp.concatenate([kb_ref[rows_of(b), cols] for b in near_blk], axis=0)
        scores.append(_dot_nt(q_aug, jnp.concatenate([k_near, oh_near], axis=1)))
    carry0 = []
    for h, cols in enumerate(heads):
        s = scores[h] + bias_ref[h]
        s = jnp.concatenate([jnp.where(causal, s[:, :W], MASK_VALUE), s[:, W:]], axis=1)
        m0 = jnp.max(s, axis=-1, keepdims=True)
        p = jnp.exp2(s - m0)
        v_near = jnp.concatenate([vb_ref[rows_of(b), cols] for b in near_blk], axis=0)
        carry0 += [m0, _dot(p.astype(BF16), jnp.concatenate([v_near, ones_near], axis=1))]

    n_far = jnp.maximum(qb - (n_near - 1), 0)
    ones_pair = jnp.ones((2 * W, hd), BF16)

    def body(i, carry):
        b0 = 2 * i
        b1 = b0 + 1
        oh = jnp.concatenate([onehot(b0, True), onehot(b1, b1 < n_far)], axis=0)
        pair_scores = []
        for h, cols in enumerate(heads):
            k_pair = jnp.concatenate([kb_ref[rows_of(b0), cols], kb_ref[rows_of(b1), cols]], axis=0)
            pair_scores.append(_dot_nt(qaug_scr[h], jnp.concatenate([k_pair, oh], axis=1)))
        out = []
        for h, cols in enumerate(heads):
            m, acc = carry[2 * h], carry[2 * h + 1]
            s = pair_scores[h]
            v_pair = jnp.concatenate([vb_ref[rows_of(b0), cols], vb_ref[rows_of(b1), cols]], axis=0)
            m_new = jnp.maximum(m, jnp.max(s, axis=-1, keepdims=True))
            p = jnp.exp2(s - m_new)
            acc = jnp.exp2(m - m_new) * acc + _dot(p.astype(BF16), jnp.concatenate([v_pair, ones_pair], axis=1))
            out += [m_new, acc]
        return tuple(out)

    carry = lax.fori_loop(0, (n_far + 1) // 2, body, tuple(carry0))
    o = jnp.concatenate([(carry[2 * h + 1][:, :hd] / carry[2 * h + 1][:, hd:]).astype(BF16) for h in range(nh)],
                        axis=1)
    o_ref[...] = x_ref[...] + _rows(gt_ref) * _dot(o, wout_ref[...])


def _moba_prompt_attention(x, gate, w_out, q, kb, vb, kmean, bias_near, batch, seq):
    nq = seq // BLOCK_B
    hd = HEAD_DIM_B
    nh = N_HEADS_B
    one = pl.Buffered(1)
    row_spec = pl.BlockSpec((BLOCK_B, D_MODEL), lambda b, i: (b * nq + i, 0))
    return pl.pallas_call(
        functools.partial(_moba_prompt_kernel, nblk=nq, nh=nh),
        grid=(batch, nq),
        in_specs=[
            row_spec,
            pl.BlockSpec((1, 1, D_MODEL), lambda b, i: (b, 0, 0)),
            pl.BlockSpec((D_MODEL, D_MODEL), lambda b, i: (0, 0), pipeline_mode=one),
            row_spec,
            pl.BlockSpec((seq, nh * hd), lambda b, i: (b, 0), pipeline_mode=one),
            pl.BlockSpec((seq, nh * hd), lambda b, i: (b, 0), pipeline_mode=one),
            pl.BlockSpec((nq, 1, nh * hd), lambda b, i: (b, 0, 0)),
            pl.BlockSpec(bias_near.shape, lambda b, i: (0, 0, 0), pipeline_mode=one),
        ],
        out_specs=row_spec,
        out_shape=jax.ShapeDtypeStruct((batch * seq, D_MODEL), F32),
        scratch_shapes=[pltpu.VMEM((nh, BLOCK_B, 2 * hd), BF16)],
        compiler_params=_cparams("parallel", "arbitrary"),
        name="moba_prompt_attn",
    )(x, gate, w_out, q, kb, vb, kmean, bias_near)


def _moba_sample_kernel(pt_ref, q_ref, kn_ref, vn_ref, *refs, pps, n_pages, t_new):
    kc_refs = refs[:pps]
    vc_refs = refs[pps:2 * pps]
    bias_ref, bias_own_ref = refs[2 * pps:2 * pps + 2]
    o_ref = refs[2 * pps + 2]
    qs_scr, qs2_scr, qg_scr, m_scr, l_scr, o_scr, ksum_scr = refs[2 * pps + 3:]
    del pt_ref
    s_id = pl.program_id(1)
    H, hd = N_HEADS_B, HEAD_DIM_B
    nrow = H * t_new
    ppb = BLOCK_B // PAGE_SIZE
    assert ppb == 2
    n_blk = n_pages // ppb
    page_rows = PAGE_SIZE * H

    @pl.when(s_id == 0)
    def _():
        q8 = q_ref[0]
        qall = jnp.concatenate([q8[:, h * hd:(h + 1) * hd] for h in range(H)], axis=0)
        qg_scr[...] = qall.astype(BF16)
        qs = (qall * (1.0 / math.sqrt(hd))).astype(BF16)
        qs_scr[...] = qs
        zero = jnp.zeros((nrow, hd), BF16)
        qs2_scr[...] = jnp.concatenate([jnp.concatenate([qs, zero], axis=1),
                                        jnp.concatenate([zero, qs], axis=1)], axis=0)

    own_head = (lax.broadcasted_iota(jnp.int32, (H, nrow), 0)
                == lax.broadcasted_iota(jnp.int32, (H, nrow), 1) // t_new)
    own_head2 = (lax.broadcasted_iota(jnp.int32, (H, ppb * nrow), 0)
                 == (lax.broadcasted_iota(jnp.int32, (H, ppb * nrow), 1) % nrow) // t_new)

    def block_scores(k_pages, bias):
        k2 = jnp.concatenate(k_pages, axis=1).astype(BF16)
        return _dot_nt(k2, qs2_scr[...]) + bias

    def block_attn(g, v_pages):
        v2 = jnp.concatenate(v_pages, axis=1).astype(BF16)
        g3 = g.reshape(PAGE_SIZE, H, ppb * nrow)
        m_t = jnp.max(g3, axis=0)
        m_t = jnp.maximum(m_t, pltpu.roll(m_t, nrow, 1))
        e3 = jnp.exp(g3 - jnp.where(own_head2, m_t, 0.0)[None])
        l_t = jnp.sum(e3, axis=0)
        l_t = l_t + pltpu.roll(l_t, nrow, 1)
        o2 = _dot_tn(e3.reshape(PAGE_SIZE * H, ppb * nrow).astype(BF16), v2)
        return m_t, l_t, o2[:nrow, :hd] + o2[nrow:, hd:]

    def partial_attn(k_rows, v_rows, bias, valid):
        n = k_rows.shape[0]
        g = _dot_nt(k_rows.astype(BF16), qs_scr[...]) + bias
        if valid is not None:
            g = jnp.where(valid, g, MASK_VALUE)
        g3 = g.reshape(n // H, H, nrow)
        m_t = jnp.max(g3, axis=0)
        e3 = jnp.exp(g3 - jnp.where(own_head, m_t, 0.0)[None])
        l_t = jnp.sum(e3, axis=0)
        o = _dot_tn(e3.reshape(n, nrow).astype(BF16), v_rows.astype(BF16))
        return m_t, l_t, o

    scores = []
    for i in range(pps // ppb):
        blk = s_id * (pps // ppb) + i
        k_pages = [kc_refs[i * ppb + u][0, 0].reshape(page_rows, hd) for u in range(ppb)]
        ksum_scr[blk] = jnp.sum((k_pages[0] + k_pages[1]).reshape(PAGE_SIZE, H, hd), axis=0)
        b0 = pl.multiple_of(blk * page_rows, page_rows)
        scores.append((blk, block_scores(k_pages, bias_ref[pl.ds(b0, page_rows), :])))
    for i, (blk, g) in enumerate(scores):
        v_pages = [vc_refs[i * ppb + u][0, 0].reshape(page_rows, hd) for u in range(ppb)]
        m_t, l_t, o = block_attn(g, v_pages)
        m_scr[blk] = m_t
        l_scr[blk] = l_t
        o_scr[blk] = o

    @pl.when(s_id == pl.num_programs(1) - 1)
    def _():
        def head_diag(t):
            return jnp.sum(jnp.where(own_head[None], t[:, :, :nrow], 0.0), axis=1)

        key_i = lax.broadcasted_iota(jnp.int32, (t_new * H, nrow), 0) // H
        qry_i = lax.broadcasted_iota(jnp.int32, (t_new * H, nrow), 1) % t_new
        m_ot, l_ot, o_o = partial_attn(kn_ref[0].reshape(t_new * H, hd), vn_ref[0].reshape(t_new * H, hd),
                                       bias_own_ref[...], key_i <= qry_i)
        m_o = head_diag(m_ot[None])
        l_o = head_diag(l_ot[None])

        kmean = (ksum_scr[...] * (1.0 / BLOCK_B)).reshape(n_blk * H, hd)
        gate = _dot_nt(kmean.astype(BF16), qg_scr[...])
        gate_t = head_diag(gate.reshape(n_blk, H, nrow))
        bi = lax.broadcasted_iota(jnp.int32, (n_blk, nrow), 0)
        sel = _topk_mask_t(gate_t, [(j, j) for j in range(n_blk)], bi, bi >= 0)

        m_all = head_diag(m_scr[...])
        l_all = head_diag(l_scr[...])
        m_fin = jnp.maximum(jnp.max(jnp.where(sel > 0.5, m_all, MASK_VALUE), axis=0, keepdims=True), m_o)
        w = jnp.where(sel > 0.5, jnp.exp(m_all - m_fin), 0.0)
        w_o = jnp.exp(m_o - m_fin)
        l_fin = jnp.sum(w * l_all, axis=0, keepdims=True) + w_o * l_o
        stack = jnp.concatenate([w, w_o, l_fin], axis=0)
        stack = jnp.concatenate([stack, jnp.zeros((stack.shape[0], LANES - nrow), F32)], axis=1)
        stack = jnp.concatenate([stack, jnp.zeros((LANES - stack.shape[0], LANES), F32)], axis=0)
        st = stack.T
        acc = st[:nrow, n_blk:n_blk + 1] * o_o
        for pp in range(n_blk):
            acc = acc + st[:nrow, pp:pp + 1] * o_scr[pp]
        out = acc / st[:nrow, n_blk + 1:n_blk + 2]
        for h in range(H):
            o_ref[0, :, h * hd:(h + 1) * hd] = out[h * t_new:(h + 1) * t_new, :]


def _moba_sample_attention(q, k_new, v_new, cache_k, cache_v, layer, page_table, bias_past, bias_own, t_new):
    nb, n_pages = page_table.shape
    pps = PAGES_PER_STEP
    n_steps = n_pages // pps
    H, hd = N_HEADS_B, HEAD_DIM_B
    nrow = H * t_new
    n_blk = n_pages * PAGE_SIZE // BLOCK_B

    def page_spec(i):
        return pl.BlockSpec((1, 1, PAGE_SIZE, H, hd),
                            lambda b, s, pt: (layer, pt[b * n_pages + s * pps + i], 0, 0, 0))

    row_spec = pl.BlockSpec((1, t_new, H * hd), lambda b, s, pt: (b, 0, 0))
    own_spec = pl.BlockSpec((1, t_new, H, hd), lambda b, s, pt: (b, 0, 0, 0))
    one = pl.Buffered(1)
    grid_spec = pltpu.PrefetchScalarGridSpec(
        num_scalar_prefetch=1,
        grid=(nb, n_steps),
        in_specs=[row_spec, own_spec, own_spec]
                 + [page_spec(i) for i in range(pps)] + [page_spec(i) for i in range(pps)]
                 + [pl.BlockSpec(bias_past.shape, lambda b, s, pt: (0, 0), pipeline_mode=one),
                    pl.BlockSpec(bias_own.shape, lambda b, s, pt: (0, 0), pipeline_mode=one)],
        out_specs=row_spec,
        scratch_shapes=[
            pltpu.VMEM((nrow, hd), BF16),
            pltpu.VMEM((2 * nrow, 2 * hd), BF16),
            pltpu.VMEM((nrow, hd), BF16),
            pltpu.VMEM((n_blk, H, 2 * nrow), F32),
            pltpu.VMEM((n_blk, H, 2 * nrow), F32),
            pltpu.VMEM((n_blk, nrow, hd), F32),
            pltpu.VMEM((n_blk, H, hd), F32),
        ],
    )
    return pl.pallas_call(
        functools.partial(_moba_sample_kernel, pps=pps, n_pages=n_pages, t_new=t_new),
        grid_spec=grid_spec,
        out_shape=jax.ShapeDtypeStruct((nb, t_new, H * hd), F32),
        compiler_params=_cparams("parallel", "arbitrary"),
        name="moba_sample_attn",
    )(page_table.reshape(-1), q, k_new, v_new, *([cache_k] * pps), *([cache_v] * pps), bias_past, bias_own)


def _proj_kernel(x_ref, gt_ref, a_ref, w_ref, o_ref):
    o_ref[...] = x_ref[...] + _rows(gt_ref) * _dot(a_ref[...].astype(BF16), w_ref[...])


def _proj_residual(x, gate, a, w, per_row, rows_per_batch):
    T = x.shape[0]
    tm = TM_MIXER
    tpb = rows_per_batch // tm if not per_row else 1
    row_spec = pl.BlockSpec((tm, D_MODEL), lambda i: (i, 0))
    return pl.pallas_call(
        _proj_kernel,
        grid=(T // tm,),
        in_specs=[row_spec, _mod_spec(per_row, tm, tpb, 1), row_spec,
                  _const_spec((D_MODEL, D_MODEL))],
        out_specs=row_spec,
        out_shape=jax.ShapeDtypeStruct((T, D_MODEL), F32),
        compiler_params=_cparams("parallel"),
        name="moba_out_proj",
    )(x, gate, a, w)


def _rel_bucket(dist):
    max_exact = N_BUCKETS // 2
    d = jnp.maximum(dist, 0)
    log_ratio = jnp.log(jnp.maximum(d, 1).astype(jnp.float32) / max_exact) / math.log(MAX_DIST / max_exact)
    large = jnp.minimum(max_exact + (log_ratio * (N_BUCKETS - max_exact)).astype(jnp.int32), N_BUCKETS - 1)
    return jnp.where(d < max_exact, d, large)


def _bias_lookup(dist, rel_bias, out_spec):
    onehot = (_rel_bucket(jnp.asarray(dist, jnp.int32))[..., None] == jnp.arange(N_BUCKETS)).astype(F32)
    return jnp.einsum(out_spec, onehot, rel_bias, precision=HIGHEST)


def _bias_tables(rel_bias, past_len, t_new):
    H, W = N_HEADS_B, BLOCK_B
    n_near = MAX_DIST // W + 1
    r = np.arange(W)[:, None]
    col = np.arange(n_near * W)[None, :]
    dist = (col // W) * W + r - (col % W)
    near = _bias_lookup(dist, rel_bias - rel_bias[N_BUCKETS - 1], "rcb,bh->hrc") * LOG2E
    ppb = W // PAGE_SIZE
    shape = (past_len // W * PAGE_SIZE * H, ppb * H * t_new)
    row = lax.broadcasted_iota(jnp.int32, shape, 0)
    col = lax.broadcasted_iota(jnp.int32, shape, 1)
    kpos = ((row // (PAGE_SIZE * H)) * ppb + col // (H * t_new)) * PAGE_SIZE + (row // H) % PAGE_SIZE
    bucket = _rel_bucket(past_len + col % t_new - kpos)
    col_bias = rel_bias[:, (np.arange(shape[1]) // t_new) % H]
    past = jnp.zeros(shape, F32)
    for b in range(N_BUCKETS):
        past = jnp.where(bucket == b, col_bias[b][None, :], past)
    past = jnp.where(row % H == (col // t_new) % H, past, MASK_VALUE)
    qr = np.arange(t_new)
    same_head = np.eye(H, dtype=bool)
    own = _bias_lookup(np.maximum(qr[None, :] - qr[:, None], 0), rel_bias, "trb,bh->thr")
    own = jnp.where(same_head[None, :, :, None], own[:, None], MASK_VALUE).reshape(t_new * H, H * t_new)
    return near, past, own


def kernel(x_prompt, x_sample, cache_k, cache_v, state_conv, page_table, c_prompt, c_sample, rel_bias, norm_mix, norm_mlp, w_mod, b_mod, w_up, w_down, a_w_in, a_v_gain, a_w_s, a_b_s, a_w_out, b_w_qkv, b_q_gain, b_k_gain, b_w_out, c_w_in, c_conv, c_w_out):
    B, S, D = x_prompt.shape
    DB, T, _ = x_sample.shape
    n_pages = page_table.shape[1]
    past_len = n_pages * PAGE_SIZE
    assert S % BLOCK_B == 0 and past_len % BLOCK_B == 0 and T == SUBLANES
    assert MAX_DIST % BLOCK_B == 0

    xp = x_prompt.reshape(B * S, D)
    xs = x_sample.reshape(DB * T, D)

    n_c = B + DB
    n_c_pad = -(-n_c // SUBLANES) * SUBLANES
    c_all = jnp.concatenate([c_prompt, c_sample, jnp.zeros((n_c_pad - n_c, D), F32)], axis=0)
    mods = _ada_all(c_all, w_mod, b_mod)

    bias_near, bias_past, bias_own = _bias_tables(rel_bias, past_len, T)

    ci = np.arange(CHUNK_A)
    tril_p = (ci[:, None] >= ci[None, :])
    tril_s = tril_p & ((ci[:, None] // T) == (ci[None, :] // T))

    k_p, v_p, k_s, v_s, conv_p, conv_s, chunkv_s = [], [], [], [], [], [], []
    for i in range(DEPTH):
        kind, j = i % N_MIXERS, i // N_MIXERS
        mp = [m.reshape(B, 1, D) for m in jnp.split(mods[i, :B], 6, axis=-1)]
        ms = [m.reshape(DB, 1, D) for m in jnp.split(mods[i, B:B + DB], 6, axis=-1)]
        g_mix = norm_mix[i].reshape(1, D)
        g_mlp = norm_mlp[i].reshape(1, D)
        if kind == 0:
            w_in = a_w_in[j].astype(BF16)
            w_out = a_w_out[j].astype(BF16)
            vg = a_v_gain[j].reshape(1, D_A)
            wmix_p = jnp.where(tril_p, a_w_s[j], 0.0).astype(BF16)
            bmix_p = jnp.repeat(jnp.transpose(a_b_s[j]), GROUP_A, axis=1)
            ws_t = jnp.tile(a_w_s[j][:, :T, :T], (1, CHUNK_A // T, CHUNK_A // T))
            wmix_s = jnp.where(tril_s, ws_t, 0.0).astype(BF16)
            bmix_s = jnp.repeat(jnp.tile(jnp.transpose(a_b_s[j][:, :T]), (CHUNK_A // T, 1)), GROUP_A, axis=1)
            (xp,) = _gmlp_layer(xp, mp[0], mp[1], mp[2], g_mix, w_in, vg, wmix_p, bmix_p, w_out,
                                False, S, False)
            xs, v_new = _gmlp_layer(xs, ms[0], ms[1], ms[2], g_mix, w_in, vg, wmix_s, bmix_s, w_out,
                                    True, T, True)
            chunkv_s.append(v_new.reshape(DB, T, D_A))
        elif kind == 1:
            w_qkv = b_w_qkv[j].astype(BF16)
            w_out = b_w_out[j].astype(BF16)
            qg = b_q_gain[j].reshape(1, HEAD_DIM_B)
            kg = b_k_gain[j].reshape(1, HEAD_DIM_B)
            qp, kp, vp, kbp, vbp, kmean_p = _qkv_layer(xp, mp[0], mp[1], g_mix, w_qkv, qg, kg, False, S, True)
            qs, ks, vs = _qkv_layer(xs, ms[0], ms[1], g_mix, w_qkv, qg, kg, True, T, False)
            xp = _moba_prompt_attention(xp, mp[2], w_out, qp, kbp, vbp, kmean_p, bias_near, B, S)
            os_ = _moba_sample_attention(
                qs.reshape(DB, T, D), ks.reshape(DB, T, N_HEADS_B, HEAD_DIM_B),
                vs.reshape(DB, T, N_HEADS_B, HEAD_DIM_B), cache_k, cache_v, j, page_table,
                bias_past, bias_own, T)
            xs = _proj_residual(xs, ms[2], os_.reshape(DB * T, D), w_out, True, T)
            k_p.append(kp.reshape(B, S, N_HEADS_B, HEAD_DIM_B))
            v_p.append(vp.reshape(B, S, N_HEADS_B, HEAD_DIM_B))
            k_s.append(ks.reshape(DB, T, N_HEADS_B, HEAD_DIM_B))
            v_s.append(vs.reshape(DB, T, N_HEADS_B, HEAD_DIM_B))
        else:
            w_in = c_w_in[j].astype(BF16)
            w_out = c_w_out[j].astype(BF16)
            st = state_conv[j]
            zrow = jnp.zeros((DB, T - 1, D_C), F32)
            p1 = jnp.concatenate([st[:, 1:2], zrow], axis=1).reshape(DB * T, D_C)
            p2 = jnp.concatenate([st, zrow[:, 1:]], axis=1).reshape(DB * T, D_C)
            xp, tail_p = _conv_layer(xp, mp[0], mp[1], mp[2], g_mix, w_in, c_conv[j], w_out, False, S)
            xs, xin_s = _conv_layer(xs, ms[0], ms[1], ms[2], g_mix, w_in, c_conv[j], w_out, True, T,
                                    fills=(p1, p2))
            conv_p.append(tail_p[:, SUBLANES - (CONV_W - 1):, :])
            conv_s.append(xin_s.reshape(DB, T, D_C)[:, T - (CONV_W - 1):, :])
        wu = w_up[i].astype(BF16)
        wd = w_down[i].astype(BF16)
        xp = _mlp_layer(xp, mp[3], mp[4], mp[5], g_mlp, wu, wd, False, S)
        xs = _mlp_layer(xs, ms[3], ms[4], ms[5], g_mlp, wu, wd, True, T)
    return (xp.reshape(B, S, D), xs.reshape(DB, T, D), jnp.stack(k_p), jnp.stack(v_p), jnp.stack(k_s),
            jnp.stack(v_s), jnp.stack(conv_p), jnp.stack(conv_s), jnp.stack(chunkv_s))
```

```python
import functools
import math

import jax
import jax.numpy as jnp
import numpy as np
from jax import lax
from jax.experimental import pallas as pl
from jax.experimental.pallas import tpu as pltpu

D_MODEL = 1024
DEPTH = 4
N_MIXERS = 3
D_A = 2 * D_MODEL
N_GROUPS_A = 8
GROUP_A = D_A // N_GROUPS_A
CHUNK_A = 128
HEAD_DIM_B = 128
N_HEADS_B = D_MODEL // HEAD_DIM_B
BLOCK_B = 256
TOPK_B = 3
N_BUCKETS = 32
MAX_DIST = 1024
D_C = D_MODEL
CONV_W = 3
D_FF = 4 * D_MODEL
EPS = 1e-6
MASK_VALUE = -1e30
LOG2E = math.log2(math.e)
PAGE_SIZE = 128

SUBLANES = 8
LANES = 128
VMEM_LIMIT_BYTES = 56 * 1024 * 1024

TM_MLP = 1024
TF_MLP = 2048
TM_MIXER = 512
TN_MOD = 1536
PAGES_PER_STEP = 16

F32 = jnp.float32
BF16 = jnp.bfloat16
HIGHEST = lax.Precision.HIGHEST


def _cparams(*sem):
    return pltpu.CompilerParams(dimension_semantics=sem, vmem_limit_bytes=VMEM_LIMIT_BYTES)


def _rows(ref):
    v = ref[...]
    n, _, d = v.shape
    if n == 1:
        return v.reshape(1, d)
    return jnp.broadcast_to(v, (n, SUBLANES, d)).reshape(n * SUBLANES, d)


def _modulate(x, g, shift, scale):
    y = x * lax.rsqrt(jnp.mean(x * x, axis=-1, keepdims=True) + EPS)
    return (y * g) * (1.0 + scale) + shift


def _gelu_tanh(x):
    c = math.sqrt(2.0 / math.pi)
    return x * (0.5 * (1.0 + jnp.tanh(c * (x + 0.044715 * (x * x * x)))))


def _dot(a, b):
    return jnp.dot(a, b, preferred_element_type=F32)


def _dot_nt(a, b, **kw):
    return lax.dot_general(a, b, (((1,), (1,)), ((), ())), preferred_element_type=F32, **kw)


def _dot_tn(a, b):
    return lax.dot_general(a, b, (((0,), (0,)), ((), ())), preferred_element_type=F32)


def _mod_spec(per_row, tm, tiles_per_batch, ngrid):
    if per_row:
        if ngrid == 1:
            return pl.BlockSpec((tm // SUBLANES, 1, D_MODEL), lambda i: (i, 0, 0))
        return pl.BlockSpec((tm // SUBLANES, 1, D_MODEL), lambda i, f: (i, 0, 0))
    if ngrid == 1:
        return pl.BlockSpec((1, 1, D_MODEL), lambda i: (i // tiles_per_batch, 0, 0))
    return pl.BlockSpec((1, 1, D_MODEL), lambda i, f: (i // tiles_per_batch, 0, 0))


def _const_spec(shape):
    nd = len(shape)
    return pl.BlockSpec(shape, lambda i: (0,) * nd, pipeline_mode=pl.Buffered(1))


def _mod_kernel(c_ref, w_ref, b_ref, o_ref):
    c = c_ref[...]
    sc = (c * jax.nn.sigmoid(c)).astype(BF16)
    o_ref[0] = _dot(sc, w_ref[0].astype(BF16)) + b_ref[0]


def _ada_all(c_all, w_mod, b_mod):
    nrow = c_all.shape[0]
    tn = TN_MOD
    nn = (6 * D_MODEL) // tn
    return pl.pallas_call(
        _mod_kernel,
        grid=(DEPTH, nn),
        in_specs=[
            pl.BlockSpec((nrow, D_MODEL), lambda l, n: (0, 0)),
            pl.BlockSpec((1, D_MODEL, tn), lambda l, n: (l, 0, n)),
            pl.BlockSpec((1, 1, tn), lambda l, n: (l, 0, n)),
        ],
        out_specs=pl.BlockSpec((1, nrow, tn), lambda l, n: (l, 0, n)),
        out_shape=jax.ShapeDtypeStruct((DEPTH, nrow, 6 * D_MODEL), F32),
        compiler_params=_cparams("arbitrary", "arbitrary"),
        name="ada_mod",
    )(c_all, w_mod, b_mod.reshape(DEPTH, 1, 6 * D_MODEL))


def _mlp_kernel(x_ref, sh_ref, sc_ref, gt_ref, g_ref, wu_ref, wd_ref, o_ref):
    x = x_ref[...]
    h = _modulate(x, g_ref[...], _rows(sh_ref), _rows(sc_ref)).astype(BF16)
    acc = None
    for c in range(D_FF // TF_MLP):
        cols = slice(c * TF_MLP, (c + 1) * TF_MLP)
        a = jnp.maximum(_dot(h, wu_ref[:, cols]), 0.0)
        y = _dot((a * a).astype(BF16), wd_ref[cols, :])
        acc = y if acc is None else acc + y
    o_ref[...] = x + _rows(gt_ref) * acc


def _mlp_layer(x, shift, scale, gate, g, w_up, w_down, per_row, rows_per_batch):
    T = x.shape[0]
    tm = TM_MLP
    tpb = rows_per_batch // tm if not per_row else 1
    ms = _mod_spec(per_row, tm, tpb, 1)
    return pl.pallas_call(
        _mlp_kernel,
        grid=(T // tm,),
        in_specs=[
            pl.BlockSpec((tm, D_MODEL), lambda i: (i, 0)),
            ms, ms, ms,
            _const_spec((1, D_MODEL)),
            _const_spec((D_MODEL, D_FF)),
            _const_spec((D_FF, D_MODEL)),
        ],
        out_specs=pl.BlockSpec((tm, D_MODEL), lambda i: (i, 0)),
        out_shape=jax.ShapeDtypeStruct((T, D_MODEL), F32),
        compiler_params=_cparams("parallel"),
        name="mlp",
    )(x, shift, scale, gate, g, w_up, w_down)


def _gmlp_kernel(x_ref, sh_ref, sc_ref, gt_ref, g_ref, win_ref, vg_ref, wmix_ref, bmix_ref, wout_ref,
                 *out_refs, tm, emit_v):
    o_ref = out_refs[0]
    gs_scr = out_refs[-1]
    x = x_ref[...]
    h = _modulate(x, g_ref[...], _rows(sh_ref), _rows(sc_ref)).astype(BF16)
    z = _gelu_tanh(_dot(h, win_ref[...]))
    u = z[:, :D_A]
    v = z[:, D_A:]
    v = v * lax.rsqrt(jnp.mean(v * v, axis=-1, keepdims=True) + EPS) * vg_ref[...]
    if emit_v:
        out_refs[1][...] = v
    vb = v.astype(BF16)
    for c in range(tm // CHUNK_A):
        r0 = c * CHUNK_A
        for gi in range(N_GROUPS_A):
            c0 = gi * GROUP_A
            s = _dot(wmix_ref[gi], vb[r0:r0 + CHUNK_A, c0:c0 + GROUP_A])
            s = s + bmix_ref[:, c0:c0 + GROUP_A]
            gs_scr[r0:r0 + CHUNK_A, c0:c0 + GROUP_A] = (u[r0:r0 + CHUNK_A, c0:c0 + GROUP_A] * s).astype(BF16)
    y = _dot(gs_scr[...], wout_ref[...])
    o_ref[...] = x + _rows(gt_ref) * y


def _gmlp_layer(x, shift, scale, gate, g, w_in, v_gain, w_mix, b_mix, w_out, per_row, rows_per_batch,
                emit_v):
    T = x.shape[0]
    tm = TM_MIXER
    tpb = rows_per_batch // tm if not per_row else 1
    ms = _mod_spec(per_row, tm, tpb, 1)
    out_shape = [jax.ShapeDtypeStruct((T, D_MODEL), F32)]
    out_specs = [pl.BlockSpec((tm, D_MODEL), lambda i: (i, 0))]
    if emit_v:
        out_shape.append(jax.ShapeDtypeStruct((T, D_A), F32))
        out_specs.append(pl.BlockSpec((tm, D_A), lambda i: (i, 0)))
    return pl.pallas_call(
        functools.partial(_gmlp_kernel, tm=tm, emit_v=emit_v),
        grid=(T // tm,),
        in_specs=[
            pl.BlockSpec((tm, D_MODEL), lambda i: (i, 0)),
            ms, ms, ms,
            _const_spec((1, D_MODEL)),
            _const_spec((D_MODEL, 2 * D_A)),
            _const_spec((1, D_A)),
            _const_spec((N_GROUPS_A, CHUNK_A, CHUNK_A)),
            _const_spec((CHUNK_A, D_A)),
            _const_spec((D_A, D_MODEL)),
        ],
        out_specs=out_specs,
        out_shape=out_shape,
        scratch_shapes=[pltpu.VMEM((tm, D_A), BF16)],
        compiler_params=_cparams("parallel"),
        name="gmlp_mixer",
    )(x, shift, scale, gate, g, w_in, v_gain, w_mix, b_mix, w_out)


def _conv_kernel(*refs, tm, per_row, tiles_per_batch):
    if per_row:
        (x_ref, sh_ref, sc_ref, gt_ref, g_ref, win_ref, cw_ref, wout_ref, p1_ref, p2_ref,
         o_ref, xin_ref) = refs
    else:
        (x_ref, sh_ref, sc_ref, gt_ref, g_ref, win_ref, cw_ref, wout_ref,
         o_ref, tail_ref, carry_scr) = refs
    x = x_ref[...]
    h = _modulate(x, g_ref[...], _rows(sh_ref), _rows(sc_ref)).astype(BF16)
    bcx = _dot(h, win_ref[...])
    gate_out = bcx[:, :D_C]
    xin = bcx[:, D_C:2 * D_C] * bcx[:, 2 * D_C:]
    row = lax.broadcasted_iota(jnp.int32, (tm, D_C), 0)
    roll1 = pltpu.roll(xin, 1, 0)
    roll2 = pltpu.roll(xin, 2, 0)
    if per_row:
        pos = row % SUBLANES
        prev1 = jnp.where(pos == 0, p1_ref[...], roll1)
        prev2 = jnp.where(pos < 2, p2_ref[...], roll2)
        xin_ref[...] = xin
    else:
        @pl.when(pl.program_id(0) % tiles_per_batch == 0)
        def _():
            carry_scr[...] = jnp.zeros_like(carry_scr)
        c6 = carry_scr[SUBLANES - 2:SUBLANES - 1, :]
        c7 = carry_scr[SUBLANES - 1:SUBLANES, :]
        prev1 = jnp.where(row == 0, c7, roll1)
        prev2 = jnp.where(row == 0, c6, jnp.where(row == 1, c7, roll2))
        tail = xin[tm - SUBLANES:, :]
        carry_scr[...] = tail
        tail_ref[0] = tail
    y = cw_ref[0:1, :] * prev2 + cw_ref[1:2, :] * prev1 + cw_ref[2:3, :] * xin
    out = _dot((gate_out * y).astype(BF16), wout_ref[...])
    o_ref[...] = x + _rows(gt_ref) * out


def _conv_layer(x, shift, scale, gate, g, w_in, conv_w, w_out, per_row, rows_per_batch, fills=None):
    T = x.shape[0]
    tm = TM_MIXER
    tpb = rows_per_batch // tm if not per_row else 1
    ms = _mod_spec(per_row, tm, tpb, 1)
    in_specs = [
        pl.BlockSpec((tm, D_MODEL), lambda i: (i, 0)),
        ms, ms, ms,
        _const_spec((1, D_MODEL)),
        _const_spec((D_MODEL, 3 * D_C)),
        _const_spec((CONV_W, D_C)),
        _const_spec((D_C, D_MODEL)),
    ]
    args = [x, shift, scale, gate, g, w_in, conv_w, w_out]
    out_shape = [jax.ShapeDtypeStruct((T, D_MODEL), F32)]
    out_specs = [pl.BlockSpec((tm, D_MODEL), lambda i: (i, 0))]
    scratch = []
    if per_row:
        in_specs += [pl.BlockSpec((tm, D_C), lambda i: (i, 0))] * 2
        args += list(fills)
        out_shape.append(jax.ShapeDtypeStruct((T, D_C), F32))
        out_specs.append(pl.BlockSpec((tm, D_C), lambda i: (i, 0)))
        sem = "parallel"
    else:
        nb = T // rows_per_batch
        out_shape.append(jax.ShapeDtypeStruct((nb, SUBLANES, D_C), F32))
        out_specs.append(pl.BlockSpec((1, SUBLANES, D_C), lambda i: (i // tpb, 0, 0)))
        scratch.append(pltpu.VMEM((SUBLANES, D_C), F32))
        sem = "arbitrary"
    return pl.pallas_call(
        functools.partial(_conv_kernel, tm=tm, per_row=per_row, tiles_per_batch=tpb),
        grid=(T // tm,),
        in_specs=in_specs,
        out_specs=out_specs,
        out_shape=out_shape,
        scratch_shapes=scratch,
        compiler_params=_cparams(sem),
        name="conv_mixer",
    )(*args)


def _qkv_kernel(x_ref, sh_ref, sc_ref, g_ref, w_ref, qg_ref, kg_ref, q_ref, k_ref, v_ref, *extra,
                tm, emit_attn_inputs):
    h = _modulate(x_ref[...], g_ref[...], _rows(sh_ref), _rows(sc_ref)).astype(BF16)
    qkv = _dot(h, w_ref[...])
    hd = HEAD_DIM_B
    for hh in range(N_HEADS_B):
        cols = slice(hh * hd, (hh + 1) * hd)
        q = qkv[:, hh * hd:(hh + 1) * hd]
        k = qkv[:, D_MODEL + hh * hd:D_MODEL + (hh + 1) * hd]
        q_ref[:, cols] = q * lax.rsqrt(jnp.mean(q * q, axis=-1, keepdims=True) + EPS) * qg_ref[...]
        kn = k * lax.rsqrt(jnp.mean(k * k, axis=-1, keepdims=True) + EPS) * kg_ref[...]
        k_ref[:, cols] = kn
        if emit_attn_inputs:
            kb_ref, vb_ref, km_ref = extra
            kb_ref[:, cols] = kn.astype(BF16)
            for c in range(tm // BLOCK_B):
                km_ref[c, :, cols] = jnp.sum(kn[c * BLOCK_B:(c + 1) * BLOCK_B], axis=0, keepdims=True) * (
                    1.0 / BLOCK_B)
    v = qkv[:, 2 * D_MODEL:]
    v_ref[...] = v
    if emit_attn_inputs:
        extra[1][...] = v.astype(BF16)


def _qkv_layer(x, shift, scale, g, w_qkv, q_gain, k_gain, per_row, rows_per_batch, emit_attn_inputs):
    T = x.shape[0]
    tm = TM_MIXER
    tpb = rows_per_batch // tm if not per_row else 1
    ms = _mod_spec(per_row, tm, tpb, 1)
    row_spec = pl.BlockSpec((tm, D_MODEL), lambda i: (i, 0))
    out_specs = [row_spec, row_spec, row_spec]
    out_shape = [jax.ShapeDtypeStruct((T, D_MODEL), F32)] * 3
    if emit_attn_inputs:
        out_specs += [row_spec, row_spec,
                      pl.BlockSpec((tm // BLOCK_B, 1, D_MODEL), lambda i: (i, 0, 0))]
        out_shape += [jax.ShapeDtypeStruct((T, D_MODEL), BF16)] * 2
        out_shape += [jax.ShapeDtypeStruct((T // BLOCK_B, 1, D_MODEL), F32)]
    return pl.pallas_call(
        functools.partial(_qkv_kernel, tm=tm, emit_attn_inputs=emit_attn_inputs),
        grid=(T // tm,),
        in_specs=[row_spec, ms, ms,
                  _const_spec((1, D_MODEL)),
                  _const_spec((D_MODEL, 3 * D_MODEL)),
                  _const_spec((1, HEAD_DIM_B)),
                  _const_spec((1, HEAD_DIM_B))],
        out_specs=out_specs,
        out_shape=out_shape,
        compiler_params=_cparams("parallel"),
        name="moba_qkv",
    )(x, shift, scale, g, w_qkv, q_gain, k_gain)


def _topk_mask_t(gate_t, cands, blk_of_row, elig):
    gm = jnp.where(elig, gate_t, MASK_VALUE)
    cnt = jnp.zeros(gate_t.shape, F32)
    for r, cb in cands:
        row = gm[r:r + 1, :]
        beats = jnp.where(row > gm, 1.0, jnp.where(row == gm, jnp.where(cb < blk_of_row, 1.0, 0.0), 0.0))
        cnt = cnt + beats
    return jnp.where(elig, jnp.where(cnt < TOPK_B, 1.0, 0.0), 0.0)


def _moba_prompt_kernel(x_ref, gt_ref, wout_ref, q_ref, kb_ref, vb_ref, km_ref, bias_ref, o_ref, qaug_scr, *,
                        nblk, nh):
    qb = pl.program_id(1)
    hd = HEAD_DIM_B
    W = BLOCK_B
    n_near = bias_ref.shape[2] // W
    dead_lane = LANES - 1
    lane = lax.broadcasted_iota(jnp.int32, (W, LANES), 1)
    blk = lax.broadcasted_iota(jnp.int32, (nblk, W), 0)
    r_i = lax.broadcasted_iota(jnp.int32, (W, W), 0)
    c_i = lax.broadcasted_iota(jnp.int32, (W, W), 1)
    causal = r_i >= c_i

    def onehot(b, valid):
        return jnp.where(lane == jnp.where(valid, b, dead_lane), 1.0, 0.0).astype(BF16)

    def rows_of(b):
        return pl.ds(pl.multiple_of(b * W, W), W)

    near_blk = [jnp.maximum(qb - d, 0) for d in range(n_near)]
    oh_near = jnp.concatenate([onehot(qb - d, qb - d >= 0) for d in range(n_near)], axis=0)
    ones_near = jnp.ones((n_near * W, hd), BF16)
    heads = [slice(h * hd, (h + 1) * hd) for h in range(nh)]
    gates = [_dot_nt(km_ref[:, 0, cols].astype(BF16), q_ref[:, cols].astype(BF16)) for cols in heads]
    scores = []
    for h, cols in enumerate(heads):
        sel_t = _topk_mask_t(gates[h], [(j, j) for j in range(nblk)], blk, blk < qb)
        neg_t = jnp.where(blk == qb, 0.0, jnp.where(sel_t > 0.5, 0.0, MASK_VALUE))
        neg = jnp.concatenate([neg_t, jnp.zeros((LANES - nblk - 1, W), F32),
                               jnp.full((1, W), MASK_VALUE, F32)], axis=0).T
        q_aug = jnp.concatenate([(q_ref[:, cols] * (LOG2E / math.sqrt(hd))).astype(BF16), neg.astype(BF16)],
                                axis=1)
        qaug_scr[h] = q_aug
        k_near = jnp.concatenate([kb_ref[rows_of(b), cols] for b in near_blk], axis=0)
        scores.append(_dot_nt(q_aug, jnp.concatenate([k_near, oh_near], axis=1)))
    carry0 = []
    for h, cols in enumerate(heads):
        s = scores[h] + bias_ref[h]
        s = jnp.concatenate([jnp.where(causal, s[:, :W], MASK_VALUE), s[:, W:]], axis=1)
        m0 = jnp.max(s, axis=-1, keepdims=True)
        p = jnp.exp2(s - m0)
        v_near = jnp.concatenate([vb_ref[rows_of(b), cols] for b in near_blk], axis=0)
        carry0 += [m0, _dot(p.astype(BF16), jnp.concatenate([v_near, ones_near], axis=1))]

    n_far = jnp.maximum(qb - (n_near - 1), 0)
    ones_pair = jnp.ones((2 * W, hd), BF16)

    def body(i, carry):
        b0 = 2 * i
        b1 = b0 + 1
        oh = jnp.concatenate([onehot(b0, True), onehot(b1, b1 < n_far)], axis=0)
        pair_scores = []
        for h, cols in enumerate(heads):
            k_pair = jnp.concatenate([kb_ref[rows_of(b0), cols], kb_ref[rows_of(b1), cols]], axis=0)
            pair_scores.append(_dot_nt(qaug_scr[h], jnp.concatenate([k_pair, oh], axis=1)))
        out = []
        for h, cols in enumerate(heads):
            m, acc = carry[2 * h], carry[2 * h + 1]
            s = pair_scores[h]
            v_pair = jnp.concatenate([vb_ref[rows_of(b0), cols], vb_ref[rows_of(b1), cols]], axis=0)
            m_new = jnp.maximum(m, jnp.max(s, axis=-1, keepdims=True))
            p = jnp.exp2(s - m_new)
            acc = jnp.exp2(m - m_new) * acc + _dot(p.astype(BF16), jnp.concatenate([v_pair, ones_pair], axis=1))
            out += [m_new, acc]
        return tuple(out)

    carry = lax.fori_loop(0, (n_far + 1) // 2, body, tuple(carry0))
    o = jnp.concatenate([(carry[2 * h + 1][:, :hd] / carry[2 * h + 1][:, hd:]).astype(BF16) for h in range(nh)],
                        axis=1)
    o_ref[...] = x_ref[...] + _rows(gt_ref) * _dot(o, wout_ref[...])


def _moba_prompt_attention(x, gate, w_out, q, kb, vb, kmean, bias_near, batch, seq):
    nq = seq // BLOCK_B
    hd = HEAD_DIM_B
    nh = N_HEADS_B
    one = pl.Buffered(1)
    row_spec = pl.BlockSpec((BLOCK_B, D_MODEL), lambda b, i: (b * nq + i, 0))
    return pl.pallas_call(
        functools.partial(_moba_prompt_kernel, nblk=nq, nh=nh),
        grid=(batch, nq),
        in_specs=[
            row_spec,
            pl.BlockSpec((1, 1, D_MODEL), lambda b, i: (b, 0, 0)),
            pl.BlockSpec((D_MODEL, D_MODEL), lambda b, i: (0, 0), pipeline_mode=one),
            row_spec,
            pl.BlockSpec((seq, nh * hd), lambda b, i: (b, 0), pipeline_mode=one),
            pl.BlockSpec((seq, nh * hd), lambda b, i: (b, 0), pipeline_mode=one),
            pl.BlockSpec((nq, 1, nh * hd), lambda b, i: (b, 0, 0)),
            pl.BlockSpec(bias_near.shape, lambda b, i: (0, 0, 0), pipeline_mode=one),
        ],
        out_specs=row_spec,
        out_shape=jax.ShapeDtypeStruct((batch * seq, D_MODEL), F32),
        scratch_shapes=[pltpu.VMEM((nh, BLOCK_B, 2 * hd), BF16)],
        compiler_params=_cparams("parallel", "arbitrary"),
        name="moba_prompt_attn",
    )(x, gate, w_out, q, kb, vb, kmean, bias_near)


def _moba_sample_kernel(pt_ref, q_ref, kn_ref, vn_ref, *refs, pps, n_pages, t_new):
    kc_refs = refs[:pps]
    vc_refs = refs[pps:2 * pps]
    bias_ref, bias_own_ref = refs[2 * pps:2 * pps + 2]
    o_ref = refs[2 * pps + 2]
    qs_scr, qs2_scr, qg_scr, m_scr, l_scr, o_scr, ksum_scr = refs[2 * pps + 3:]
    del pt_ref
    s_id = pl.program_id(1)
    H, hd = N_HEADS_B, HEAD_DIM_B
    nrow = H * t_new
    ppb = BLOCK_B // PAGE_SIZE
    assert ppb == 2
    n_blk = n_pages // ppb
    page_rows = PAGE_SIZE * H

    @pl.when(s_id == 0)
    def _():
        q8 = q_ref[0]
        qall = jnp.concatenate([q8[:, h * hd:(h + 1) * hd] for h in range(H)], axis=0)
        qg_scr[...] = qall.astype(BF16)
        qs = (qall * (1.0 / math.sqrt(hd))).astype(BF16)
        qs_scr[...] = qs
        zero = jnp.zeros((nrow, hd), BF16)
        qs2_scr[...] = jnp.concatenate([jnp.concatenate([qs, zero], axis=1),
                                        jnp.concatenate([zero, qs], axis=1)], axis=0)

    own_head = (lax.broadcasted_iota(jnp.int32, (H, nrow), 0)
                == lax.broadcasted_iota(jnp.int32, (H, nrow), 1) // t_new)
    own_head2 = (lax.broadcasted_iota(jnp.int32, (H, ppb * nrow), 0)
                 == (lax.broadcasted_iota(jnp.int32, (H, ppb * nrow), 1) % nrow) // t_new)

    def block_scores(k_pages, bias):
        k2 = jnp.concatenate(k_pages, axis=1).astype(BF16)
        return _dot_nt(k2, qs2_scr[...]) + bias

    def block_attn(g, v_pages):
        v2 = jnp.concatenate(v_pages, axis=1).astype(BF16)
        g3 = g.reshape(PAGE_SIZE, H, ppb * nrow)
        m_t = jnp.max(g3, axis=0)
        m_t = jnp.maximum(m_t, pltpu.roll(m_t, nrow, 1))
        e3 = jnp.exp(g3 - jnp.where(own_head2, m_t, 0.0)[None])
        l_t = jnp.sum(e3, axis=0)
        l_t = l_t + pltpu.roll(l_t, nrow, 1)
        o2 = _dot_tn(e3.reshape(PAGE_SIZE * H, ppb * nrow).astype(BF16), v2)
        return m_t, l_t, o2[:nrow, :hd] + o2[nrow:, hd:]

    def partial_attn(k_rows, v_rows, bias, valid):
        n = k_rows.shape[0]
        g = _dot_nt(k_rows.astype(BF16), qs_scr[...]) + bias
        if valid is not None:
            g = jnp.where(valid, g, MASK_VALUE)
        g3 = g.reshape(n // H, H, nrow)
        m_t = jnp.max(g3, axis=0)
        e3 = jnp.exp(g3 - jnp.where(own_head, m_t, 0.0)[None])
        l_t = jnp.sum(e3, axis=0)
        o = _dot_tn(e3.reshape(n, nrow).astype(BF16), v_rows.astype(BF16))
        return m_t, l_t, o

    scores = []
    for i in range(pps // ppb):
        blk = s_id * (pps // ppb) + i
        k_pages = [kc_refs[i * ppb + u][0, 0].reshape(page_rows, hd) for u in range(ppb)]
        ksum_scr[blk] = jnp.sum((k_pages[0] + k_pages[1]).reshape(PAGE_SIZE, H, hd), axis=0)
        b0 = pl.multiple_of(blk * page_rows, page_rows)
        scores.append((blk, block_scores(k_pages, bias_ref[pl.ds(b0, page_rows), :])))
    for i, (blk, g) in enumerate(scores):
        v_pages = [vc_refs[i * ppb + u][0, 0].reshape(page_rows, hd) for u in range(ppb)]
        m_t, l_t, o = block_attn(g, v_pages)
        m_scr[blk] = m_t
        l_scr[blk] = l_t
        o_scr[blk] = o

    @pl.when(s_id == pl.num_programs(1) - 1)
    def _():
        def head_diag(t):
            return jnp.sum(jnp.where(own_head[None], t[:, :, :nrow], 0.0), axis=1)

        key_i = lax.broadcasted_iota(jnp.int32, (t_new * H, nrow), 0) // H
        qry_i = lax.broadcasted_iota(jnp.int32, (t_new * H, nrow), 1) % t_new
        m_ot, l_ot, o_o = partial_attn(kn_ref[0].reshape(t_new * H, hd), vn_ref[0].reshape(t_new * H, hd),
                                       bias_own_ref[...], key_i <= qry_i)
        m_o = head_diag(m_ot[None])
        l_o = head_diag(l_ot[None])

        kmean = (ksum_scr[...] * (1.0 / BLOCK_B)).reshape(n_blk * H, hd)
        gate = _dot_nt(kmean.astype(BF16), qg_scr[...])
        gate_t = head_diag(gate.reshape(n_blk, H, nrow))
        bi = lax.broadcasted_iota(jnp.int32, (n_blk, nrow), 0)
        sel = _topk_mask_t(gate_t, [(j, j) for j in range(n_blk)], bi, bi >= 0)

        m_all = head_diag(m_scr[...])
        l_all = head_diag(l_scr[...])
        m_fin = jnp.maximum(jnp.max(jnp.where(sel > 0.5, m_all, MASK_VALUE), axis=0, keepdims=True), m_o)
        w = jnp.where(sel > 0.5, jnp.exp(m_all - m_fin), 0.0)
        w_o = jnp.exp(m_o - m_fin)
        l_fin = jnp.sum(w * l_all, axis=0, keepdims=True) + w_o * l_o
        stack = jnp.concatenate([w, w_o, l_fin], axis=0)
        stack = jnp.concatenate([stack, jnp.zeros((stack.shape[0], LANES - nrow), F32)], axis=1)
        stack = jnp.concatenate([stack, jnp.zeros((LANES - stack.shape[0], LANES), F32)], axis=0)
        st = stack.T
        acc = st[:nrow, n_blk:n_blk + 1] * o_o
        for pp in range(n_blk):
            acc = acc + st[:nrow, pp:pp + 1] * o_scr[pp]
        out = acc / st[:nrow, n_blk + 1:n_blk + 2]
        for h in range(H):
            o_ref[0, :, h * hd:(h + 1) * hd] = out[h * t_new:(h + 1) * t_new, :]


def _moba_sample_attention(q, k_new, v_new, cache_k, cache_v, layer, page_table, bias_past, bias_own, t_new):
    nb, n_pages = page_table.shape
    pps = PAGES_PER_STEP
    n_steps = n_pages // pps
    H, hd = N_HEADS_B, HEAD_DIM_B
    nrow = H * t_new
    n_blk = n_pages * PAGE_SIZE // BLOCK_B

    def page_spec(i):
        return pl.BlockSpec((1, 1, PAGE_SIZE, H, hd),
                            lambda b, s, pt: (layer, pt[b * n_pages + s * pps + i], 0, 0, 0))

    row_spec = pl.BlockSpec((1, t_new, H * hd), lambda b, s, pt: (b, 0, 0))
    own_spec = pl.BlockSpec((1, t_new, H, hd), lambda b, s, pt: (b, 0, 0, 0))
    one = pl.Buffered(1)
    grid_spec = pltpu.PrefetchScalarGridSpec(
        num_scalar_prefetch=1,
        grid=(nb, n_steps),
        in_specs=[row_spec, own_spec, own_spec]
                 + [page_spec(i) for i in range(pps)] + [page_spec(i) for i in range(pps)]
                 + [pl.BlockSpec(bias_past.shape, lambda b, s, pt: (0, 0), pipeline_mode=one),
                    pl.BlockSpec(bias_own.shape, lambda b, s, pt: (0, 0), pipeline_mode=one)],
        out_specs=row_spec,
        scratch_shapes=[
            pltpu.VMEM((nrow, hd), BF16),
            pltpu.VMEM((2 * nrow, 2 * hd), BF16),
            pltpu.VMEM((nrow, hd), BF16),
            pltpu.VMEM((n_blk, H, 2 * nrow), F32),
            pltpu.VMEM((n_blk, H, 2 * nrow), F32),
            pltpu.VMEM((n_blk, nrow, hd), F32),
            pltpu.VMEM((n_blk, H, hd), F32),
        ],
    )
    return pl.pallas_call(
        functools.partial(_moba_sample_kernel, pps=pps, n_pages=n_pages, t_new=t_new),
        grid_spec=grid_spec,
        out_shape=jax.ShapeDtypeStruct((nb, t_new, H * hd), F32),
        compiler_params=_cparams("parallel", "arbitrary"),
        name="moba_sample_attn",
    )(page_table.reshape(-1), q, k_new, v_new, *([cache_k] * pps), *([cache_v] * pps), bias_past, bias_own)


def _proj_kernel(x_ref, gt_ref, a_ref, w_ref, o_ref):
    o_ref[...] = x_ref[...] + _rows(gt_ref) * _dot(a_ref[...].astype(BF16), w_ref[...])


def _proj_residual(x, gate, a, w, per_row, rows_per_batch):
    T = x.shape[0]
    tm = TM_MIXER
    tpb = rows_per_batch // tm if not per_row else 1
    row_spec = pl.BlockSpec((tm, D_MODEL), lambda i: (i, 0))
    return pl.pallas_call(
        _proj_kernel,
        grid=(T // tm,),
        in_specs=[row_spec, _mod_spec(per_row, tm, tpb, 1), row_spec,
                  _const_spec((D_MODEL, D_MODEL))],
        out_specs=row_spec,
        out_shape=jax.ShapeDtypeStruct((T, D_MODEL), F32),
        compiler_params=_cparams("parallel"),
        name="moba_out_proj",
    )(x, gate, a, w)


def _rel_bucket(dist):
    max_exact = N_BUCKETS // 2
    d = jnp.maximum(dist, 0)
    log_ratio = jnp.log(jnp.maximum(d, 1).astype(jnp.float32) / max_exact) / math.log(MAX_DIST / max_exact)
    large = jnp.minimum(max_exact + (log_ratio * (N_BUCKETS - max_exact)).astype(jnp.int32), N_BUCKETS - 1)
    return jnp.where(d < max_exact, d, large)


def _bias_lookup(dist, rel_bias, out_spec):
    onehot = (_rel_bucket(jnp.asarray(dist, jnp.int32))[..., None] == jnp.arange(N_BUCKETS)).astype(F32)
    return jnp.einsum(out_spec, onehot, rel_bias, precision=HIGHEST)


def _bias_tables(rel_bias, past_len, t_new):
    H, W = N_HEADS_B, BLOCK_B
    n_near = MAX_DIST // W + 1
    r = np.arange(W)[:, None]
    col = np.arange(n_near * W)[None, :]
    dist = (col // W) * W + r - (col % W)
    near = _bias_lookup(dist, rel_bias - rel_bias[N_BUCKETS - 1], "rcb,bh->hrc") * LOG2E
    ppb = W // PAGE_SIZE
    shape = (past_len // W * PAGE_SIZE * H, ppb * H * t_new)
    row = lax.broadcasted_iota(jnp.int32, shape, 0)
    col = lax.broadcasted_iota(jnp.int32, shape, 1)
    kpos = ((row // (PAGE_SIZE * H)) * ppb + col // (H * t_new)) * PAGE_SIZE + (row // H) % PAGE_SIZE
    bucket = _rel_bucket(past_len + col % t_new - kpos)
    col_bias = rel_bias[:, (np.arange(shape[1]) // t_new) % H]
    past = jnp.zeros(shape, F32)
    for b in range(N_BUCKETS):
        past = jnp.where(bucket == b, col_bias[b][None, :], past)
    past = jnp.where(row % H == (col // t_new) % H, past, MASK_VALUE)
    qr = np.arange(t_new)
    same_head = np.eye(H, dtype=bool)
    own = _bias_lookup(np.maximum(qr[None, :] - qr[:, None], 0), rel_bias, "trb,bh->thr")
    own = jnp.where(same_head[None, :, :, None], own[:, None], MASK_VALUE).reshape(t_new * H, H * t_new)
    return near, past, own


def kernel(x_prompt, x_sample, cache_k, cache_v, state_conv, page_table, c_prompt, c_sample, rel_bias, norm_mix, norm_mlp, w_mod, b_mod, w_up, w_down, a_w_in, a_v_gain, a_w_s, a_b_s, a_w_out, b_w_qkv, b_q_gain, b_k_gain, b_w_out, c_w_in, c_conv, c_w_out):
    B, S, D = x_prompt.shape
    DB, T, _ = x_sample.shape
    n_pages = page_table.shape[1]
    past_len = n_pages * PAGE_SIZE
    assert S % BLOCK_B == 0 and past_len % BLOCK_B == 0 and T == SUBLANES
    assert MAX_DIST % BLOCK_B == 0

    xp = x_prompt.reshape(B * S, D)
    xs = x_sample.reshape(DB * T, D)

    n_c = B + DB
    n_c_pad = -(-n_c // SUBLANES) * SUBLANES
    c_all = jnp.concatenate([c_prompt, c_sample, jnp.zeros((n_c_pad - n_c, D), F32)], axis=0)
    mods = _ada_all(c_all, w_mod, b_mod)

    bias_near, bias_past, bias_own = _bias_tables(rel_bias, past_len, T)

    ci = np.arange(CHUNK_A)
    tril_p = (ci[:, None] >= ci[None, :])
    tril_s = tril_p & ((ci[:, None] // T) == (ci[None, :] // T))

    k_p, v_p, k_s, v_s, conv_p, conv_s, chunkv_s = [], [], [], [], [], [], []
    for i in range(DEPTH):
        kind, j = i % N_MIXERS, i // N_MIXERS
        mp = [m.reshape(B, 1, D) for m in jnp.split(mods[i, :B], 6, axis=-1)]
        ms = [m.reshape(DB, 1, D) for m in jnp.split(mods[i, B:B + DB], 6, axis=-1)]
        g_mix = norm_mix[i].reshape(1, D)
        g_mlp = norm_mlp[i].reshape(1, D)
        if kind == 0:
            w_in = a_w_in[j].astype(BF16)
            w_out = a_w_out[j].astype(BF16)
            vg = a_v_gain[j].reshape(1, D_A)
            wmix_p = jnp.where(tril_p, a_w_s[j], 0.0).astype(BF16)
            bmix_p = jnp.repeat(jnp.transpose(a_b_s[j]), GROUP_A, axis=1)
            ws_t = jnp.tile(a_w_s[j][:, :T, :T], (1, CHUNK_A // T, CHUNK_A // T))
            wmix_s = jnp.where(tril_s, ws_t, 0.0).astype(BF16)
            bmix_s = jnp.repeat(jnp.tile(jnp.transpose(a_b_s[j][:, :T]), (CHUNK_A // T, 1)), GROUP_A, axis=1)
            (xp,) = _gmlp_layer(xp, mp[0], mp[1], mp[2], g_mix, w_in, vg, wmix_p, bmix_p, w_out,
                                False, S, False)
            xs, v_new = _gmlp_layer(xs, ms[0], ms[1], ms[2], g_mix, w_in, vg, wmix_s, bmix_s, w_out,
                                    True, T, True)
            chunkv_s.append(v_new.reshape(DB, T, D_A))
        elif kind == 1:
            w_qkv = b_w_qkv[j].astype(BF16)
            w_out = b_w_out[j].astype(BF16)
            qg = b_q_gain[j].reshape(1, HEAD_DIM_B)
            kg = b_k_gain[j].reshape(1, HEAD_DIM_B)
            qp, kp, vp, kbp, vbp, kmean_p = _qkv_layer(xp, mp[0], mp[1], g_mix, w_qkv, qg, kg, False, S, True)
            qs, ks, vs = _qkv_layer(xs, ms[0], ms[1], g_mix, w_qkv, qg, kg, True, T, False)
            xp = _moba_prompt_attention(xp, mp[2], w_out, qp, kbp, vbp, kmean_p, bias_near, B, S)
            os_ = _moba_sample_attention(
                qs.reshape(DB, T, D), ks.reshape(DB, T, N_HEADS_B, HEAD_DIM_B),
                vs.reshape(DB, T, N_HEADS_B, HEAD_DIM_B), cache_k, cache_v, j, page_table,
                bias_past, bias_own, T)
            xs = _proj_residual(xs, ms[2], os_.reshape(DB * T, D), w_out, True, T)
            k_p.append(kp.reshape(B, S, N_HEADS_B, HEAD_DIM_B))
            v_p.append(vp.reshape(B, S, N_HEADS_B, HEAD_DIM_B))
            k_s.append(ks.reshape(DB, T, N_HEADS_B, HEAD_DIM_B))
            v_s.append(vs.reshape(DB, T, N_HEADS_B, HEAD_DIM_B))
        else:
            w_in = c_w_in[j].astype(BF16)
            w_out = c_w_out[j].astype(BF16)
            st = state_conv[j]
            zrow = jnp.zeros((DB, T - 1, D_C), F32)
            p1 = jnp.concatenate([st[:, 1:2], zrow], axis=1).reshape(DB * T, D_C)
            p2 = jnp.concatenate([st, zrow[:, 1:]], axis=1).reshape(DB * T, D_C)
            xp, tail_p = _conv_layer(xp, mp[0], mp[1], mp[2], g_mix, w_in, c_conv[j], w_out, False, S)
            xs, xin_s = _conv_layer(xs, ms[0], ms[1], ms[2], g_mix, w_in, c_conv[j], w_out, True, T,
                                    fills=(p1, p2))
            conv_p.append(tail_p[:, SUBLANES - (CONV_W - 1):, :])
            conv_s.append(xin_s.reshape(DB, T, D_C)[:, T - (CONV_W - 1):, :])
        wu = w_up[i].astype(BF16)
        wd = w_down[i].astype(BF16)
        xp = _mlp_layer(xp, mp[3], mp[4], mp[5], g_mlp, wu, wd, False, S)
        xs = _mlp_layer(xs, ms[3], ms[4], ms[5], g_mlp, wu, wd, True, T)
    return (xp.reshape(B, S, D), xs.reshape(DB, T, D), jnp.stack(k_p), jnp.stack(v_p), jnp.stack(k_s),
            jnp.stack(v_s), jnp.stack(conv_p), jnp.stack(conv_s), jnp.stack(chunkv_s))
```

```python
import functools
import math

import jax
import jax.numpy as jnp
import numpy as np
from jax import lax
from jax.experimental import pallas as pl
from jax.experimental.pallas import tpu as pltpu

D_MODEL = 1024
DEPTH = 4
N_MIXERS = 3
D_A = 2 * D_MODEL
N_GROUPS_A = 8
GROUP_A = D_A // N_GROUPS_A
CHUNK_A = 128
HEAD_DIM_B = 128
N_HEADS_B = D_MODEL // HEAD_DIM_B
BLOCK_B = 256
TOPK_B = 3
N_BUCKETS = 32
MAX_DIST = 1024
D_C = D_MODEL
CONV_W = 3
D_FF = 4 * D_MODEL
EPS = 1e-6
MASK_VALUE = -1e30
LOG2E = math.log2(math.e)
PAGE_SIZE = 128

SUBLANES = 8
LANES = 128
VMEM_LIMIT_BYTES = 56 * 1024 * 1024

TM_MLP = 1024
TF_MLP = 2048
TM_MIXER = 512
TM_PROJ = 1024
TN_MOD = 1536
PAGES_PER_STEP = 16

F32 = jnp.float32
BF16 = jnp.bfloat16
HIGHEST = lax.Precision.HIGHEST


def _cparams(*sem):
    return pltpu.CompilerParams(dimension_semantics=sem, vmem_limit_bytes=VMEM_LIMIT_BYTES)


def _rows(ref):
    v = ref[...]
    n, _, d = v.shape
    if n == 1:
        return v.reshape(1, d)
    return jnp.broadcast_to(v, (n, SUBLANES, d)).reshape(n * SUBLANES, d)


def _modulate(x, g, shift, scale):
    y = x * lax.rsqrt(jnp.mean(x * x, axis=-1, keepdims=True) + EPS)
    return (y * g) * (1.0 + scale) + shift


def _gelu_tanh(x):
    c = math.sqrt(2.0 / math.pi)
    return x * (0.5 * (1.0 + jnp.tanh(c * (x + 0.044715 * (x * x * x)))))


def _dot(a, b):
    return jnp.dot(a, b, preferred_element_type=F32)


def _dot_nt(a, b, **kw):
    return lax.dot_general(a, b, (((1,), (1,)), ((), ())), preferred_element_type=F32, **kw)


def _dot_tn(a, b):
    return lax.dot_general(a, b, (((0,), (0,)), ((), ())), preferred_element_type=F32)


def _mod_spec(per_row, tm, tiles_per_batch, ngrid):
    if per_row:
        if ngrid == 1:
            return pl.BlockSpec((tm // SUBLANES, 1, D_MODEL), lambda i: (i, 0, 0))
        return pl.BlockSpec((tm // SUBLANES, 1, D_MODEL), lambda i, f: (i, 0, 0))
    if ngrid == 1:
        return pl.BlockSpec((1, 1, D_MODEL), lambda i: (i // tiles_per_batch, 0, 0))
    return pl.BlockSpec((1, 1, D_MODEL), lambda i, f: (i // tiles_per_batch, 0, 0))


def _const_spec(shape):
    nd = len(shape)
    return pl.BlockSpec(shape, lambda i: (0,) * nd, pipeline_mode=pl.Buffered(1))


def _mod_kernel(c_ref, w_ref, b_ref, o_ref):
    c = c_ref[...]
    sc = (c * jax.nn.sigmoid(c)).astype(BF16)
    o_ref[0] = _dot(sc, w_ref[0].astype(BF16)) + b_ref[0]


def _ada_all(c_all, w_mod, b_mod):
    nrow = c_all.shape[0]
    tn = TN_MOD
    nn = (6 * D_MODEL) // tn
    return pl.pallas_call(
        _mod_kernel,
        grid=(DEPTH, nn),
        in_specs=[
            pl.BlockSpec((nrow, D_MODEL), lambda l, n: (0, 0)),
            pl.BlockSpec((1, D_MODEL, tn), lambda l, n: (l, 0, n)),
            pl.BlockSpec((1, 1, tn), lambda l, n: (l, 0, n)),
        ],
        out_specs=pl.BlockSpec((1, nrow, tn), lambda l, n: (l, 0, n)),
        out_shape=jax.ShapeDtypeStruct((DEPTH, nrow, 6 * D_MODEL), F32),
        compiler_params=_cparams("arbitrary", "arbitrary"),
        name="ada_mod",
    )(c_all, w_mod, b_mod.reshape(DEPTH, 1, 6 * D_MODEL))


def _mlp_kernel(x_ref, sh_ref, sc_ref, gt_ref, g_ref, wu_ref, wd_ref, o_ref):
    x = x_ref[...]
    h = _modulate(x, g_ref[...], _rows(sh_ref), _rows(sc_ref)).astype(BF16)
    acc = None
    for c in range(D_FF // TF_MLP):
        cols = slice(c * TF_MLP, (c + 1) * TF_MLP)
        a = jnp.maximum(_dot(h, wu_ref[:, cols]), 0.0)
        y = _dot((a * a).astype(BF16), wd_ref[cols, :])
        acc = y if acc is None else acc + y
    o_ref[...] = x + _rows(gt_ref) * acc


def _mlp_layer(x, shift, scale, gate, g, w_up, w_down, per_row, rows_per_batch):
    T = x.shape[0]
    tm = TM_MLP
    tpb = rows_per_batch // tm if not per_row else 1
    ms = _mod_spec(per_row, tm, tpb, 1)
    return pl.pallas_call(
        _mlp_kernel,
        grid=(T // tm,),
        in_specs=[
            pl.BlockSpec((tm, D_MODEL), lambda i: (i, 0)),
            ms, ms, ms,
            _const_spec((1, D_MODEL)),
            _const_spec((D_MODEL, D_FF)),
            _const_spec((D_FF, D_MODEL)),
        ],
        out_specs=pl.BlockSpec((tm, D_MODEL), lambda i: (i, 0)),
        out_shape=jax.ShapeDtypeStruct((T, D_MODEL), F32),
        compiler_params=_cparams("parallel"),
        name="mlp",
    )(x, shift, scale, gate, g, w_up, w_down)


def _gmlp_kernel(x_ref, sh_ref, sc_ref, gt_ref, g_ref, win_ref, vg_ref, wmix_ref, bmix_ref, wout_ref,
                 *out_refs, tm, emit_v):
    o_ref = out_refs[0]
    gs_scr = out_refs[-1]
    x = x_ref[...]
    h = _modulate(x, g_ref[...], _rows(sh_ref), _rows(sc_ref)).astype(BF16)
    z = _gelu_tanh(_dot(h, win_ref[...]))
    u = z[:, :D_A]
    v = z[:, D_A:]
    v = v * lax.rsqrt(jnp.mean(v * v, axis=-1, keepdims=True) + EPS) * vg_ref[...]
    if emit_v:
        out_refs[1][...] = v
    vb = v.astype(BF16)
    for c in range(tm // CHUNK_A):
        r0 = c * CHUNK_A
        for gi in range(N_GROUPS_A):
            c0 = gi * GROUP_A
            s = _dot(wmix_ref[gi], vb[r0:r0 + CHUNK_A, c0:c0 + GROUP_A])
            s = s + bmix_ref[:, c0:c0 + GROUP_A]
            gs_scr[r0:r0 + CHUNK_A, c0:c0 + GROUP_A] = (u[r0:r0 + CHUNK_A, c0:c0 + GROUP_A] * s).astype(BF16)
    y = _dot(gs_scr[...], wout_ref[...])
    o_ref[...] = x + _rows(gt_ref) * y


def _gmlp_layer(x, shift, scale, gate, g, w_in, v_gain, w_mix, b_mix, w_out, per_row, rows_per_batch,
                emit_v):
    T = x.shape[0]
    tm = TM_MIXER
    tpb = rows_per_batch // tm if not per_row else 1
    ms = _mod_spec(per_row, tm, tpb, 1)
    out_shape = [jax.ShapeDtypeStruct((T, D_MODEL), F32)]
    out_specs = [pl.BlockSpec((tm, D_MODEL), lambda i: (i, 0))]
    if emit_v:
        out_shape.append(jax.ShapeDtypeStruct((T, D_A), F32))
        out_specs.append(pl.BlockSpec((tm, D_A), lambda i: (i, 0)))
    return pl.pallas_call(
        functools.partial(_gmlp_kernel, tm=tm, emit_v=emit_v),
        grid=(T // tm,),
        in_specs=[
            pl.BlockSpec((tm, D_MODEL), lambda i: (i, 0)),
            ms, ms, ms,
            _const_spec((1, D_MODEL)),
            _const_spec((D_MODEL, 2 * D_A)),
            _const_spec((1, D_A)),
            _const_spec((N_GROUPS_A, CHUNK_A, CHUNK_A)),
            _const_spec((CHUNK_A, D_A)),
            _const_spec((D_A, D_MODEL)),
        ],
        out_specs=out_specs,
        out_shape=out_shape,
        scratch_shapes=[pltpu.VMEM((tm, D_A), BF16)],
        compiler_params=_cparams("parallel"),
        name="gmlp_mixer",
    )(x, shift, scale, gate, g, w_in, v_gain, w_mix, b_mix, w_out)


def _conv_kernel(*refs, tm, per_row, tiles_per_batch):
    if per_row:
        (x_ref, sh_ref, sc_ref, gt_ref, g_ref, win_ref, cw_ref, wout_ref, p1_ref, p2_ref,
         o_ref, xin_ref) = refs
    else:
        (x_ref, sh_ref, sc_ref, gt_ref, g_ref, win_ref, cw_ref, wout_ref,
         o_ref, tail_ref, carry_scr) = refs
    x = x_ref[...]
    h = _modulate(x, g_ref[...], _rows(sh_ref), _rows(sc_ref)).astype(BF16)
    bcx = _dot(h, win_ref[...])
    gate_out = bcx[:, :D_C]
    xin = bcx[:, D_C:2 * D_C] * bcx[:, 2 * D_C:]
    row = lax.broadcasted_iota(jnp.int32, (tm, D_C), 0)
    roll1 = pltpu.roll(xin, 1, 0)
    roll2 = pltpu.roll(xin, 2, 0)
    if per_row:
        pos = row % SUBLANES
        prev1 = jnp.where(pos == 0, p1_ref[...], roll1)
        prev2 = jnp.where(pos < 2, p2_ref[...], roll2)
        xin_ref[...] = xin
    else:
        @pl.when(pl.program_id(0) % tiles_per_batch == 0)
        def _():
            carry_scr[...] = jnp.zeros_like(carry_scr)
        c6 = carry_scr[SUBLANES - 2:SUBLANES - 1, :]
        c7 = carry_scr[SUBLANES - 1:SUBLANES, :]
        prev1 = jnp.where(row == 0, c7, roll1)
        prev2 = jnp.where(row == 0, c6, jnp.where(row == 1, c7, roll2))
        tail = xin[tm - SUBLANES:, :]
        carry_scr[...] = tail
        tail_ref[0] = tail
    y = cw_ref[0:1, :] * prev2 + cw_ref[1:2, :] * prev1 + cw_ref[2:3, :] * xin
    out = _dot((gate_out * y).astype(BF16), wout_ref[...])
    o_ref[...] = x + _rows(gt_ref) * out


def _conv_layer(x, shift, scale, gate, g, w_in, conv_w, w_out, per_row, rows_per_batch, fills=None):
    T = x.shape[0]
    tm = TM_PROJ
    tpb = rows_per_batch // tm if not per_row else 1
    ms = _mod_spec(per_row, tm, tpb, 1)
    in_specs = [
        pl.BlockSpec((tm, D_MODEL), lambda i: (i, 0)),
        ms, ms, ms,
        _const_spec((1, D_MODEL)),
        _const_spec((D_MODEL, 3 * D_C)),
        _const_spec((CONV_W, D_C)),
        _const_spec((D_C, D_MODEL)),
    ]
    args = [x, shift, scale, gate, g, w_in, conv_w, w_out]
    out_shape = [jax.ShapeDtypeStruct((T, D_MODEL), F32)]
    out_specs = [pl.BlockSpec((tm, D_MODEL), lambda i: (i, 0))]
    scratch = []
    if per_row:
        in_specs += [pl.BlockSpec((tm, D_C), lambda i: (i, 0))] * 2
        args += list(fills)
        out_shape.append(jax.ShapeDtypeStruct((T, D_C), F32))
        out_specs.append(pl.BlockSpec((tm, D_C), lambda i: (i, 0)))
        sem = "parallel"
    else:
        nb = T // rows_per_batch
        out_shape.append(jax.ShapeDtypeStruct((nb, SUBLANES, D_C), F32))
        out_specs.append(pl.BlockSpec((1, SUBLANES, D_C), lambda i: (i // tpb, 0, 0)))
        scratch.append(pltpu.VMEM((SUBLANES, D_C), F32))
        sem = "arbitrary"
    return pl.pallas_call(
        functools.partial(_conv_kernel, tm=tm, per_row=per_row, tiles_per_batch=tpb),
        grid=(T // tm,),
        in_specs=in_specs,
        out_specs=out_specs,
        out_shape=out_shape,
        scratch_shapes=scratch,
        compiler_params=_cparams(sem),
        name="conv_mixer",
    )(*args)


def _qkv_kernel(x_ref, sh_ref, sc_ref, g_ref, w_ref, qg_ref, kg_ref, q_ref, k_ref, v_ref, *extra,
                tm, emit_attn_inputs):
    h = _modulate(x_ref[...], g_ref[...], _rows(sh_ref), _rows(sc_ref)).astype(BF16)
    qkv = _dot(h, w_ref[...])
    hd = HEAD_DIM_B
    for hh in range(N_HEADS_B):
        cols = slice(hh * hd, (hh + 1) * hd)
        q = qkv[:, hh * hd:(hh + 1) * hd]
        k = qkv[:, D_MODEL + hh * hd:D_MODEL + (hh + 1) * hd]
        q_ref[:, cols] = q * lax.rsqrt(jnp.mean(q * q, axis=-1, keepdims=True) + EPS) * qg_ref[...]
        kn = k * lax.rsqrt(jnp.mean(k * k, axis=-1, keepdims=True) + EPS) * kg_ref[...]
        k_ref[:, cols] = kn
        if emit_attn_inputs:
            kb_ref, vb_ref, km_ref = extra
            kb_ref[:, cols] = kn.astype(BF16)
            for c in range(tm // BLOCK_B):
                km_ref[c, :, cols] = jnp.sum(kn[c * BLOCK_B:(c + 1) * BLOCK_B], axis=0, keepdims=True) * (
                    1.0 / BLOCK_B)
    v = qkv[:, 2 * D_MODEL:]
    v_ref[...] = v
    if emit_attn_inputs:
        extra[1][...] = v.astype(BF16)


def _qkv_layer(x, shift, scale, g, w_qkv, q_gain, k_gain, per_row, rows_per_batch, emit_attn_inputs):
    T = x.shape[0]
    tm = TM_PROJ
    tpb = rows_per_batch // tm if not per_row else 1
    ms = _mod_spec(per_row, tm, tpb, 1)
    row_spec = pl.BlockSpec((tm, D_MODEL), lambda i: (i, 0))
    out_specs = [row_spec, row_spec, row_spec]
    out_shape = [jax.ShapeDtypeStruct((T, D_MODEL), F32)] * 3
    if emit_attn_inputs:
        out_specs += [row_spec, row_spec,
                      pl.BlockSpec((tm // BLOCK_B, 1, D_MODEL), lambda i: (i, 0, 0))]
        out_shape += [jax.ShapeDtypeStruct((T, D_MODEL), BF16)] * 2
        out_shape += [jax.ShapeDtypeStruct((T // BLOCK_B, 1, D_MODEL), F32)]
    return pl.pallas_call(
        functools.partial(_qkv_kernel, tm=tm, emit_attn_inputs=emit_attn_inputs),
        grid=(T // tm,),
        in_specs=[row_spec, ms, ms,
                  _const_spec((1, D_MODEL)),
                  _const_spec((D_MODEL, 3 * D_MODEL)),
                  _const_spec((1, HEAD_DIM_B)),
                  _const_spec((1, HEAD_DIM_B))],
        out_specs=out_specs,
        out_shape=out_shape,
        compiler_params=_cparams("parallel"),
        name="moba_qkv",
    )(x, shift, scale, g, w_qkv, q_gain, k_gain)


def _topk_mask_t(gate_t, cands, blk_of_row, elig):
    gm = jnp.where(elig, gate_t, MASK_VALUE)
    cnt = jnp.zeros(gate_t.shape, F32)
    for r, cb in cands:
        row = gm[r:r + 1, :]
        beats = jnp.where(row > gm, 1.0, jnp.where(row == gm, jnp.where(cb < blk_of_row, 1.0, 0.0), 0.0))
        cnt = cnt + beats
    return jnp.where(elig, jnp.where(cnt < TOPK_B, 1.0, 0.0), 0.0)


def _moba_prompt_kernel(x_ref, gt_ref, wout_ref, q_ref, kb_ref, vb_ref, km_ref, bias_ref, o_ref, qaug_scr, *,
                        nblk, nh):
    qb = pl.program_id(1)
    hd = HEAD_DIM_B
    W = BLOCK_B
    n_near = bias_ref.shape[2] // W
    dead_lane = LANES - 1
    lane = lax.broadcasted_iota(jnp.int32, (W, LANES), 1)
    blk = lax.broadcasted_iota(jnp.int32, (nblk, W), 0)
    r_i = lax.broadcasted_iota(jnp.int32, (W, W), 0)
    c_i = lax.broadcasted_iota(jnp.int32, (W, W), 1)
    causal = r_i >= c_i

    def onehot(b, valid):
        return jnp.where(lane == jnp.where(valid, b, dead_lane), 1.0, 0.0).astype(BF16)

    def rows_of(b):
        return pl.ds(pl.multiple_of(b * W, W), W)

    near_blk = [jnp.maximum(qb - d, 0) for d in range(n_near)]
    oh_near = jnp.concatenate([onehot(qb - d, qb - d >= 0) for d in range(n_near)], axis=0)
    ones_near = jnp.ones((n_near * W, hd), BF16)
    heads = [slice(h * hd, (h + 1) * hd) for h in range(nh)]
    gates = [_dot_nt(km_ref[:, 0, cols].astype(BF16), q_ref[:, cols].astype(BF16)) for cols in heads]
    scores = []
    for h, cols in enumerate(heads):
        sel_t = _topk_mask_t(gates[h], [(j, j) for j in range(nblk)], blk, blk < qb)
        neg_t = jnp.where(blk == qb, 0.0, jnp.where(sel_t > 0.5, 0.0, MASK_VALUE))
        neg = jnp.concatenate([neg_t, jnp.zeros((LANES - nblk - 1, W), F32),
                               jnp.full((1, W), MASK_VALUE, F32)], axis=0).T
        q_aug = jnp.concatenate([(q_ref[:, cols] * (LOG2E / math.sqrt(hd))).astype(BF16), neg.astype(BF16)],
                                axis=1)
        qaug_scr[h] = q_aug
        k_near = jnp.concatenate([kb_ref[rows_of(b), cols] for b in near_blk], axis=0)
        scores.append(_dot_nt(q_aug, jnp.concatenate([k_near, oh_near], axis=1)))
    carry0 = []
    for h, cols in enumerate(heads):
        s = scores[h] + bias_ref[h]
        s = jnp.concatenate([jnp.where(causal, s[:, :W], MASK_VALUE), s[:, W:]], axis=1)
        m0 = jnp.max(s, axis=-1, keepdims=True)
        p = jnp.exp2(s - m0)
        v_near = jnp.concatenate([vb_ref[rows_of(b), cols] for b in near_blk], axis=0)
        carry0 += [m0, _dot(p.astype(BF16), jnp.concatenate([v_near, ones_near], axis=1))]

    n_far = jnp.maximum(qb - (n_near - 1), 0)
    ones_pair = jnp.ones((2 * W, hd), BF16)

    def body(i, carry):
        b0 = 2 * i
        b1 = b0 + 1
        oh = jnp.concatenate([onehot(b0, True), onehot(b1, b1 < n_far)], axis=0)
        pair_scores = []
        for h, cols in enumerate(heads):
            k_pair = jnp.concatenate([kb_ref[rows_of(b0), cols], kb_ref[rows_of(b1), cols]], axis=0)
            pair_scores.append(_dot_nt(qaug_scr[h], jnp.concatenate([k_pair, oh], axis=1)))
        out = []
        for h, cols in enumerate(heads):
            m, acc = carry[2 * h], carry[2 * h + 1]
            s = pair_scores[h]
            v_pair = jnp.concatenate([vb_ref[rows_of(b0), cols], vb_ref[rows_of(b1), cols]], axis=0)
            m_new = jnp.maximum(m, jnp.max(s, axis=-1, keepdims=True))
            p = jnp.exp2(s - m_new)
            acc = jnp.exp2(m - m_new) * acc + _dot(p.astype(BF16), jnp.concatenate([v_pair, ones_pair], axis=1))
            out += [m_new, acc]
        return tuple(out)

    carry = lax.fori_loop(0, (n_far + 1) // 2, body, tuple(carry0))
    o = jnp.concatenate([(carry[2 * h + 1][:, :hd] / carry[2 * h + 1][:, hd:]).astype(BF16) for h in range(nh)],
                        axis=1)
    o_ref[...] = x_ref[...] + _rows(gt_ref) * _dot(o, wout_ref[...])


def _moba_prompt_attention(x, gate, w_out, q, kb, vb, kmean, bias_near, batch, seq):
    nq = seq // BLOCK_B
    hd = HEAD_DIM_B
    nh = N_HEADS_B
    one = pl.Buffered(1)
    row_spec = pl.BlockSpec((BLOCK_B, D_MODEL), lambda b, i: (b * nq + i, 0))
    return pl.pallas_call(
        functools.partial(_moba_prompt_kernel, nblk=nq, nh=nh),
        grid=(batch, nq),
        in_specs=[
            row_spec,
            pl.BlockSpec((1, 1, D_MODEL), lambda b, i: (b, 0, 0)),
            pl.BlockSpec((D_MODEL, D_MODEL), lambda b, i: (0, 0), pipeline_mode=one),
            row_spec,
            pl.BlockSpec((seq, nh * hd), lambda b, i: (b, 0), pipeline_mode=one),
            pl.BlockSpec((seq, nh * hd), lambda b, i: (b, 0), pipeline_mode=one),
            pl.BlockSpec((nq, 1, nh * hd), lambda b, i: (b, 0, 0)),
            pl.BlockSpec(bias_near.shape, lambda b, i: (0, 0, 0), pipeline_mode=one),
        ],
        out_specs=row_spec,
        out_shape=jax.ShapeDtypeStruct((batch * seq, D_MODEL), F32),
        scratch_shapes=[pltpu.VMEM((nh, BLOCK_B, 2 * hd), BF16)],
        compiler_params=_cparams("parallel", "arbitrary"),
        name="moba_prompt_attn",
    )(x, gate, w_out, q, kb, vb, kmean, bias_near)


def _moba_sample_kernel(pt_ref, q_ref, kn_ref, vn_ref, *refs, pps, n_pages, t_new):
    kc_refs = refs[:pps]
    vc_refs = refs[pps:2 * pps]
    bias_ref, bias_own_ref = refs[2 * pps:2 * pps + 2]
    o_ref = refs[2 * pps + 2]
    qs_scr, qs2_scr, qg_scr, m_scr, l_scr, o_scr, ksum_scr = refs[2 * pps + 3:]
    del pt_ref
    s_id = pl.program_id(1)
    H, hd = N_HEADS_B, HEAD_DIM_B
    nrow = H * t_new
    ppb = BLOCK_B // PAGE_SIZE
    assert ppb == 2
    n_blk = n_pages // ppb
    page_rows = PAGE_SIZE * H

    @pl.when(s_id == 0)
    def _():
        q8 = q_ref[0]
        qall = jnp.concatenate([q8[:, h * hd:(h + 1) * hd] for h in range(H)], axis=0)
        qg_scr[...] = qall.astype(BF16)
        qs = (qall * (1.0 / math.sqrt(hd))).astype(BF16)
        qs_scr[...] = qs
        zero = jnp.zeros((nrow, hd), BF16)
        qs2_scr[...] = jnp.concatenate([jnp.concatenate([qs, zero], axis=1),
                                        jnp.concatenate([zero, qs], axis=1)], axis=0)

    own_head = (lax.broadcasted_iota(jnp.int32, (H, nrow), 0)
                == lax.broadcasted_iota(jnp.int32, (H, nrow), 1) // t_new)
    own_head2 = (lax.broadcasted_iota(jnp.int32, (H, ppb * nrow), 0)
                 == (lax.broadcasted_iota(jnp.int32, (H, ppb * nrow), 1) % nrow) // t_new)

    def block_scores(k_pages, bias):
        k2 = jnp.concatenate(k_pages, axis=1).astype(BF16)
        return _dot_nt(k2, qs2_scr[...]) + bias

    def block_attn(g, v_pages):
        v2 = jnp.concatenate(v_pages, axis=1).astype(BF16)
        g3 = g.reshape(PAGE_SIZE, H, ppb * nrow)
        m_t = jnp.max(g3, axis=0)
        m_t = jnp.maximum(m_t, pltpu.roll(m_t, nrow, 1))
        e3 = jnp.exp(g3 - jnp.where(own_head2, m_t, 0.0)[None])
        l_t = jnp.sum(e3, axis=0)
        l_t = l_t + pltpu.roll(l_t, nrow, 1)
        o2 = _dot_tn(e3.reshape(PAGE_SIZE * H, ppb * nrow).astype(BF16), v2)
        return m_t, l_t, o2[:nrow, :hd] + o2[nrow:, hd:]

    def partial_attn(k_rows, v_rows, bias, valid):
        n = k_rows.shape[0]
        g = _dot_nt(k_rows.astype(BF16), qs_scr[...]) + bias
        if valid is not None:
            g = jnp.where(valid, g, MASK_VALUE)
        g3 = g.reshape(n // H, H, nrow)
        m_t = jnp.max(g3, axis=0)
        e3 = jnp.exp(g3 - jnp.where(own_head, m_t, 0.0)[None])
        l_t = jnp.sum(e3, axis=0)
        o = _dot_tn(e3.reshape(n, nrow).astype(BF16), v_rows.astype(BF16))
        return m_t, l_t, o

    scores = []
    for i in range(pps // ppb):
        blk = s_id * (pps // ppb) + i
        k_pages = [kc_refs[i * ppb + u][0, 0].reshape(page_rows, hd) for u in range(ppb)]
        ksum_scr[blk] = jnp.sum((k_pages[0] + k_pages[1]).reshape(PAGE_SIZE, H, hd), axis=0)
        b0 = pl.multiple_of(blk * page_rows, page_rows)
        scores.append((blk, block_scores(k_pages, bias_ref[pl.ds(b0, page_rows), :])))
    for i, (blk, g) in enumerate(scores):
        v_pages = [vc_refs[i * ppb + u][0, 0].reshape(page_rows, hd) for u in range(ppb)]
        m_t, l_t, o = block_attn(g, v_pages)
        m_scr[blk] = m_t
        l_scr[blk] = l_t
        o_scr[blk] = o

    @pl.when(s_id == pl.num_programs(1) - 1)
    def _():
        def head_diag(t):
            return jnp.sum(jnp.where(own_head[None], t[:, :, :nrow], 0.0), axis=1)

        key_i = lax.broadcasted_iota(jnp.int32, (t_new * H, nrow), 0) // H
        qry_i = lax.broadcasted_iota(jnp.int32, (t_new * H, nrow), 1) % t_new
        m_ot, l_ot, o_o = partial_attn(kn_ref[0].reshape(t_new * H, hd), vn_ref[0].reshape(t_new * H, hd),
                                       bias_own_ref[...], key_i <= qry_i)
        m_o = head_diag(m_ot[None])
        l_o = head_diag(l_ot[None])

        kmean = (ksum_scr[...] * (1.0 / BLOCK_B)).reshape(n_blk * H, hd)
        gate = _dot_nt(kmean.astype(BF16), qg_scr[...])
        gate_t = head_diag(gate.reshape(n_blk, H, nrow))
        bi = lax.broadcasted_iota(jnp.int32, (n_blk, nrow), 0)
        sel = _topk_mask_t(gate_t, [(j, j) for j in range(n_blk)], bi, bi >= 0)

        m_all = head_diag(m_scr[...])
        l_all = head_diag(l_scr[...])
        m_fin = jnp.maximum(jnp.max(jnp.where(sel > 0.5, m_all, MASK_VALUE), axis=0, keepdims=True), m_o)
        w = jnp.where(sel > 0.5, jnp.exp(m_all - m_fin), 0.0)
        w_o = jnp.exp(m_o - m_fin)
        l_fin = jnp.sum(w * l_all, axis=0, keepdims=True) + w_o * l_o
        stack = jnp.concatenate([w, w_o, l_fin], axis=0)
        stack = jnp.concatenate([stack, jnp.zeros((stack.shape[0], LANES - nrow), F32)], axis=1)
        stack = jnp.concatenate([stack, jnp.zeros((LANES - stack.shape[0], LANES), F32)], axis=0)
        st = stack.T
        acc = st[:nrow, n_blk:n_blk + 1] * o_o
        for pp in range(n_blk):
            acc = acc + st[:nrow, pp:pp + 1] * o_scr[pp]
        out = acc / st[:nrow, n_blk + 1:n_blk + 2]
        for h in range(H):
            o_ref[0, :, h * hd:(h + 1) * hd] = out[h * t_new:(h + 1) * t_new, :]


def _moba_sample_attention(q, k_new, v_new, cache_k, cache_v, layer, page_table, bias_past, bias_own, t_new):
    nb, n_pages = page_table.shape
    pps = PAGES_PER_STEP
    n_steps = n_pages // pps
    H, hd = N_HEADS_B, HEAD_DIM_B
    nrow = H * t_new
    n_blk = n_pages * PAGE_SIZE // BLOCK_B

    def page_spec(i):
        return pl.BlockSpec((1, 1, PAGE_SIZE, H, hd),
                            lambda b, s, pt: (layer, pt[b * n_pages + s * pps + i], 0, 0, 0))

    row_spec = pl.BlockSpec((1, t_new, H * hd), lambda b, s, pt: (b, 0, 0))
    own_spec = pl.BlockSpec((1, t_new, H, hd), lambda b, s, pt: (b, 0, 0, 0))
    one = pl.Buffered(1)
    grid_spec = pltpu.PrefetchScalarGridSpec(
        num_scalar_prefetch=1,
        grid=(nb, n_steps),
        in_specs=[row_spec, own_spec, own_spec]
                 + [page_spec(i) for i in range(pps)] + [page_spec(i) for i in range(pps)]
                 + [pl.BlockSpec(bias_past.shape, lambda b, s, pt: (0, 0), pipeline_mode=one),
                    pl.BlockSpec(bias_own.shape, lambda b, s, pt: (0, 0), pipeline_mode=one)],
        out_specs=row_spec,
        scratch_shapes=[
            pltpu.VMEM((nrow, hd), BF16),
            pltpu.VMEM((2 * nrow, 2 * hd), BF16),
            pltpu.VMEM((nrow, hd), BF16),
            pltpu.VMEM((n_blk, H, 2 * nrow), F32),
            pltpu.VMEM((n_blk, H, 2 * nrow), F32),
            pltpu.VMEM((n_blk, nrow, hd), F32),
            pltpu.VMEM((n_blk, H, hd), F32),
        ],
    )
    return pl.pallas_call(
        functools.partial(_moba_sample_kernel, pps=pps, n_pages=n_pages, t_new=t_new),
        grid_spec=grid_spec,
        out_shape=jax.ShapeDtypeStruct((nb, t_new, H * hd), F32),
        compiler_params=_cparams("parallel", "arbitrary"),
        name="moba_sample_attn",
    )(page_table.reshape(-1), q, k_new, v_new, *([cache_k] * pps), *([cache_v] * pps), bias_past, bias_own)


def _proj_kernel(x_ref, gt_ref, a_ref, w_ref, o_ref):
    o_ref[...] = x_ref[...] + _rows(gt_ref) * _dot(a_ref[...].astype(BF16), w_ref[...])


def _proj_residual(x, gate, a, w, per_row, rows_per_batch):
    T = x.shape[0]
    tm = TM_MIXER
    tpb = rows_per_batch // tm if not per_row else 1
    row_spec = pl.BlockSpec((tm, D_MODEL), lambda i: (i, 0))
    return pl.pallas_call(
        _proj_kernel,
        grid=(T // tm,),
        in_specs=[row_spec, _mod_spec(per_row, tm, tpb, 1), row_spec,
                  _const_spec((D_MODEL, D_MODEL))],
        out_specs=row_spec,
        out_shape=jax.ShapeDtypeStruct((T, D_MODEL), F32),
        compiler_params=_cparams("parallel"),
        name="moba_out_proj",
    )(x, gate, a, w)


def _rel_bucket(dist):
    max_exact = N_BUCKETS // 2
    d = jnp.maximum(dist, 0)
    log_ratio = jnp.log(jnp.maximum(d, 1).astype(jnp.float32) / max_exact) / math.log(MAX_DIST / max_exact)
    large = jnp.minimum(max_exact + (log_ratio * (N_BUCKETS - max_exact)).astype(jnp.int32), N_BUCKETS - 1)
    return jnp.where(d < max_exact, d, large)


def _bias_lookup(dist, rel_bias, out_spec):
    onehot = (_rel_bucket(jnp.asarray(dist, jnp.int32))[..., None] == jnp.arange(N_BUCKETS)).astype(F32)
    return jnp.einsum(out_spec, onehot, rel_bias, precision=HIGHEST)


def _bias_tables(rel_bias, past_len, t_new):
    H, W = N_HEADS_B, BLOCK_B
    n_near = MAX_DIST // W + 1
    r = np.arange(W)[:, None]
    col = np.arange(n_near * W)[None, :]
    dist = (col // W) * W + r - (col % W)
    near = _bias_lookup(dist, rel_bias - rel_bias[N_BUCKETS - 1], "rcb,bh->hrc") * LOG2E
    ppb = W // PAGE_SIZE
    shape = (past_len // W * PAGE_SIZE * H, ppb * H * t_new)
    row = lax.broadcasted_iota(jnp.int32, shape, 0)
    col = lax.broadcasted_iota(jnp.int32, shape, 1)
    kpos = ((row // (PAGE_SIZE * H)) * ppb + col // (H * t_new)) * PAGE_SIZE + (row // H) % PAGE_SIZE
    bucket = _rel_bucket(past_len + col % t_new - kpos)
    col_bias = rel_bias[:, (np.arange(shape[1]) // t_new) % H]
    past = jnp.zeros(shape, F32)
    for b in range(N_BUCKETS):
        past = jnp.where(bucket == b, col_bias[b][None, :], past)
    past = jnp.where(row % H == (col // t_new) % H, past, MASK_VALUE)
    qr = np.arange(t_new)
    same_head = np.eye(H, dtype=bool)
    own = _bias_lookup(np.maximum(qr[None, :] - qr[:, None], 0), rel_bias, "trb,bh->thr")
    own = jnp.where(same_head[None, :, :, None], own[:, None], MASK_VALUE).reshape(t_new * H, H * t_new)
    return near, past, own


def kernel(x_prompt, x_sample, cache_k, cache_v, state_conv, page_table, c_prompt, c_sample, rel_bias, norm_mix, norm_mlp, w_mod, b_mod, w_up, w_down, a_w_in, a_v_gain, a_w_s, a_b_s, a_w_out, b_w_qkv, b_q_gain, b_k_gain, b_w_out, c_w_in, c_conv, c_w_out):
    B, S, D = x_prompt.shape
    DB, T, _ = x_sample.shape
    n_pages = page_table.shape[1]
    past_len = n_pages * PAGE_SIZE
    assert S % BLOCK_B == 0 and past_len % BLOCK_B == 0 and T == SUBLANES
    assert MAX_DIST % BLOCK_B == 0

    xp = x_prompt.reshape(B * S, D)
    xs = x_sample.reshape(DB * T, D)

    n_c = B + DB
    n_c_pad = -(-n_c // SUBLANES) * SUBLANES
    c_all = jnp.concatenate([c_prompt, c_sample, jnp.zeros((n_c_pad - n_c, D), F32)], axis=0)
    mods = _ada_all(c_all, w_mod, b_mod)

    bias_near, bias_past, bias_own = _bias_tables(rel_bias, past_len, T)

    ci = np.arange(CHUNK_A)
    tril_p = (ci[:, None] >= ci[None, :])
    tril_s = tril_p & ((ci[:, None] // T) == (ci[None, :] // T))

    k_p, v_p, k_s, v_s, conv_p, conv_s, chunkv_s = [], [], [], [], [], [], []
    for i in range(DEPTH):
        kind, j = i % N_MIXERS, i // N_MIXERS
        mp = [m.reshape(B, 1, D) for m in jnp.split(mods[i, :B], 6, axis=-1)]
        ms = [m.reshape(DB, 1, D) for m in jnp.split(mods[i, B:B + DB], 6, axis=-1)]
        g_mix = norm_mix[i].reshape(1, D)
        g_mlp = norm_mlp[i].reshape(1, D)
        if kind == 0:
            w_in = a_w_in[j].astype(BF16)
            w_out = a_w_out[j].astype(BF16)
            vg = a_v_gain[j].reshape(1, D_A)
            wmix_p = jnp.where(tril_p, a_w_s[j], 0.0).astype(BF16)
            bmix_p = jnp.repeat(jnp.transpose(a_b_s[j]), GROUP_A, axis=1)
            ws_t = jnp.tile(a_w_s[j][:, :T, :T], (1, CHUNK_A // T, CHUNK_A // T))
            wmix_s = jnp.where(tril_s, ws_t, 0.0).astype(BF16)
            bmix_s = jnp.repeat(jnp.tile(jnp.transpose(a_b_s[j][:, :T]), (CHUNK_A // T, 1)), GROUP_A, axis=1)
            (xp,) = _gmlp_layer(xp, mp[0], mp[1], mp[2], g_mix, w_in, vg, wmix_p, bmix_p, w_out,
                                False, S, False)
            xs, v_new = _gmlp_layer(xs, ms[0], ms[1], ms[2], g_mix, w_in, vg, wmix_s, bmix_s, w_out,
                                    True, T, True)
            chunkv_s.append(v_new.reshape(DB, T, D_A))
        elif kind == 1:
            w_qkv = b_w_qkv[j].astype(BF16)
            w_out = b_w_out[j].astype(BF16)
            qg = b_q_gain[j].reshape(1, HEAD_DIM_B)
            kg = b_k_gain[j].reshape(1, HEAD_DIM_B)
            qp, kp, vp, kbp, vbp, kmean_p = _qkv_layer(xp, mp[0], mp[1], g_mix, w_qkv, qg, kg, False, S, True)
            qs, ks, vs = _qkv_layer(xs, ms[0], ms[1], g_mix, w_qkv, qg, kg, True, T, False)
            xp = _moba_prompt_attention(xp, mp[2], w_out, qp, kbp, vbp, kmean_p, bias_near, B, S)
            os_ = _moba_sample_attention(
                qs.reshape(DB, T, D), ks.reshape(DB, T, N_HEADS_B, HEAD_DIM_B),
                vs.reshape(DB, T, N_HEADS_B, HEAD_DIM_B), cache_k, cache_v, j, page_table,
                bias_past, bias_own, T)
            xs = _proj_residual(xs, ms[2], os_.reshape(DB * T, D), w_out, True, T)
            k_p.append(kp.reshape(B, S, N_HEADS_B, HEAD_DIM_B))
            v_p.append(vp.reshape(B, S, N_HEADS_B, HEAD_DIM_B))
            k_s.append(ks.reshape(DB, T, N_HEADS_B, HEAD_DIM_B))
            v_s.append(vs.reshape(DB, T, N_HEADS_B, HEAD_DIM_B))
        else:
            w_in = c_w_in[j].astype(BF16)
            w_out = c_w_out[j].astype(BF16)
            st = state_conv[j]
            zrow = jnp.zeros((DB, T - 1, D_C), F32)
            p1 = jnp.concatenate([st[:, 1:2], zrow], axis=1).reshape(DB * T, D_C)
            p2 = jnp.concatenate([st, zrow[:, 1:]], axis=1).reshape(DB * T, D_C)
            xp, tail_p = _conv_layer(xp, mp[0], mp[1], mp[2], g_mix, w_in, c_conv[j], w_out, False, S)
            xs, xin_s = _conv_layer(xs, ms[0], ms[1], ms[2], g_mix, w_in, c_conv[j], w_out, True, T,
                                    fills=(p1, p2))
            conv_p.append(tail_p[:, SUBLANES - (CONV_W - 1):, :])
            conv_s.append(xin_s.reshape(DB, T, D_C)[:, T - (CONV_W - 1):, :])
        wu = w_up[i].astype(BF16)
        wd = w_down[i].astype(BF16)
        xp = _mlp_layer(xp, mp[3], mp[4], mp[5], g_mlp, wu, wd, False, S)
        xs = _mlp_layer(xs, ms[3], ms[4], ms[5], g_mlp, wu, wd, True, T)
    return (xp.reshape(B, S, D), xs.reshape(DB, T, D), jnp.stack(k_p), jnp.stack(v_p), jnp.stack(k_s),
            jnp.stack(v_s), jnp.stack(conv_p), jnp.stack(conv_s), jnp.stack(chunkv_s))
```

```python
import functools
import math

import jax
import jax.numpy as jnp
import numpy as np
from jax import lax
from jax.experimental import pallas as pl
from jax.experimental.pallas import tpu as pltpu

D_MODEL = 1024
DEPTH = 4
N_MIXERS = 3
D_A = 2 * D_MODEL
N_GROUPS_A = 8
GROUP_A = D_A // N_GROUPS_A
CHUNK_A = 128
HEAD_DIM_B = 128
N_HEADS_B = D_MODEL // HEAD_DIM_B
BLOCK_B = 256
TOPK_B = 3
N_BUCKETS = 32
MAX_DIST = 1024
D_C = D_MODEL
CONV_W = 3
D_FF = 4 * D_MODEL
EPS = 1e-6
MASK_VALUE = -1e30
LOG2E = math.log2(math.e)
PAGE_SIZE = 128

SUBLANES = 8
LANES = 128
VMEM_LIMIT_BYTES = 56 * 1024 * 1024

TM_MLP = 1024
TF_MLP = 2048
TM_MIXER = 512
TM_PROJ = 1024
TN_MOD = 1536
PAGES_PER_STEP = 16

F32 = jnp.float32
BF16 = jnp.bfloat16
HIGHEST = lax.Precision.HIGHEST


def _cparams(*sem):
    return pltpu.CompilerParams(dimension_semantics=sem, vmem_limit_bytes=VMEM_LIMIT_BYTES)


def _rows(ref):
    v = ref[...]
    n, _, d = v.shape
    if n == 1:
        return v.reshape(1, d)
    return jnp.broadcast_to(v, (n, SUBLANES, d)).reshape(n * SUBLANES, d)


def _modulate(x, g, shift, scale):
    y = x * lax.rsqrt(jnp.mean(x * x, axis=-1, keepdims=True) + EPS)
    return (y * g) * (1.0 + scale) + shift


def _gelu_tanh(x):
    c = math.sqrt(2.0 / math.pi)
    return x * (0.5 * (1.0 + jnp.tanh(c * (x + 0.044715 * (x * x * x)))))


def _dot(a, b):
    return jnp.dot(a, b, preferred_element_type=F32)


def _dot_nt(a, b, **kw):
    return lax.dot_general(a, b, (((1,), (1,)), ((), ())), preferred_element_type=F32, **kw)


def _dot_tn(a, b):
    return lax.dot_general(a, b, (((0,), (0,)), ((), ())), preferred_element_type=F32)


def _mod_spec(per_row, tm, tiles_per_batch, ngrid):
    if per_row:
        if ngrid == 1:
            return pl.BlockSpec((tm // SUBLANES, 1, D_MODEL), lambda i: (i, 0, 0))
        return pl.BlockSpec((tm // SUBLANES, 1, D_MODEL), lambda i, f: (i, 0, 0))
    if ngrid == 1:
        return pl.BlockSpec((1, 1, D_MODEL), lambda i: (i // tiles_per_batch, 0, 0))
    return pl.BlockSpec((1, 1, D_MODEL), lambda i, f: (i // tiles_per_batch, 0, 0))


def _const_spec(shape):
    nd = len(shape)
    return pl.BlockSpec(shape, lambda i: (0,) * nd, pipeline_mode=pl.Buffered(1))


def _mod_kernel(c_ref, w_ref, b_ref, o_ref):
    c = c_ref[...]
    sc = (c * jax.nn.sigmoid(c)).astype(BF16)
    o_ref[0] = _dot(sc, w_ref[0].astype(BF16)) + b_ref[0]


def _ada_all(c_all, w_mod, b_mod):
    nrow = c_all.shape[0]
    tn = TN_MOD
    nn = (6 * D_MODEL) // tn
    return pl.pallas_call(
        _mod_kernel,
        grid=(DEPTH, nn),
        in_specs=[
            pl.BlockSpec((nrow, D_MODEL), lambda l, n: (0, 0)),
            pl.BlockSpec((1, D_MODEL, tn), lambda l, n: (l, 0, n)),
            pl.BlockSpec((1, 1, tn), lambda l, n: (l, 0, n)),
        ],
        out_specs=pl.BlockSpec((1, nrow, tn), lambda l, n: (l, 0, n)),
        out_shape=jax.ShapeDtypeStruct((DEPTH, nrow, 6 * D_MODEL), F32),
        compiler_params=_cparams("arbitrary", "arbitrary"),
        name="ada_mod",
    )(c_all, w_mod, b_mod.reshape(DEPTH, 1, 6 * D_MODEL))


def _mlp_kernel(x_ref, sh_ref, sc_ref, gt_ref, g_ref, wu_ref, wd_ref, o_ref):
    x = x_ref[...]
    h = _modulate(x, g_ref[...], _rows(sh_ref), _rows(sc_ref)).astype(BF16)
    acc = None
    for c in range(D_FF // TF_MLP):
        cols = slice(c * TF_MLP, (c + 1) * TF_MLP)
        a = jnp.maximum(_dot(h, wu_ref[:, cols]), 0.0)
        y = _dot((a * a).astype(BF16), wd_ref[cols, :])
        acc = y if acc is None else acc + y
    o_ref[...] = x + _rows(gt_ref) * acc


def _mlp_layer(x, shift, scale, gate, g, w_up, w_down, per_row, rows_per_batch):
    T = x.shape[0]
    tm = TM_MLP
    tpb = rows_per_batch // tm if not per_row else 1
    ms = _mod_spec(per_row, tm, tpb, 1)
    return pl.pallas_call(
        _mlp_kernel,
        grid=(T // tm,),
        in_specs=[
            pl.BlockSpec((tm, D_MODEL), lambda i: (i, 0)),
            ms, ms, ms,
            _const_spec((1, D_MODEL)),
            _const_spec((D_MODEL, D_FF)),
            _const_spec((D_FF, D_MODEL)),
        ],
        out_specs=pl.BlockSpec((tm, D_MODEL), lambda i: (i, 0)),
        out_shape=jax.ShapeDtypeStruct((T, D_MODEL), F32),
        compiler_params=_cparams("parallel"),
        name="mlp",
    )(x, shift, scale, gate, g, w_up, w_down)


def _gmlp_kernel(x_ref, sh_ref, sc_ref, gt_ref, g_ref, win_ref, vg_ref, wmix_ref, bmix_ref, wout_ref,
                 *out_refs, tm, emit_v):
    o_ref = out_refs[0]
    gs_scr = out_refs[-1]
    x = x_ref[...]
    h = _modulate(x, g_ref[...], _rows(sh_ref), _rows(sc_ref)).astype(BF16)
    v = _gelu_tanh(_dot(h, win_ref[:, D_A:]))
    v = v * lax.rsqrt(jnp.mean(v * v, axis=-1, keepdims=True) + EPS) * vg_ref[...]
    if emit_v:
        out_refs[1][...] = v
    vb = v.astype(BF16)
    u = _gelu_tanh(_dot(h, win_ref[:, :D_A]))
    for c in range(tm // CHUNK_A):
        r0 = c * CHUNK_A
        for gi in range(N_GROUPS_A):
            c0 = gi * GROUP_A
            s = _dot(wmix_ref[gi], vb[r0:r0 + CHUNK_A, c0:c0 + GROUP_A])
            s = s + bmix_ref[:, c0:c0 + GROUP_A]
            gs_scr[r0:r0 + CHUNK_A, c0:c0 + GROUP_A] = (u[r0:r0 + CHUNK_A, c0:c0 + GROUP_A] * s).astype(BF16)
    y = _dot(gs_scr[...], wout_ref[...])
    o_ref[...] = x + _rows(gt_ref) * y


def _gmlp_layer(x, shift, scale, gate, g, w_in, v_gain, w_mix, b_mix, w_out, per_row, rows_per_batch,
                emit_v):
    T = x.shape[0]
    tm = TM_MIXER
    tpb = rows_per_batch // tm if not per_row else 1
    ms = _mod_spec(per_row, tm, tpb, 1)
    out_shape = [jax.ShapeDtypeStruct((T, D_MODEL), F32)]
    out_specs = [pl.BlockSpec((tm, D_MODEL), lambda i: (i, 0))]
    if emit_v:
        out_shape.append(jax.ShapeDtypeStruct((T, D_A), F32))
        out_specs.append(pl.BlockSpec((tm, D_A), lambda i: (i, 0)))
    return pl.pallas_call(
        functools.partial(_gmlp_kernel, tm=tm, emit_v=emit_v),
        grid=(T // tm,),
        in_specs=[
            pl.BlockSpec((tm, D_MODEL), lambda i: (i, 0)),
            ms, ms, ms,
            _const_spec((1, D_MODEL)),
            _const_spec((D_MODEL, 2 * D_A)),
            _const_spec((1, D_A)),
            _const_spec((N_GROUPS_A, CHUNK_A, CHUNK_A)),
            _const_spec((CHUNK_A, D_A)),
            _const_spec((D_A, D_MODEL)),
        ],
        out_specs=out_specs,
        out_shape=out_shape,
        scratch_shapes=[pltpu.VMEM((tm, D_A), BF16)],
        compiler_params=_cparams("parallel"),
        name="gmlp_mixer",
    )(x, shift, scale, gate, g, w_in, v_gain, w_mix, b_mix, w_out)


def _conv_kernel(*refs, tm, per_row, tiles_per_batch):
    if per_row:
        (x_ref, sh_ref, sc_ref, gt_ref, g_ref, win_ref, cw_ref, wout_ref, p1_ref, p2_ref,
         o_ref, xin_ref) = refs
    else:
        (x_ref, sh_ref, sc_ref, gt_ref, g_ref, win_ref, cw_ref, wout_ref,
         o_ref, tail_ref, carry_scr) = refs
    x = x_ref[...]
    h = _modulate(x, g_ref[...], _rows(sh_ref), _rows(sc_ref)).astype(BF16)
    bcx = _dot(h, win_ref[...])
    gate_out = bcx[:, :D_C]
    xin = bcx[:, D_C:2 * D_C] * bcx[:, 2 * D_C:]
    row = lax.broadcasted_iota(jnp.int32, (tm, D_C), 0)
    roll1 = pltpu.roll(xin, 1, 0)
    roll2 = pltpu.roll(xin, 2, 0)
    if per_row:
        pos = row % SUBLANES
        prev1 = jnp.where(pos == 0, p1_ref[...], roll1)
        prev2 = jnp.where(pos < 2, p2_ref[...], roll2)
        xin_ref[...] = xin
    else:
        @pl.when(pl.program_id(0) % tiles_per_batch == 0)
        def _():
            carry_scr[...] = jnp.zeros_like(carry_scr)
        c6 = carry_scr[SUBLANES - 2:SUBLANES - 1, :]
        c7 = carry_scr[SUBLANES - 1:SUBLANES, :]
        prev1 = jnp.where(row == 0, c7, roll1)
        prev2 = jnp.where(row == 0, c6, jnp.where(row == 1, c7, roll2))
        tail = xin[tm - SUBLANES:, :]
        carry_scr[...] = tail
        tail_ref[0] = tail
    y = cw_ref[0:1, :] * prev2 + cw_ref[1:2, :] * prev1 + cw_ref[2:3, :] * xin
    out = _dot((gate_out * y).astype(BF16), wout_ref[...])
    o_ref[...] = x + _rows(gt_ref) * out


def _conv_layer(x, shift, scale, gate, g, w_in, conv_w, w_out, per_row, rows_per_batch, fills=None):
    T = x.shape[0]
    tm = TM_PROJ
    tpb = rows_per_batch // tm if not per_row else 1
    ms = _mod_spec(per_row, tm, tpb, 1)
    in_specs = [
        pl.BlockSpec((tm, D_MODEL), lambda i: (i, 0)),
        ms, ms, ms,
        _const_spec((1, D_MODEL)),
        _const_spec((D_MODEL, 3 * D_C)),
        _const_spec((CONV_W, D_C)),
        _const_spec((D_C, D_MODEL)),
    ]
    args = [x, shift, scale, gate, g, w_in, conv_w, w_out]
    out_shape = [jax.ShapeDtypeStruct((T, D_MODEL), F32)]
    out_specs = [pl.BlockSpec((tm, D_MODEL), lambda i: (i, 0))]
    scratch = []
    if per_row:
        in_specs += [pl.BlockSpec((tm, D_C), lambda i: (i, 0))] * 2
        args += list(fills)
        out_shape.append(jax.ShapeDtypeStruct((T, D_C), F32))
        out_specs.append(pl.BlockSpec((tm, D_C), lambda i: (i, 0)))
        sem = "parallel"
    else:
        nb = T // rows_per_batch
        out_shape.append(jax.ShapeDtypeStruct((nb, SUBLANES, D_C), F32))
        out_specs.append(pl.BlockSpec((1, SUBLANES, D_C), lambda i: (i // tpb, 0, 0)))
        scratch.append(pltpu.VMEM((SUBLANES, D_C), F32))
        sem = "arbitrary"
    return pl.pallas_call(
        functools.partial(_conv_kernel, tm=tm, per_row=per_row, tiles_per_batch=tpb),
        grid=(T // tm,),
        in_specs=in_specs,
        out_specs=out_specs,
        out_shape=out_shape,
        scratch_shapes=scratch,
        compiler_params=_cparams(sem),
        name="conv_mixer",
    )(*args)


def _qkv_kernel(x_ref, sh_ref, sc_ref, g_ref, w_ref, qg_ref, kg_ref, q_ref, k_ref, v_ref, *extra,
                tm, emit_attn_inputs):
    h = _modulate(x_ref[...], g_ref[...], _rows(sh_ref), _rows(sc_ref)).astype(BF16)
    qkv = _dot(h, w_ref[...])
    hd = HEAD_DIM_B
    for hh in range(N_HEADS_B):
        cols = slice(hh * hd, (hh + 1) * hd)
        q = qkv[:, hh * hd:(hh + 1) * hd]
        k = qkv[:, D_MODEL + hh * hd:D_MODEL + (hh + 1) * hd]
        q_ref[:, cols] = q * lax.rsqrt(jnp.mean(q * q, axis=-1, keepdims=True) + EPS) * qg_ref[...]
        kn = k * lax.rsqrt(jnp.mean(k * k, axis=-1, keepdims=True) + EPS) * kg_ref[...]
        k_ref[:, cols] = kn
        if emit_attn_inputs:
            kb_ref, vb_ref, km_ref = extra
            kb_ref[:, cols] = kn.astype(BF16)
            for c in range(tm // BLOCK_B):
                km_ref[c, :, cols] = jnp.sum(kn[c * BLOCK_B:(c + 1) * BLOCK_B], axis=0, keepdims=True) * (
                    1.0 / BLOCK_B)
    v = qkv[:, 2 * D_MODEL:]
    v_ref[...] = v
    if emit_attn_inputs:
        extra[1][...] = v.astype(BF16)


def _qkv_layer(x, shift, scale, g, w_qkv, q_gain, k_gain, per_row, rows_per_batch, emit_attn_inputs):
    T = x.shape[0]
    tm = TM_PROJ
    tpb = rows_per_batch // tm if not per_row else 1
    ms = _mod_spec(per_row, tm, tpb, 1)
    row_spec = pl.BlockSpec((tm, D_MODEL), lambda i: (i, 0))
    out_specs = [row_spec, row_spec, row_spec]
    out_shape = [jax.ShapeDtypeStruct((T, D_MODEL), F32)] * 3
    if emit_attn_inputs:
        out_specs += [row_spec, row_spec,
                      pl.BlockSpec((tm // BLOCK_B, 1, D_MODEL), lambda i: (i, 0, 0))]
        out_shape += [jax.ShapeDtypeStruct((T, D_MODEL), BF16)] * 2
        out_shape += [jax.ShapeDtypeStruct((T // BLOCK_B, 1, D_MODEL), F32)]
    return pl.pallas_call(
        functools.partial(_qkv_kernel, tm=tm, emit_attn_inputs=emit_attn_inputs),
        grid=(T // tm,),
        in_specs=[row_spec, ms, ms,
                  _const_spec((1, D_MODEL)),
                  _const_spec((D_MODEL, 3 * D_MODEL)),
                  _const_spec((1, HEAD_DIM_B)),
                  _const_spec((1, HEAD_DIM_B))],
        out_specs=out_specs,
        out_shape=out_shape,
        compiler_params=_cparams("parallel"),
        name="moba_qkv",
    )(x, shift, scale, g, w_qkv, q_gain, k_gain)


def _topk_mask_t(gate_t, cands, blk_of_row, elig):
    gm = jnp.where(elig, gate_t, MASK_VALUE)
    cnt = jnp.zeros(gate_t.shape, F32)
    for r, cb in cands:
        row = gm[r:r + 1, :]
        beats = jnp.where(row > gm, 1.0, jnp.where(row == gm, jnp.where(cb < blk_of_row, 1.0, 0.0), 0.0))
        cnt = cnt + beats
    return jnp.where(elig, jnp.where(cnt < TOPK_B, 1.0, 0.0), 0.0)


def _moba_prompt_kernel(x_ref, gt_ref, wout_ref, q_ref, kb_ref, vb_ref, km_ref, bias_ref, o_ref, qaug_scr, *,
                        nblk, nh):
    qb = pl.program_id(1)
    hd = HEAD_DIM_B
    W = BLOCK_B
    n_near = bias_ref.shape[2] // W
    dead_lane = LANES - 1
    lane = lax.broadcasted_iota(jnp.int32, (W, LANES), 1)
    blk = lax.broadcasted_iota(jnp.int32, (nblk, W), 0)
    r_i = lax.broadcasted_iota(jnp.int32, (W, W), 0)
    c_i = lax.broadcasted_iota(jnp.int32, (W, W), 1)
    causal = r_i >= c_i

    def onehot(b, valid):
        return jnp.where(lane == jnp.where(valid, b, dead_lane), 1.0, 0.0).astype(BF16)

    def rows_of(b):
        return pl.ds(pl.multiple_of(b * W, W), W)

    near_blk = [jnp.maximum(qb - d, 0) for d in range(n_near)]
    oh_near = jnp.concatenate([onehot(qb - d, qb - d >= 0) for d in range(n_near)], axis=0)
    ones_near = jnp.ones((n_near * W, hd), BF16)
    heads = [slice(h * hd, (h + 1) * hd) for h in range(nh)]
    gates = [_dot_nt(km_ref[:, 0, cols].astype(BF16), q_ref[:, cols].astype(BF16)) for cols in heads]
    scores = []
    for h, cols in enumerate(heads):
        sel_t = _topk_mask_t(gates[h], [(j, j) for j in range(nblk)], blk, blk < qb)
        neg_t = jnp.where(blk == qb, 0.0, jnp.where(sel_t > 0.5, 0.0, MASK_VALUE))
        neg = jnp.concatenate([neg_t, jnp.zeros((LANES - nblk - 1, W), F32),
                               jnp.full((1, W), MASK_VALUE, F32)], axis=0).T
        q_aug = jnp.concatenate([(q_ref[:, cols] * (LOG2E / math.sqrt(hd))).astype(BF16), neg.astype(BF16)],
                                axis=1)
        qaug_scr[h] = q_aug
        k_near = jnp.concatenate([kb_ref[rows_of(b), cols] for b in near_blk], axis=0)
        scores.append(_dot_nt(q_aug, jnp.concatenate([k_near, oh_near], axis=1)))
    carry0 = []
    for h, cols in enumerate(heads):
        s = scores[h] + bias_ref[h]
        s = jnp.concatenate([jnp.where(causal, s[:, :W], MASK_VALUE), s[:, W:]], axis=1)
        m0 = jnp.max(s, axis=-1, keepdims=True)
        p = jnp.exp2(s - m0)
        v_near = jnp.concatenate([vb_ref[rows_of(b), cols] for b in near_blk], axis=0)
        carry0 += [m0, _dot(p.astype(BF16), jnp.concatenate([v_near, ones_near], axis=1))]

    n_far = jnp.maximum(qb - (n_near - 1), 0)
    ones_pair = jnp.ones((2 * W, hd), BF16)

    def body(i, carry):
        b0 = 2 * i
        b1 = b0 + 1
        oh = jnp.concatenate([onehot(b0, True), onehot(b1, b1 < n_far)], axis=0)
        pair_scores = []
        for h, cols in enumerate(heads):
            k_pair = jnp.concatenate([kb_ref[rows_of(b0), cols], kb_ref[rows_of(b1), cols]], axis=0)
            pair_scores.append(_dot_nt(qaug_scr[h], jnp.concatenate([k_pair, oh], axis=1)))
        out = []
        for h, cols in enumerate(heads):
            m, acc = carry[2 * h], carry[2 * h + 1]
            s = pair_scores[h]
            v_pair = jnp.concatenate([vb_ref[rows_of(b0), cols], vb_ref[rows_of(b1), cols]], axis=0)
            m_new = jnp.maximum(m, jnp.max(s, axis=-1, keepdims=True))
            p = jnp.exp2(s - m_new)
            acc = jnp.exp2(m - m_new) * acc + _dot(p.astype(BF16), jnp.concatenate([v_pair, ones_pair], axis=1))
            out += [m_new, acc]
        return tuple(out)

    carry = lax.fori_loop(0, (n_far + 1) // 2, body, tuple(carry0))
    o = jnp.concatenate([(carry[2 * h + 1][:, :hd] / carry[2 * h + 1][:, hd:]).astype(BF16) for h in range(nh)],
                        axis=1)
    o_ref[...] = x_ref[...] + _rows(gt_ref) * _dot(o, wout_ref[...])


def _moba_prompt_attention(x, gate, w_out, q, kb, vb, kmean, bias_near, batch, seq):
    nq = seq // BLOCK_B
    hd = HEAD_DIM_B
    nh = N_HEADS_B
    one = pl.Buffered(1)
    row_spec = pl.BlockSpec((BLOCK_B, D_MODEL), lambda b, i: (b * nq + i, 0))
    return pl.pallas_call(
        functools.partial(_moba_prompt_kernel, nblk=nq, nh=nh),
        grid=(batch, nq),
        in_specs=[
            row_spec,
            pl.BlockSpec((1, 1, D_MODEL), lambda b, i: (b, 0, 0)),
            pl.BlockSpec((D_MODEL, D_MODEL), lambda b, i: (0, 0), pipeline_mode=one),
            row_spec,
            pl.BlockSpec((seq, nh * hd), lambda b, i: (b, 0), pipeline_mode=one),
            pl.BlockSpec((seq, nh * hd), lambda b, i: (b, 0), pipeline_mode=one),
            pl.BlockSpec((nq, 1, nh * hd), lambda b, i: (b, 0, 0)),
            pl.BlockSpec(bias_near.shape, lambda b, i: (0, 0, 0), pipeline_mode=one),
        ],
        out_specs=row_spec,
        out_shape=jax.ShapeDtypeStruct((batch * seq, D_MODEL), F32),
        scratch_shapes=[pltpu.VMEM((nh, BLOCK_B, 2 * hd), BF16)],
        compiler_params=_cparams("parallel", "arbitrary"),
        name="moba_prompt_attn",
    )(x, gate, w_out, q, kb, vb, kmean, bias_near)


def _moba_sample_kernel(pt_ref, q_ref, kn_ref, vn_ref, *refs, pps, n_pages, t_new):
    kc_refs = refs[:pps]
    vc_refs = refs[pps:2 * pps]
    bias_ref, bias_own_ref = refs[2 * pps:2 * pps + 2]
    o_ref = refs[2 * pps + 2]
    qs_scr, qs2_scr, qg_scr, m_scr, l_scr, o_scr, ksum_scr = refs[2 * pps + 3:]
    del pt_ref
    s_id = pl.program_id(1)
    H, hd = N_HEADS_B, HEAD_DIM_B
    nrow = H * t_new
    ppb = BLOCK_B // PAGE_SIZE
    assert ppb == 2
    n_blk = n_pages // ppb
    page_rows = PAGE_SIZE * H

    @pl.when(s_id == 0)
    def _():
        q8 = q_ref[0]
        qall = jnp.concatenate([q8[:, h * hd:(h + 1) * hd] for h in range(H)], axis=0)
        qg_scr[...] = qall.astype(BF16)
        qs = (qall * (1.0 / math.sqrt(hd))).astype(BF16)
        qs_scr[...] = qs
        zero = jnp.zeros((nrow, hd), BF16)
        qs2_scr[...] = jnp.concatenate([jnp.concatenate([qs, zero], axis=1),
                                        jnp.concatenate([zero, qs], axis=1)], axis=0)

    own_head = (lax.broadcasted_iota(jnp.int32, (H, nrow), 0)
                == lax.broadcasted_iota(jnp.int32, (H, nrow), 1) // t_new)
    own_head2 = (lax.broadcasted_iota(jnp.int32, (H, ppb * nrow), 0)
                 == (lax.broadcasted_iota(jnp.int32, (H, ppb * nrow), 1) % nrow) // t_new)

    def block_scores(k_pages, bias):
        k2 = jnp.concatenate(k_pages, axis=1).astype(BF16)
        return _dot_nt(k2, qs2_scr[...]) + bias

    def block_attn(g, v_pages):
        v2 = jnp.concatenate(v_pages, axis=1).astype(BF16)
        g3 = g.reshape(PAGE_SIZE, H, ppb * nrow)
        m_t = jnp.max(g3, axis=0)
        m_t = jnp.maximum(m_t, pltpu.roll(m_t, nrow, 1))
        e3 = jnp.exp(g3 - jnp.where(own_head2, m_t, 0.0)[None])
        l_t = jnp.sum(e3, axis=0)
        l_t = l_t + pltpu.roll(l_t, nrow, 1)
        o2 = _dot_tn(e3.reshape(PAGE_SIZE * H, ppb * nrow).astype(BF16), v2)
        return m_t, l_t, o2[:nrow, :hd] + o2[nrow:, hd:]

    def partial_attn(k_rows, v_rows, bias, valid):
        n = k_rows.shape[0]
        g = _dot_nt(k_rows.astype(BF16), qs_scr[...]) + bias
        if valid is not None:
            g = jnp.where(valid, g, MASK_VALUE)
        g3 = g.reshape(n // H, H, nrow)
        m_t = jnp.max(g3, axis=0)
        e3 = jnp.exp(g3 - jnp.where(own_head, m_t, 0.0)[None])
        l_t = jnp.sum(e3, axis=0)
        o = _dot_tn(e3.reshape(n, nrow).astype(BF16), v_rows.astype(BF16))
        return m_t, l_t, o

    scores = []
    for i in range(pps // ppb):
        blk = s_id * (pps // ppb) + i
        k_pages = [kc_refs[i * ppb + u][0, 0].reshape(page_rows, hd) for u in range(ppb)]
        ksum_scr[blk] = jnp.sum((k_pages[0] + k_pages[1]).reshape(PAGE_SIZE, H, hd), axis=0)
        b0 = pl.multiple_of(blk * page_rows, page_rows)
        scores.append((blk, block_scores(k_pages, bias_ref[pl.ds(b0, page_rows), :])))
    for i, (blk, g) in enumerate(scores):
        v_pages = [vc_refs[i * ppb + u][0, 0].reshape(page_rows, hd) for u in range(ppb)]
        m_t, l_t, o = block_attn(g, v_pages)
        m_scr[blk] = m_t
        l_scr[blk] = l_t
        o_scr[blk] = o

    @pl.when(s_id == pl.num_programs(1) - 1)
    def _():
        def head_diag(t):
            return jnp.sum(jnp.where(own_head[None], t[:, :, :nrow], 0.0), axis=1)

        key_i = lax.broadcasted_iota(jnp.int32, (t_new * H, nrow), 0) // H
        qry_i = lax.broadcasted_iota(jnp.int32, (t_new * H, nrow), 1) % t_new
        m_ot, l_ot, o_o = partial_attn(kn_ref[0].reshape(t_new * H, hd), vn_ref[0].reshape(t_new * H, hd),
                                       bias_own_ref[...], key_i <= qry_i)
        m_o = head_diag(m_ot[None])
        l_o = head_diag(l_ot[None])

        kmean = (ksum_scr[...] * (1.0 / BLOCK_B)).reshape(n_blk * H, hd)
        gate = _dot_nt(kmean.astype(BF16), qg_scr[...])
        gate_t = head_diag(gate.reshape(n_blk, H, nrow))
        bi = lax.broadcasted_iota(jnp.int32, (n_blk, nrow), 0)
        sel = _topk_mask_t(gate_t, [(j, j) for j in range(n_blk)], bi, bi >= 0)

        m_all = head_diag(m_scr[...])
        l_all = head_diag(l_scr[...])
        m_fin = jnp.maximum(jnp.max(jnp.where(sel > 0.5, m_all, MASK_VALUE), axis=0, keepdims=True), m_o)
        w = jnp.where(sel > 0.5, jnp.exp(m_all - m_fin), 0.0)
        w_o = jnp.exp(m_o - m_fin)
        l_fin = jnp.sum(w * l_all, axis=0, keepdims=True) + w_o * l_o
        stack = jnp.concatenate([w, w_o, l_fin], axis=0)
        stack = jnp.concatenate([stack, jnp.zeros((stack.shape[0], LANES - nrow), F32)], axis=1)
        stack = jnp.concatenate([stack, jnp.zeros((LANES - stack.shape[0], LANES), F32)], axis=0)
        st = stack.T
        acc = st[:nrow, n_blk:n_blk + 1] * o_o
        for pp in range(n_blk):
            acc = acc + st[:nrow, pp:pp + 1] * o_scr[pp]
        out = acc / st[:nrow, n_blk + 1:n_blk + 2]
        for h in range(H):
            o_ref[0, :, h * hd:(h + 1) * hd] = out[h * t_new:(h + 1) * t_new, :]


def _moba_sample_attention(q, k_new, v_new, cache_k, cache_v, layer, page_table, bias_past, bias_own, t_new):
    nb, n_pages = page_table.shape
    pps = PAGES_PER_STEP
    n_steps = n_pages // pps
    H, hd = N_HEADS_B, HEAD_DIM_B
    nrow = H * t_new
    n_blk = n_pages * PAGE_SIZE // BLOCK_B

    def page_spec(i):
        return pl.BlockSpec((1, 1, PAGE_SIZE, H, hd),
                            lambda b, s, pt: (layer, pt[b * n_pages + s * pps + i], 0, 0, 0))

    row_spec = pl.BlockSpec((1, t_new, H * hd), lambda b, s, pt: (b, 0, 0))
    own_spec = pl.BlockSpec((1, t_new, H, hd), lambda b, s, pt: (b, 0, 0, 0))
    one = pl.Buffered(1)
    grid_spec = pltpu.PrefetchScalarGridSpec(
        num_scalar_prefetch=1,
        grid=(nb, n_steps),
        in_specs=[row_spec, own_spec, own_spec]
                 + [page_spec(i) for i in range(pps)] + [page_spec(i) for i in range(pps)]
                 + [pl.BlockSpec(bias_past.shape, lambda b, s, pt: (0, 0), pipeline_mode=one),
                    pl.BlockSpec(bias_own.shape, lambda b, s, pt: (0, 0), pipeline_mode=one)],
        out_specs=row_spec,
        scratch_shapes=[
            pltpu.VMEM((nrow, hd), BF16),
            pltpu.VMEM((2 * nrow, 2 * hd), BF16),
            pltpu.VMEM((nrow, hd), BF16),
            pltpu.VMEM((n_blk, H, 2 * nrow), F32),
            pltpu.VMEM((n_blk, H, 2 * nrow), F32),
            pltpu.VMEM((n_blk, nrow, hd), F32),
            pltpu.VMEM((n_blk, H, hd), F32),
        ],
    )
    return pl.pallas_call(
        functools.partial(_moba_sample_kernel, pps=pps, n_pages=n_pages, t_new=t_new),
        grid_spec=grid_spec,
        out_shape=jax.ShapeDtypeStruct((nb, t_new, H * hd), F32),
        compiler_params=_cparams("parallel", "arbitrary"),
        name="moba_sample_attn",
    )(page_table.reshape(-1), q, k_new, v_new, *([cache_k] * pps), *([cache_v] * pps), bias_past, bias_own)


def _proj_kernel(x_ref, gt_ref, a_ref, w_ref, o_ref):
    o_ref[...] = x_ref[...] + _rows(gt_ref) * _dot(a_ref[...].astype(BF16), w_ref[...])


def _proj_residual(x, gate, a, w, per_row, rows_per_batch):
    T = x.shape[0]
    tm = TM_MIXER
    tpb = rows_per_batch // tm if not per_row else 1
    row_spec = pl.BlockSpec((tm, D_MODEL), lambda i: (i, 0))
    return pl.pallas_call(
        _proj_kernel,
        grid=(T // tm,),
        in_specs=[row_spec, _mod_spec(per_row, tm, tpb, 1), row_spec,
                  _const_spec((D_MODEL, D_MODEL))],
        out_specs=row_spec,
        out_shape=jax.ShapeDtypeStruct((T, D_MODEL), F32),
        compiler_params=_cparams("parallel"),
        name="moba_out_proj",
    )(x, gate, a, w)


def _rel_bucket(dist):
    max_exact = N_BUCKETS // 2
    d = jnp.maximum(dist, 0)
    log_ratio = jnp.log(jnp.maximum(d, 1).astype(jnp.float32) / max_exact) / math.log(MAX_DIST / max_exact)
    large = jnp.minimum(max_exact + (log_ratio * (N_BUCKETS - max_exact)).astype(jnp.int32), N_BUCKETS - 1)
    return jnp.where(d < max_exact, d, large)


def _bias_lookup(dist, rel_bias, out_spec):
    onehot = (_rel_bucket(jnp.asarray(dist, jnp.int32))[..., None] == jnp.arange(N_BUCKETS)).astype(F32)
    return jnp.einsum(out_spec, onehot, rel_bias, precision=HIGHEST)


def _bias_tables(rel_bias, past_len, t_new):
    H, W = N_HEADS_B, BLOCK_B
    n_near = MAX_DIST // W + 1
    r = np.arange(W)[:, None]
    col = np.arange(n_near * W)[None, :]
    dist = (col // W) * W + r - (col % W)
    near = _bias_lookup(dist, rel_bias - rel_bias[N_BUCKETS - 1], "rcb,bh->hrc") * LOG2E
    ppb = W // PAGE_SIZE
    shape = (past_len // W * PAGE_SIZE * H, ppb * H * t_new)
    row = lax.broadcasted_iota(jnp.int32, shape, 0)
    col = lax.broadcasted_iota(jnp.int32, shape, 1)
    kpos = ((row // (PAGE_SIZE * H)) * ppb + col // (H * t_new)) * PAGE_SIZE + (row // H) % PAGE_SIZE
    bucket = _rel_bucket(past_len + col % t_new - kpos)
    col_bias = rel_bias[:, (np.arange(shape[1]) // t_new) % H]
    past = jnp.zeros(shape, F32)
    for b in range(N_BUCKETS):
        past = jnp.where(bucket == b, col_bias[b][None, :], past)
    past = jnp.where(row % H == (col // t_new) % H, past, MASK_VALUE)
    qr = np.arange(t_new)
    same_head = np.eye(H, dtype=bool)
    own = _bias_lookup(np.maximum(qr[None, :] - qr[:, None], 0), rel_bias, "trb,bh->thr")
    own = jnp.where(same_head[None, :, :, None], own[:, None], MASK_VALUE).reshape(t_new * H, H * t_new)
    return near, past, own


def kernel(x_prompt, x_sample, cache_k, cache_v, state_conv, page_table, c_prompt, c_sample, rel_bias, norm_mix, norm_mlp, w_mod, b_mod, w_up, w_down, a_w_in, a_v_gain, a_w_s, a_b_s, a_w_out, b_w_qkv, b_q_gain, b_k_gain, b_w_out, c_w_in, c_conv, c_w_out):
    B, S, D = x_prompt.shape
    DB, T, _ = x_sample.shape
    n_pages = page_table.shape[1]
    past_len = n_pages * PAGE_SIZE
    assert S % BLOCK_B == 0 and past_len % BLOCK_B == 0 and T == SUBLANES
    assert MAX_DIST % BLOCK_B == 0

    xp = x_prompt.reshape(B * S, D)
    xs = x_sample.reshape(DB * T, D)

    n_c = B + DB
    n_c_pad = -(-n_c // SUBLANES) * SUBLANES
    c_all = jnp.concatenate([c_prompt, c_sample, jnp.zeros((n_c_pad - n_c, D), F32)], axis=0)
    mods = _ada_all(c_all, w_mod, b_mod)

    bias_near, bias_past, bias_own = _bias_tables(rel_bias, past_len, T)

    ci = np.arange(CHUNK_A)
    tril_p = (ci[:, None] >= ci[None, :])
    tril_s = tril_p & ((ci[:, None] // T) == (ci[None, :] // T))

    k_p, v_p, k_s, v_s, conv_p, conv_s, chunkv_s = [], [], [], [], [], [], []
    for i in range(DEPTH):
        kind, j = i % N_MIXERS, i // N_MIXERS
        mp = [m.reshape(B, 1, D) for m in jnp.split(mods[i, :B], 6, axis=-1)]
        ms = [m.reshape(DB, 1, D) for m in jnp.split(mods[i, B:B + DB], 6, axis=-1)]
        g_mix = norm_mix[i].reshape(1, D)
        g_mlp = norm_mlp[i].reshape(1, D)
        if kind == 0:
            w_in = a_w_in[j].astype(BF16)
            w_out = a_w_out[j].astype(BF16)
            vg = a_v_gain[j].reshape(1, D_A)
            wmix_p = jnp.where(tril_p, a_w_s[j], 0.0).astype(BF16)
            bmix_p = jnp.repeat(jnp.transpose(a_b_s[j]), GROUP_A, axis=1)
            ws_t = jnp.tile(a_w_s[j][:, :T, :T], (1, CHUNK_A // T, CHUNK_A // T))
            wmix_s = jnp.where(tril_s, ws_t, 0.0).astype(BF16)
            bmix_s = jnp.repeat(jnp.tile(jnp.transpose(a_b_s[j][:, :T]), (CHUNK_A // T, 1)), GROUP_A, axis=1)
            (xp,) = _gmlp_layer(xp, mp[0], mp[1], mp[2], g_mix, w_in, vg, wmix_p, bmix_p, w_out,
                                False, S, False)
            xs, v_new = _gmlp_layer(xs, ms[0], ms[1], ms[2], g_mix, w_in, vg, wmix_s, bmix_s, w_out,
                                    True, T, True)
            chunkv_s.append(v_new.reshape(DB, T, D_A))
        elif kind == 1:
            w_qkv = b_w_qkv[j].astype(BF16)
            w_out = b_w_out[j].astype(BF16)
            qg = b_q_gain[j].reshape(1, HEAD_DIM_B)
            kg = b_k_gain[j].reshape(1, HEAD_DIM_B)
            qp, kp, vp, kbp, vbp, kmean_p = _qkv_layer(xp, mp[0], mp[1], g_mix, w_qkv, qg, kg, False, S, True)
            qs, ks, vs = _qkv_layer(xs, ms[0], ms[1], g_mix, w_qkv, qg, kg, True, T, False)
            xp = _moba_prompt_attention(xp, mp[2], w_out, qp, kbp, vbp, kmean_p, bias_near, B, S)
            os_ = _moba_sample_attention(
                qs.reshape(DB, T, D), ks.reshape(DB, T, N_HEADS_B, HEAD_DIM_B),
                vs.reshape(DB, T, N_HEADS_B, HEAD_DIM_B), cache_k, cache_v, j, page_table,
                bias_past, bias_own, T)
            xs = _proj_residual(xs, ms[2], os_.reshape(DB * T, D), w_out, True, T)
            k_p.append(kp.reshape(B, S, N_HEADS_B, HEAD_DIM_B))
            v_p.append(vp.reshape(B, S, N_HEADS_B, HEAD_DIM_B))
            k_s.append(ks.reshape(DB, T, N_HEADS_B, HEAD_DIM_B))
            v_s.append(vs.reshape(DB, T, N_HEADS_B, HEAD_DIM_B))
        else:
            w_in = c_w_in[j].astype(BF16)
            w_out = c_w_out[j].astype(BF16)
            st = state_conv[j]
            zrow = jnp.zeros((DB, T - 1, D_C), F32)
            p1 = jnp.concatenate([st[:, 1:2], zrow], axis=1).reshape(DB * T, D_C)
            p2 = jnp.concatenate([st, zrow[:, 1:]], axis=1).reshape(DB * T, D_C)
            xp, tail_p = _conv_layer(xp, mp[0], mp[1], mp[2], g_mix, w_in, c_conv[j], w_out, False, S)
            xs, xin_s = _conv_layer(xs, ms[0], ms[1], ms[2], g_mix, w_in, c_conv[j], w_out, True, T,
                                    fills=(p1, p2))
            conv_p.append(tail_p[:, SUBLANES - (CONV_W - 1):, :])
            conv_s.append(xin_s.reshape(DB, T, D_C)[:, T - (CONV_W - 1):, :])
        wu = w_up[i].astype(BF16)
        wd = w_down[i].astype(BF16)
        xp = _mlp_layer(xp, mp[3], mp[4], mp[5], g_mlp, wu, wd, False, S)
        xs = _mlp_layer(xs, ms[3], ms[4], ms[5], g_mlp, wu, wd, True, T)
    return (xp.reshape(B, S, D), xs.reshape(DB, T, D), jnp.stack(k_p), jnp.stack(v_p), jnp.stack(k_s),
            jnp.stack(v_s), jnp.stack(conv_p), jnp.stack(conv_s), jnp.stack(chunkv_s))
```
